```python
import numpy as np
import jax
import jax.numpy as jnp
from jax import lax

D_MODEL = 1024
BATCH = 16
SEQ = 2048
DEPTH = 1

N_META = 16
LEAD = 128
N_PAD = LEAD - N_META
BLOCK = 128

HG_HEADS = 8
HG_DK = 128
HG_DV = 128
HG_WIDTH = HG_HEADS * HG_DK
HG_CHUNK = 16

MLA_HEADS = 16
MLA_NOPE = 128
MLA_ROPE = 64
MLA_V = 128
MLA_Q_RANK = 384
MLA_KV_RANK = 256
MLA_SCALE = (MLA_NOPE + MLA_ROPE) ** -0.5
ROPE_BASE = 10000.0

PEER_HEADS = 8
PEER_NKEYS = 128
PEER_N = PEER_NKEYS * PEER_NKEYS
PEER_HALF = 128
PEER_QDIM = 2 * PEER_HALF
PEER_TOPK = 16

ALPHA = (2 * DEPTH) ** 0.25
BETA = (8 * DEPTH) ** -0.25
EPS = 1e-5

IN_SIZES = (HG_WIDTH, HG_WIDTH, HG_WIDTH, HG_WIDTH, MLA_Q_RANK, MLA_KV_RANK, MLA_ROPE, D_MODEL, D_MODEL)
IN_WIDTH = 4 * HG_WIDTH + MLA_Q_RANK + MLA_KV_RANK + MLA_ROPE + 2 * D_MODEL

kernel_name = 'hybrid_hgrn2_mla_peer_block'


def layer_norm(t, gain, bias):
    tf = t.astype(jnp.float32)
    mu = jnp.mean(tf, axis=-1, keepdims=True)
    var = jnp.mean(jnp.square(tf - mu), axis=-1, keepdims=True)
    return ((tf - mu) * lax.rsqrt(var + EPS) * gain + bias).astype(t.dtype)


def rms_norm(t, gain):
    tf = t.astype(jnp.float32)
    ms = jnp.mean(jnp.square(tf), axis=-1, keepdims=True)
    return (tf * lax.rsqrt(ms + EPS) * gain).astype(t.dtype)


def rope_tables(pos):
    half = MLA_ROPE // 2
    inv_freq = ROPE_BASE ** (-jnp.arange(half, dtype=jnp.float32) / half)
    ang = pos.astype(jnp.float32)[:, None] * inv_freq[None, :]
    return jnp.cos(ang), jnp.sin(ang)


def apply_rope(t, cos, sin):
    half = MLA_ROPE // 2
    tf = t.astype(jnp.float32)
    t1, t2 = tf[..., :half], tf[..., half:]
    return jnp.concatenate([t1 * cos - t2 * sin, t2 * cos + t1 * sin], axis=-1).astype(t.dtype)


def hgrn2_branch(q, f_logit, inp, gate, lower_bound, valid, norm_gain):
    B, P, _ = q.shape
    n_chunks = P // HG_CHUNK
    f32 = jnp.float32
    f = lower_bound + (1.0 - lower_bound) * jax.nn.sigmoid(f_logit.astype(f32))
    vmask = valid[None, :, None]
    log_f = jnp.where(vmask, jnp.log(f), 0.0)
    k = jnp.where(vmask, 1.0 - f, 0.0)

    def chunks(t):
        return t.reshape(B, n_chunks, HG_CHUNK, HG_HEADS, -1).transpose(0, 3, 1, 2, 4)

    qc = chunks(q.astype(f32))
    kc = chunks(k)
    ic = chunks(inp.astype(f32))
    bc = jnp.cumsum(chunks(log_f), axis=3)
    b_last = bc[:, :, :, -1:, :]
    q_dec = qc * jnp.exp(bc)
    k_intra = kc * jnp.exp(-bc)
    k_state = kc * jnp.exp(b_last - bc)

    causal = jnp.tril(jnp.ones((HG_CHUNK, HG_CHUNK), dtype=bool))
    scores = jnp.where(causal, jnp.einsum('bhnck,bhnsk->bhncs', q_dec, k_intra), 0.0)
    o_intra = jnp.einsum('bhncs,bhnsv->bhncv', scores, ic)

    def step(state, xs):
        qd, ks, iv, dec = xs
        o = jnp.einsum('bhck,bhkv->bhcv', qd, state)
        state = dec[..., None] * state + jnp.einsum('bhck,bhcv->bhkv', ks, iv)
        return state, o

    s0 = jnp.zeros((B, HG_HEADS, HG_DK, HG_DV), f32)
    xs = (jnp.moveaxis(q_dec, 2, 0), jnp.moveaxis(k_state, 2, 0), jnp.moveaxis(ic, 2, 0),
          jnp.moveaxis(jnp.exp(b_last[:, :, :, 0, :]), 2, 0))
    _, o_inter = lax.scan(step, s0, xs)
    o = o_intra + jnp.moveaxis(o_inter, 0, 2)
    o = o.transpose(0, 2, 3, 1, 4).reshape(B, P, HG_HEADS, HG_DV)
    o = rms_norm(o, norm_gain.reshape(HG_HEADS, HG_DV)).reshape(B, P, HG_WIDTH)
    return (o * jax.nn.silu(gate.astype(f32))).astype(q.dtype)


def mla_branch(c_q, c_kv, k_rope, q_norm_gain, kv_norm_gain, w_uq, w_ukv, cos, sin):
    B, P, _ = c_q.shape
    q = (rms_norm(c_q, q_norm_gain) @ w_uq).reshape(B, P, MLA_HEADS, MLA_NOPE + MLA_ROPE)
    q_nope = q[..., :MLA_NOPE]
    q_rope = apply_rope(q[..., MLA_NOPE:], cos[:, None, :], sin[:, None, :])
    kv = (rms_norm(c_kv, kv_norm_gain) @ w_ukv).reshape(B, P, MLA_HEADS, MLA_NOPE + MLA_V)
    k_nope, v = kv[..., :MLA_NOPE], kv[..., MLA_NOPE:]
    k_r = apply_rope(k_rope, cos, sin)

    n_blocks = P // BLOCK
    qn_b = q_nope.reshape(B, n_blocks, BLOCK, MLA_HEADS, MLA_NOPE).transpose(1, 0, 3, 2, 4)
    qr_b = q_rope.reshape(B, n_blocks, BLOCK, MLA_HEADS, MLA_ROPE).transpose(1, 0, 3, 2, 4)
    qpos_b = jnp.arange(P).reshape(n_blocks, BLOCK)
    kpos = jnp.arange(P)
    key_real = kpos >= N_PAD

    def attend(args):
        qn, qr, qpos = args
        s = (jnp.einsum('bhqd,bkhd->bhqk', qn, k_nope, preferred_element_type=jnp.float32)
             + jnp.einsum('bhqr,bkr->bhqk', qr, k_r, preferred_element_type=jnp.float32)) * MLA_SCALE
        allowed = (kpos[None, :] <= qpos[:, None]) & (key_real[None, :] | (kpos[None, :] == qpos[:, None]))
        s = jnp.where(allowed, s, -jnp.inf)
        p = jax.nn.softmax(s, axis=-1).astype(v.dtype)
        return jnp.einsum('bhqk,bkhd->bqhd', p, v)

    out = lax.map(attend, (qn_b, qr_b, qpos_b))
    return out.transpose(1, 0, 2, 3, 4).reshape(B, P, MLA_HEADS * MLA_V)


def peer_ffn(h, w_query, sub_keys, u_table, v_table):
    B, P, D = h.shape
    T = B * P
    ht = h.reshape(T, D)
    q = (ht @ w_query).reshape(T, PEER_HEADS, 2, PEER_HALF)
    s = jnp.einsum('thpd,hpnd->thpn', q, sub_keys, preferred_element_type=jnp.float32)
    s_top, i_top = lax.top_k(s, PEER_TOPK)
    cand = s_top[:, :, 0, :, None] + s_top[:, :, 1, None, :]
    cand_idx = i_top[:, :, 0, :, None] * PEER_NKEYS + i_top[:, :, 1, None, :]
    best, sel = lax.top_k(cand.reshape(T, PEER_HEADS, PEER_TOPK * PEER_TOPK), PEER_TOPK)
    experts = jnp.take_along_axis(cand_idx.reshape(T, PEER_HEADS, PEER_TOPK * PEER_TOPK), sel, axis=-1)
    gates = jax.nn.softmax(best, axis=-1)
    n_blk = T // BLOCK
    n_sel = PEER_HEADS * PEER_TOPK

    def apply(args):
        xb, eb, gb = args
        u = jnp.take(u_table, eb, axis=0)
        a = jnp.einsum('tkd,td->tk', u, xb, preferred_element_type=jnp.float32)
        act = (jax.nn.gelu(a) * gb).astype(xb.dtype)
        vv = jnp.take(v_table, eb, axis=0)
        return jnp.einsum('tk,tkd->td', act, vv)

    out = lax.map(apply, (ht.reshape(n_blk, BLOCK, D), experts.reshape(n_blk, BLOCK, n_sel),
                          gates.reshape(n_blk, BLOCK, n_sel)))
    return out.reshape(B, P, D)


def setup_inputs(seed: int = 0) -> dict:
    key = jax.random.key(seed)
    ks = jax.random.split(key, 20)
    nrm = jax.random.normal
    f32 = jnp.float32
    L = DEPTH
    return {
        'x': nrm(ks[0], (BATCH, SEQ, D_MODEL), f32),
        'meta_tokens': nrm(ks[1], (N_META, D_MODEL), f32),
        'hgrn_lb_logits': 0.1 * nrm(ks[2], (DEPTH + 1, HG_WIDTH), f32),
        'w_in': nrm(ks[3], (L, D_MODEL, IN_WIDTH), f32) * D_MODEL ** -0.5,
        'q_norm_gain': 1.0 + 0.01 * nrm(ks[4], (L, MLA_Q_RANK), f32),
        'kv_norm_gain': 1.0 + 0.01 * nrm(ks[5], (L, MLA_KV_RANK), f32),
        'w_uq': nrm(ks[6], (L, MLA_Q_RANK, MLA_HEADS * (MLA_NOPE + MLA_ROPE)), f32) * MLA_Q_RANK ** -0.5,
        'w_ukv': nrm(ks[7], (L, MLA_KV_RANK, MLA_HEADS * (MLA_NOPE + MLA_V)), f32) * MLA_KV_RANK ** -0.5,
        'hgrn_norm_gain': 1.0 + 0.01 * nrm(ks[8], (L, HG_WIDTH), f32),
        'w_proj_hgrn': nrm(ks[9], (L, HG_WIDTH, D_MODEL), f32) * HG_WIDTH ** -0.5,
        'w_proj_mla': nrm(ks[10], (L, MLA_HEADS * MLA_V, D_MODEL), f32) * (MLA_HEADS * MLA_V) ** -0.5,
        'w_out': nrm(ks[11], (L, D_MODEL, D_MODEL), f32) * (BETA * D_MODEL ** -0.5),
        'ln1_gain': 1.0 + 0.01 * nrm(ks[12], (L, D_MODEL), f32),
        'ln1_bias': 0.01 * nrm(ks[13], (L, D_MODEL), f32),
        'peer_query': nrm(ks[14], (L, D_MODEL, PEER_HEADS * PEER_QDIM), f32) * D_MODEL ** -0.5,
        'peer_sub_keys': nrm(ks[15], (L, PEER_HEADS, 2, PEER_NKEYS, PEER_HALF), f32) * PEER_HALF ** -0.5,
        'peer_u': nrm(ks[16], (L, PEER_N, D_MODEL), f32) * D_MODEL ** -0.5,
        'peer_v': nrm(ks[17], (L, PEER_N, D_MODEL), f32) * BETA,
        'ln2_gain': 1.0 + 0.01 * nrm(ks[18], (L, D_MODEL), f32),
        'ln2_bias': 0.01 * nrm(ks[19], (L, D_MODEL), f32),
    }


def reference(x, meta_tokens, hgrn_lb_logits, w_in, q_norm_gain, kv_norm_gain, w_uq, w_ukv,
              hgrn_norm_gain, w_proj_hgrn, w_proj_mla, w_out, ln1_gain, ln1_bias,
              peer_query, peer_sub_keys, peer_u, peer_v, ln2_gain, ln2_bias):
    B, S, D = x.shape
    P = LEAD + S
    pad = jnp.zeros((B, N_PAD, D), x.dtype)
    meta = jnp.broadcast_to(meta_tokens.astype(x.dtype)[None], (B, N_META, D))
    h = jnp.concatenate([pad, meta, x], axis=1)
    rows = jnp.arange(P)
    valid = rows >= N_PAD
    cos, sin = rope_tables(rows - N_PAD)
    lower_bounds = jnp.cumsum(jax.nn.softmax(hgrn_lb_logits.astype(jnp.float32), axis=0), axis=0)
    offsets = [int(o) for o in np.cumsum(IN_SIZES)[:-1]]

    for l in range(DEPTH):
        proj = h @ w_in[l]
        hq, hf, hi, hg, cq, ckv, kr, ga, gb = jnp.split(proj, offsets, axis=-1)
        ya = hgrn2_branch(hq, hf, hi, hg, lower_bounds[l], valid, hgrn_norm_gain[l]) @ w_proj_hgrn[l]
        yb = mla_branch(cq, ckv, kr, q_norm_gain[l], kv_norm_gain[l], w_uq[l], w_ukv[l], cos, sin) @ w_proj_mla[l]
        mixed = (jax.nn.sigmoid(ga) * ya + jax.nn.sigmoid(gb) * yb) @ w_out[l]
        h = layer_norm(ALPHA * h + mixed, ln1_gain[l], ln1_bias[l])
        ffn = peer_ffn(h, peer_query[l], peer_sub_keys[l], peer_u[l], peer_v[l])
        h = layer_norm(ALPHA * h + ffn, ln2_gain[l], ln2_bias[l])

    return h[:, LEAD:]
```

```python
import functools

import numpy as np
import jax
import jax.numpy as jnp
from jax import lax
from jax.experimental import pallas as pl
from jax.experimental.pallas import tpu as pltpu

F32 = jnp.float32
BF16 = jnp.bfloat16

D_MODEL = 1024
DEPTH = 1
N_META = 16
LEAD = 128
N_PAD = LEAD - N_META

HG_HEADS = 8
HG_D = 128
HG_WIDTH = HG_HEADS * HG_D
HG_SUB = 16

MLA_HEADS = 16
MLA_NOPE = 128
MLA_ROPE = 64
MLA_V = 128
MLA_Q_RANK = 384
MLA_KV_RANK = 256
MLA_QK = 256
MLA_SCALE = (MLA_NOPE + MLA_ROPE) ** -0.5
ROPE_BASE = 10000.0

PEER_HEADS = 8
PEER_NKEYS = 128
PEER_N = PEER_NKEYS * PEER_NKEYS
PEER_HALF = 128
PEER_TOPK = 16

ALPHA = (2 * DEPTH) ** 0.25
EPS = 1e-5
NEG = -1e30

VMEM_LIMIT = 56 * 1024 * 1024


def _cparams(sem):
    return pltpu.CompilerParams(dimension_semantics=sem, vmem_limit_bytes=VMEM_LIMIT)


def _pick(n, pref):
    t = min(n, pref)
    while n % t:
        t -= 128
    return t


def _mm_kernel(x_ref, w_ref, o_ref):
    o_ref[...] = jnp.dot(x_ref[...], w_ref[...], preferred_element_type=F32).astype(o_ref.dtype)


def _matmul(x, w, tm, tn):
    m, k = x.shape
    out_dtype = x.dtype
    n = w.shape[1]
    return pl.pallas_call(
        _mm_kernel,
        grid=(m // tm, n // tn),
        in_specs=[pl.BlockSpec((tm, k), lambda i, j: (i, 0)),
                  pl.BlockSpec((k, tn), lambda i, j: (0, j))],
        out_specs=pl.BlockSpec((tm, tn), lambda i, j: (i, j)),
        out_shape=jax.ShapeDtypeStruct((m, n), out_dtype),
        compiler_params=_cparams(("parallel", "parallel")),
        name="in_proj",
    )(x, w)


def _hgrn_chunk(q, k, lf, iv, st, tri, masks, chunk):
    hi = lf.astype(BF16)
    r1 = lf - hi.astype(F32)
    mid = r1.astype(BF16)
    lo = (r1 - mid.astype(F32)).astype(BF16)
    g = (jnp.dot(tri, hi, preferred_element_type=F32)
         + jnp.dot(tri, mid, preferred_element_type=F32)
         + jnp.dot(tri, lo, preferred_element_type=F32))

    def bcast_row(r, n):
        return jnp.broadcast_to(g[r:r + 1, :], (n, HG_D))

    def prev_end(s):
        parts = [jnp.zeros((s, HG_D), F32)] + [bcast_row(b * s - 1, s) for b in range(1, chunk // s)]
        return parts[0] if len(parts) == 1 else jnp.concatenate(parts, axis=0)

    def own_end(s):
        parts = [bcast_row((b + 1) * s - 1, s) for b in range(chunk // s)]
        return parts[0] if len(parts) == 1 else jnp.concatenate(parts, axis=0)

    nt = (((1,), (1,)), ((), ()))
    tn = (((0,), (0,)), ((), ()))

    q_c = (q * jnp.exp(g)).astype(BF16)
    o = lax.dot_general(q_c, st.astype(BF16), nt, preferred_element_type=F32)
    k_c = (k * jnp.exp(own_end(chunk) - g)).astype(BF16)
    upd = lax.dot_general(iv, k_c, tn, preferred_element_type=F32)
    st_new = st * jnp.exp(g[chunk - 1:chunk, :]) + upd

    p16 = prev_end(HG_SUB)
    q_d = (q * jnp.exp(g - p16)).astype(BF16)
    k_d = (k * jnp.exp(p16 - g)).astype(BF16)
    a = jnp.where(masks[0], lax.dot_general(q_d, k_d, nt, preferred_element_type=F32), 0.0)
    s = HG_SUB
    lvl = 1
    while s < chunk:
        q_s = q_d if s == HG_SUB else (q * jnp.exp(g - prev_end(s))).astype(BF16)
        k_s = (k * jnp.exp(own_end(s) - g)).astype(BF16)
        a = jnp.where(masks[lvl], lax.dot_general(q_s, k_s, nt, preferred_element_type=F32), a)
        s *= 2
        lvl += 1
    o = o + jnp.dot(a.astype(BF16), iv, preferred_element_type=F32)
    return o, st_new


def _hgrn_kernel(q_ref, f_ref, i_ref, g_ref, lb_ref, gain_ref, tri_ref, o_ref, st_ref, *, chunk, rows):
    c = pl.program_id(1)

    @pl.when(c == 0)
    def _():
        st_ref[...] = jnp.zeros_like(st_ref)

    ri = lax.broadcasted_iota(jnp.int32, (chunk, chunk), 0)
    ci = lax.broadcasted_iota(jnp.int32, (chunk, chunk), 1)
    blk = lambda v, s: lax.shift_right_logical(v, s.bit_length() - 1)
    masks = [(blk(ri, HG_SUB) == blk(ci, HG_SUB)) & (ci <= ri)]
    s = HG_SUB
    while s < chunk:
        masks.append((blk(ri, 2 * s) == blk(ci, 2 * s)) & ((blk(ri, s) & 1) == 1) & ((blk(ci, s) & 1) == 0))
        s *= 2
    tri = tri_ref[...]

    for n in range(rows // chunk):
        rs = slice(n * chunk, (n + 1) * chunk)
        seq_row = c * rows + n * chunk + lax.broadcasted_iota(jnp.int32, (chunk, HG_D), 0)
        valid = seq_row >= N_PAD
        for h in range(HG_HEADS):
            hs = slice(h * HG_D, (h + 1) * HG_D)
            lb = lb_ref[:, hs]
            f = lb + (1.0 - lb) * jax.nn.sigmoid(f_ref[rs, hs].astype(F32))
            lf = jnp.where(valid, jnp.log(f), 0.0)
            k = jnp.where(valid, 1.0 - f, 0.0)
            q = q_ref[rs, hs].astype(F32)
            o, st_new = _hgrn_chunk(q, k, lf, i_ref[rs, hs], st_ref[h], tri, masks, chunk)
            st_ref[h] = st_new
            ms = jnp.mean(o * o, axis=-1, keepdims=True)
            o = o * lax.rsqrt(ms + EPS) * gain_ref[:, hs]
            gate = g_ref[rs, hs].astype(F32)
            o_ref[0, rs, hs] = (o * (gate * jax.nn.sigmoid(gate))).astype(o_ref.dtype)


def _hgrn(hg, lb, gain, batch, p_rows, chunk=64, rows=128):
    nc = p_rows // rows
    lead_blocks = LEAD // rows
    tri = jnp.asarray(np.tril(np.ones((chunk, chunk), np.float32)), BF16)
    col = lambda j: pl.BlockSpec((rows, HG_WIDTH), lambda b, c: (b * nc + c, j))
    vec = pl.BlockSpec((1, HG_WIDTH), lambda b, c: (0, 0))
    return pl.pallas_call(
        functools.partial(_hgrn_kernel, chunk=chunk, rows=rows),
        grid=(batch, nc),
        in_specs=[col(0), col(1), col(2), col(3), vec, vec,
                  pl.BlockSpec((chunk, chunk), lambda b, c: (0, 0))],
        out_specs=pl.BlockSpec((1, rows, HG_WIDTH), lambda b, c: (b, jnp.maximum(c - lead_blocks, 0), 0)),
        out_shape=jax.ShapeDtypeStruct((batch, p_rows - LEAD, HG_WIDTH), BF16),
        scratch_shapes=[pltpu.VMEM((HG_HEADS, HG_D, HG_D), F32)],
        compiler_params=_cparams(("parallel", "arbitrary")),
        name="hgrn2",
    )(hg, hg, hg, hg, lb, gain, tri)


def _rms(x, gain):
    ms = jnp.mean(x * x, axis=-1, keepdims=True)
    return x * lax.rsqrt(ms + EPS) * gain


def _qprep_kernel(cq_ref, gain_ref, w_ref, cos_ref, sin_ref, o_ref):
    xn = _rms(cq_ref[...].astype(F32), gain_ref[...]).astype(BF16)
    cos = cos_ref[...]
    sin = sin_ref[...]
    for h in range(MLA_HEADS):
        y = jnp.dot(xn, w_ref[h], preferred_element_type=F32)
        o_ref[0, h, :, :MLA_NOPE] = (y[:, :MLA_NOPE] * MLA_SCALE).astype(o_ref.dtype)
        roped = y[:, MLA_NOPE:2 * MLA_NOPE] * cos + y[:, 2 * MLA_NOPE:] * sin
        o_ref[0, h, :, MLA_NOPE:] = (roped * MLA_SCALE).astype(o_ref.dtype)


def _qprep(cq, gain, wq, cos, sin, batch, seq):
    tm = _pick(seq, 512)
    nb = seq // tm
    return pl.pallas_call(
        _qprep_kernel,
        grid=(batch, nb),
        in_specs=[pl.BlockSpec((tm, MLA_Q_RANK), lambda b, i: (b * nb + i, 0)),
                  pl.BlockSpec((1, MLA_Q_RANK), lambda b, i: (0, 0)),
                  pl.BlockSpec((MLA_HEADS, MLA_Q_RANK, 3 * MLA_NOPE), lambda b, i: (0, 0, 0)),
                  pl.BlockSpec((tm, MLA_NOPE), lambda b, i: (i, 0)),
                  pl.BlockSpec((tm, MLA_NOPE), lambda b, i: (i, 0))],
        out_specs=pl.BlockSpec((1, MLA_HEADS, tm, MLA_QK), lambda b, i: (b, 0, i, 0)),
        out_shape=jax.ShapeDtypeStruct((batch, MLA_HEADS, seq, MLA_QK), BF16),
        compiler_params=_cparams(("parallel", "parallel")),
        name="mla_q",
    )(cq, gain, wq, cos, sin)


def _kvprep_kernel(kv_ref, gain_ref, w_ref, cos_ref, sin_ref, k_ref, v_ref):
    x = kv_ref[...]
    xn = _rms(x[:, :MLA_KV_RANK].astype(F32), gain_ref[...]).astype(BF16)
    kr = (x[:, MLA_KV_RANK:MLA_KV_RANK + MLA_NOPE].astype(F32) * cos_ref[...]
          + x[:, MLA_KV_RANK + MLA_NOPE:].astype(F32) * sin_ref[...]).astype(k_ref.dtype)
    for h in range(MLA_HEADS):
        y = jnp.dot(xn, w_ref[h], preferred_element_type=F32)
        k_ref[0, h, :, :MLA_NOPE] = y[:, :MLA_NOPE].astype(k_ref.dtype)
        k_ref[0, h, :, MLA_NOPE:] = kr
        v_ref[0, h] = y[:, MLA_NOPE:].astype(v_ref.dtype)


def _kvprep(kvp, gain, wkv, cos, sin, batch, p_rows):
    tm = p_rows // 2
    return pl.pallas_call(
        _kvprep_kernel,
        grid=(batch, 2),
        in_specs=[pl.BlockSpec((tm, MLA_KV_RANK + 2 * MLA_NOPE), lambda b, i: (b * 2 + i, 0)),
                  pl.BlockSpec((1, MLA_KV_RANK), lambda b, i: (0, 0)),
                  pl.BlockSpec((MLA_HEADS, MLA_KV_RANK, MLA_NOPE + MLA_V), lambda b, i: (0, 0, 0)),
                  pl.BlockSpec((tm, MLA_NOPE), lambda b, i: (i, 0)),
                  pl.BlockSpec((tm, MLA_NOPE), lambda b, i: (i, 0))],
        out_specs=[pl.BlockSpec((1, MLA_HEADS, tm, MLA_QK), lambda b, i: (b, 0, i, 0)),
                   pl.BlockSpec((1, MLA_HEADS, tm, MLA_V), lambda b, i: (b, 0, i, 0))],
        out_shape=[jax.ShapeDtypeStruct((batch, MLA_HEADS, p_rows, MLA_QK), BF16),
                   jax.ShapeDtypeStruct((batch, MLA_HEADS, p_rows, MLA_V), BF16)],
        compiler_params=_cparams(("parallel", "parallel")),
        name="mla_kv",
    )(kvp, gain, wkv, cos, sin)


def _flash_kernel(q_ref, k_ref, v_ref, o_ref, *, tq, tk):
    qi = pl.program_id(2)
    q = q_ref[0, 0]
    qpos = LEAD + qi * tq + lax.broadcasted_iota(jnp.int32, (tq, tk), 0)
    kiota = lax.broadcasted_iota(jnp.int32, (tq, tk), 1)
    nblk = (LEAD + (qi + 1) * tq) // tk
    nt = (((1,), (1,)), ((), ()))

    def body(j, carry):
        m, l, acc = carry
        k0 = pl.multiple_of(j * tk, tk)
        kb = k_ref[0, 0, pl.ds(k0, tk), :]
        vb = v_ref[0, 0, pl.ds(k0, tk), :]
        s = lax.dot_general(q, kb, nt, preferred_element_type=F32)
        kpos = k0 + kiota
        s = jnp.where((kpos >= N_PAD) & (kpos <= qpos), s, NEG)
        m_new = jnp.maximum(m, jnp.max(s, axis=-1, keepdims=True))
        p = jnp.exp(s - m_new)
        corr = jnp.exp(m - m_new)
        l = corr * l + jnp.sum(p, axis=-1, keepdims=True)
        acc = corr * acc + jnp.dot(p.astype(BF16), vb, preferred_element_type=F32)
        return m_new, l, acc

    init = (jnp.full((tq, 1), NEG, F32), jnp.zeros((tq, 1), F32), jnp.zeros((tq, MLA_V), F32))
    _, l, acc = lax.fori_loop(0, nblk, body, init)
    o_ref[0] = (acc / l).astype(o_ref.dtype)


def _flash(q, k, v, batch, seq, p_rows, tq=256, tk=128):
    tq = min(tq, seq)
    return pl.pallas_call(
        functools.partial(_flash_kernel, tq=tq, tk=tk),
        grid=(batch, MLA_HEADS, seq // tq),
        in_specs=[pl.BlockSpec((1, 1, tq, MLA_QK), lambda b, h, i: (b, h, i, 0)),
                  pl.BlockSpec((1, 1, p_rows, MLA_QK), lambda b, h, i: (b, h, 0, 0)),
                  pl.BlockSpec((1, 1, p_rows, MLA_V), lambda b, h, i: (b, h, 0, 0))],
        out_specs=pl.BlockSpec((1, tq, MLA_V), lambda b, h, i: (b, i, h)),
        out_shape=jax.ShapeDtypeStruct((batch, seq, MLA_HEADS * MLA_V), BF16),
        compiler_params=_cparams(("parallel", "parallel", "arbitrary")),
        name="mla_flash",
    )(q, k, v)


def _layer_norm(z, gain, bias):
    mu = jnp.mean(z, axis=-1, keepdims=True)
    zc = z - mu
    var = jnp.mean(zc * zc, axis=-1, keepdims=True)
    return zc * lax.rsqrt(var + EPS) * gain + bias


def _mix_kernel(x_ref, hg_ref, at_ref, ga_ref, gb_ref, wph_ref, wpm_ref, wo_ref, gain_ref, bias_ref,
                h_ref, ht_ref):
    ya = jnp.dot(hg_ref[...], wph_ref[...], preferred_element_type=F32)
    yb = jnp.dot(at_ref[...], wpm_ref[...], preferred_element_type=F32)
    mix = (jax.nn.sigmoid(ga_ref[...].astype(F32)) * ya + jax.nn.sigmoid(gb_ref[...].astype(F32)) * yb)
    mixed = jnp.dot(mix.astype(BF16), wo_ref[...], preferred_element_type=F32)
    h = _layer_norm(ALPHA * x_ref[...] + mixed, gain_ref[...], bias_ref[...])
    h_ref[...] = h
    ht_ref[...] = h.T.astype(ht_ref.dtype)


def _mix(x2, hg_o, at_o, gates, wph, wpm, wo, gain, bias):
    t = x2.shape[0]
    tm = _pick(t, 512)
    full = lambda a: pl.BlockSpec(a.shape, lambda i: (0,) * a.ndim)
    return pl.pallas_call(
        _mix_kernel,
        grid=(t // tm,),
        in_specs=[pl.BlockSpec((tm, D_MODEL), lambda i: (i, 0)),
                  pl.BlockSpec((tm, HG_WIDTH), lambda i: (i, 0)),
                  pl.BlockSpec((tm, MLA_HEADS * MLA_V), lambda i: (i, 0)),
                  pl.BlockSpec((tm, D_MODEL), lambda i: (i, 0)),
                  pl.BlockSpec((tm, D_MODEL), lambda i: (i, 1)),
                  full(wph), full(wpm), full(wo), full(gain), full(bias)],
        out_specs=[pl.BlockSpec((tm, D_MODEL), lambda i: (i, 0)),
                   pl.BlockSpec((D_MODEL, tm), lambda i: (0, i))],
        out_shape=[jax.ShapeDtypeStruct((t, D_MODEL), F32),
                   jax.ShapeDtypeStruct((D_MODEL, t), BF16)],
        compiler_params=_cparams(("parallel",)),
        name="mix_ln1",
    )(x2, hg_o, at_o, gates, gates, wph, wpm, wo, gain, bias)


N_TOP = PEER_TOPK + 1
TOP_ROWS = 24


def _top_rows(x, n):
    vals = []
    for _ in range(n):
        m = jnp.max(x, axis=0, keepdims=True)
        vals.append(m)
        x = jnp.where(x == m, -jnp.inf, x)
    return vals


def _stack_rows(rows, tokens):
    ri = lax.broadcasted_iota(jnp.int32, (TOP_ROWS, tokens), 0)
    out = jnp.full((TOP_ROWS, tokens), -jnp.inf, F32)
    for r, v in enumerate(rows):
        out = jnp.where(ri == r, v, out)
    return out


def _route_kernel(ht_ref, wq_ref, sk_ref, thr_ref, e1_ref, u2_ref, e2_ref):
    tokens = ht_ref.shape[1]
    qp = jnp.dot(wq_ref[...], ht_ref[...], preferred_element_type=F32).astype(BF16)
    for h in range(PEER_HEADS):
        s1 = jnp.dot(sk_ref[2 * h], qp[(2 * h) * PEER_HALF:(2 * h + 1) * PEER_HALF],
                     preferred_element_type=F32)
        s2 = jnp.dot(sk_ref[2 * h + 1], qp[(2 * h + 1) * PEER_HALF:(2 * h + 2) * PEER_HALF],
                     preferred_element_type=F32)
        t1 = _top_rows(s1, N_TOP)
        t2 = _stack_rows(_top_rows(s2, N_TOP), tokens)
        cand = jnp.concatenate([t1[a] + t2 for a in range(N_TOP)], axis=0)
        best = _top_rows(cand, N_TOP)
        m = best[0]
        z = jnp.zeros_like(m)
        for r in range(PEER_TOPK):
            z = z + jnp.exp(best[r] - m)
        tau = 0.5 * (best[PEER_TOPK - 1] + best[PEER_TOPK]) - m
        u1 = s1 - t1[0]
        u2 = s2 - t2[0:1, :]
        thr_ref[h] = tau - u1
        e1_ref[h] = jnp.exp(u1 - jnp.log(z))
        u2_ref[h] = u2
        e2_ref[h] = jnp.exp(u2)


def _route(ht, wqt, sk):
    t = ht.shape[1]
    tm = _pick(t, 256)
    out = jax.ShapeDtypeStruct((PEER_HEADS, PEER_NKEYS, t), F32)
    ospec = pl.BlockSpec((PEER_HEADS, PEER_NKEYS, tm), lambda i: (0, 0, i))
    return pl.pallas_call(
        _route_kernel,
        grid=(t // tm,),
        in_specs=[pl.BlockSpec((D_MODEL, tm), lambda i: (0, i)),
                  pl.BlockSpec(wqt.shape, lambda i: (0, 0)),
                  pl.BlockSpec(sk.shape, lambda i: (0, 0, 0))],
        out_specs=[ospec] * 4,
        out_shape=[out] * 4,
        compiler_params=_cparams(("parallel",)),
        name="peer_route",
    )(ht, wqt, sk)


def _gelu(a):
    return 0.5 * a * (1.0 + jnp.tanh(0.7978845608028654 * (a + 0.044715 * (a * a * a))))


def _expert_kernel(ht_ref, h_ref, u_ref, vt_ref, thr_ref, e1_ref, u2_ref, e2_ref, gain_ref, bias_ref,
                   o_ref, acc_ref, act_ref, *, rows_per_tile):
    e = pl.program_id(1)

    @pl.when(e == 0)
    def _():
        acc_ref[...] = jnp.zeros_like(acc_ref)

    for r in range(rows_per_tile):
        rs = slice(r * PEER_NKEYS, (r + 1) * PEER_NKEYS)
        a = jnp.dot(u_ref[rs, :], ht_ref[...], preferred_element_type=F32)
        g = jnp.zeros_like(a)
        for h in range(PEER_HEADS):
            sel = u2_ref[h] >= thr_ref[h, r:r + 1, :]
            g = g + jnp.where(sel, e2_ref[h] * e1_ref[h, r:r + 1, :], 0.0)
        act_ref[rs, :] = (_gelu(a) * g).astype(act_ref.dtype)
    acc_ref[...] += jnp.dot(vt_ref[...], act_ref[...], preferred_element_type=F32)

    @pl.when(e == pl.num_programs(1) - 1)
    def _():
        z = ALPHA * h_ref[...] + acc_ref[...].T
        o_ref[...] = _layer_norm(z, gain_ref[...], bias_ref[...])


def _experts(ht, h1, u, vt, thr, e1, u2, e2, gain, bias, tm=512, te=1024):
    t = h1.shape[0]
    tm = _pick(t, tm)
    rows_per_tile = te // PEER_NKEYS
    hspec = pl.BlockSpec((PEER_HEADS, PEER_NKEYS, tm), lambda i, e: (0, 0, i))
    rspec = pl.BlockSpec((PEER_HEADS, rows_per_tile, tm), lambda i, e: (0, e, i))
    vec = pl.BlockSpec((1, D_MODEL), lambda i, e: (0, 0))
    return pl.pallas_call(
        functools.partial(_expert_kernel, rows_per_tile=rows_per_tile),
        grid=(t // tm, PEER_N // te),
        in_specs=[pl.BlockSpec((D_MODEL, tm), lambda i, e: (0, i)),
                  pl.BlockSpec((tm, D_MODEL), lambda i, e: (i, 0)),
                  pl.BlockSpec((te, D_MODEL), lambda i, e: (e, 0)),
                  pl.BlockSpec((D_MODEL, te), lambda i, e: (0, e)),
                  rspec, rspec, hspec, hspec, vec, vec],
        out_specs=pl.BlockSpec((tm, D_MODEL), lambda i, e: (i, 0)),
        out_shape=jax.ShapeDtypeStruct((t, D_MODEL), F32),
        scratch_shapes=[pltpu.VMEM((D_MODEL, tm), F32), pltpu.VMEM((te, tm), BF16)],
        compiler_params=_cparams(("parallel", "arbitrary")),
        name="peer_experts",
    )(ht, h1, u, vt, thr, e1, u2, e2, gain, bias)


def _rope_tables(pos):
    half = MLA_ROPE // 2
    inv_freq = ROPE_BASE ** (-jnp.arange(half, dtype=F32) / half)
    ang = pos.astype(F32)[:, None] * inv_freq[None, :]
    zeros = jnp.zeros((pos.shape[0], MLA_NOPE - MLA_ROPE), F32)
    cos, sin = jnp.cos(ang), jnp.sin(ang)
    return jnp.concatenate([cos, cos, zeros], axis=1), jnp.concatenate([sin, sin, zeros], axis=1)


def _rot_cols(w):
    half = MLA_ROPE // 2
    return jnp.concatenate([-w[..., half:], w[..., :half]], axis=-1)


def kernel(x, meta_tokens, hgrn_lb_logits, w_in, q_norm_gain, kv_norm_gain, w_uq, w_ukv, hgrn_norm_gain,
           w_proj_hgrn, w_proj_mla, w_out, ln1_gain, ln1_bias, peer_query, peer_sub_keys, peer_u, peer_v,
           ln2_gain, ln2_bias):
    batch, seq, d = x.shape
    p_rows = LEAD + seq
    t_real = batch * seq
    l = 0

    h_all = jnp.concatenate([jnp.zeros((batch, N_PAD, d), BF16),
                             jnp.broadcast_to(meta_tokens.astype(BF16)[None], (batch, N_META, d)),
                             x.astype(BF16)], axis=1).reshape(batch * p_rows, d)
    x2 = x.reshape(t_real, d)
    x_bf = x2.astype(BF16)

    lower_bounds = jnp.cumsum(jax.nn.softmax(hgrn_lb_logits.astype(F32), axis=0), axis=0)
    rows = jnp.arange(p_rows)
    cos_all, sin_all = _rope_tables(rows - N_PAD)

    w = w_in[l]
    o_hg = 4 * HG_WIDTH
    o_cq = o_hg + MLA_Q_RANK
    o_ckv = o_cq + MLA_KV_RANK
    o_kr = o_ckv + MLA_ROPE
    w_hg = w[:, :o_hg].astype(BF16)
    w_cq = w[:, o_hg:o_cq].astype(BF16)
    w_kr = w[:, o_ckv:o_kr]
    zpad = jnp.zeros((d, MLA_NOPE - MLA_ROPE), F32)
    w_kv = jnp.concatenate([w[:, o_cq:o_ckv], w_kr, zpad, _rot_cols(w_kr), zpad], axis=1).astype(BF16)
    w_g = w[:, o_kr:].astype(BF16)

    wq3 = w_uq[l].reshape(MLA_Q_RANK, MLA_HEADS, MLA_NOPE + MLA_ROPE)
    zq = jnp.zeros((MLA_Q_RANK, MLA_HEADS, MLA_NOPE - MLA_ROPE), F32)
    wq_rope = wq3[..., MLA_NOPE:]
    wq = jnp.concatenate([wq3[..., :MLA_NOPE], wq_rope, zq, _rot_cols(wq_rope), zq], axis=-1)
    wq = wq.transpose(1, 0, 2).astype(BF16)
    wkv = w_ukv[l].reshape(MLA_KV_RANK, MLA_HEADS, MLA_NOPE + MLA_V).transpose(1, 0, 2).astype(BF16)

    tm_all = _pick(batch * p_rows, 1024)
    tm_real = _pick(t_real, 1024)
    hg = _matmul(h_all, w_hg, tm_all, 1024)
    kvp = _matmul(h_all, w_kv, tm_all, w_kv.shape[1])
    cq = _matmul(x_bf, w_cq, tm_real, MLA_Q_RANK)
    gates = _matmul(x_bf, w_g, tm_real, 1024)

    hg_o = _hgrn(hg, lower_bounds[l][None, :], hgrn_norm_gain[l][None, :], batch, p_rows)

    q = _qprep(cq, q_norm_gain[l][None, :], wq, cos_all[LEAD:], sin_all[LEAD:], batch, seq)
    k, v = _kvprep(kvp, kv_norm_gain[l][None, :], wkv, cos_all, sin_all, batch, p_rows)
    at_o = _flash(q, k, v, batch, seq, p_rows)

    h1, h1t = _mix(x2, hg_o.reshape(t_real, HG_WIDTH), at_o.reshape(t_real, MLA_HEADS * MLA_V), gates,
                   w_proj_hgrn[l].astype(BF16), w_proj_mla[l].astype(BF16), w_out[l].astype(BF16),
                   ln1_gain[l][None, :], ln1_bias[l][None, :])

    wqt = peer_query[l].T.astype(BF16)
    sk = peer_sub_keys[l].reshape(PEER_HEADS * 2, PEER_NKEYS, PEER_HALF).astype(BF16)
    thr, e1, u2, e2 = _route(h1t, wqt, sk)

    out = _experts(h1t, h1, peer_u[l].astype(BF16), peer_v[l].T.astype(BF16), thr, e1, u2, e2,
                   ln2_gain[l][None, :], ln2_bias[l][None, :])
    return out.reshape(batch, seq, d)
```

```python
import functools

import numpy as np
import jax
import jax.numpy as jnp
from jax import lax
from jax.experimental import pallas as pl
from jax.experimental.pallas import tpu as pltpu

F32 = jnp.float32
BF16 = jnp.bfloat16

D_MODEL = 1024
DEPTH = 1
N_META = 16
LEAD = 128
N_PAD = LEAD - N_META

HG_HEADS = 8
HG_D = 128
HG_WIDTH = HG_HEADS * HG_D
HG_SUB = 16

MLA_HEADS = 16
MLA_NOPE = 128
MLA_ROPE = 64
MLA_V = 128
MLA_Q_RANK = 384
MLA_KV_RANK = 256
MLA_QK = 256
MLA_SCALE = (MLA_NOPE + MLA_ROPE) ** -0.5
ROPE_BASE = 10000.0

PEER_HEADS = 8
PEER_NKEYS = 128
PEER_N = PEER_NKEYS * PEER_NKEYS
PEER_HALF = 128
PEER_TOPK = 16

ALPHA = (2 * DEPTH) ** 0.25
EPS = 1e-5
NEG = -1e30

VMEM_LIMIT = 56 * 1024 * 1024


def _cparams(sem):
    return pltpu.CompilerParams(dimension_semantics=sem, vmem_limit_bytes=VMEM_LIMIT)


def _pick(n, pref):
    t = min(n, pref)
    while n % t:
        t -= 128
    return t


def _mm_kernel(x_ref, w_ref, o_ref):
    o_ref[...] = jnp.dot(x_ref[...], w_ref[...], preferred_element_type=F32).astype(o_ref.dtype)


def _matmul(x, w, tm, tn):
    m, k = x.shape
    out_dtype = x.dtype
    n = w.shape[1]
    return pl.pallas_call(
        _mm_kernel,
        grid=(m // tm, n // tn),
        in_specs=[pl.BlockSpec((tm, k), lambda i, j: (i, 0)),
                  pl.BlockSpec((k, tn), lambda i, j: (0, j))],
        out_specs=pl.BlockSpec((tm, tn), lambda i, j: (i, j)),
        out_shape=jax.ShapeDtypeStruct((m, n), out_dtype),
        compiler_params=_cparams(("parallel", "parallel")),
        name="in_proj",
    )(x, w)


def _hgrn_chunk(q, k, lf, iv, st, tri, masks, chunk):
    hi = lf.astype(BF16)
    r1 = lf - hi.astype(F32)
    mid = r1.astype(BF16)
    lo = (r1 - mid.astype(F32)).astype(BF16)
    g = (jnp.dot(tri, hi, preferred_element_type=F32)
         + jnp.dot(tri, mid, preferred_element_type=F32)
         + jnp.dot(tri, lo, preferred_element_type=F32))

    def bcast_row(r, n):
        return jnp.broadcast_to(g[r:r + 1, :], (n, HG_D))

    def prev_end(s):
        parts = [jnp.zeros((s, HG_D), F32)] + [bcast_row(b * s - 1, s) for b in range(1, chunk // s)]
        return parts[0] if len(parts) == 1 else jnp.concatenate(parts, axis=0)

    def own_end(s):
        parts = [bcast_row((b + 1) * s - 1, s) for b in range(chunk // s)]
        return parts[0] if len(parts) == 1 else jnp.concatenate(parts, axis=0)

    nt = (((1,), (1,)), ((), ()))
    tn = (((0,), (0,)), ((), ()))

    q_c = (q * jnp.exp(g)).astype(BF16)
    o = lax.dot_general(q_c, st.astype(BF16), nt, preferred_element_type=F32)
    k_c = (k * jnp.exp(own_end(chunk) - g)).astype(BF16)
    upd = lax.dot_general(iv, k_c, tn, preferred_element_type=F32)
    st_new = st * jnp.exp(g[chunk - 1:chunk, :]) + upd

    p16 = prev_end(HG_SUB)
    q_d = (q * jnp.exp(g - p16)).astype(BF16)
    k_d = (k * jnp.exp(p16 - g)).astype(BF16)
    a = jnp.where(masks[0], lax.dot_general(q_d, k_d, nt, preferred_element_type=F32), 0.0)
    s = HG_SUB
    lvl = 1
    while s < chunk:
        q_s = q_d if s == HG_SUB else (q * jnp.exp(g - prev_end(s))).astype(BF16)
        k_s = (k * jnp.exp(own_end(s) - g)).astype(BF16)
        a = jnp.where(masks[lvl], lax.dot_general(q_s, k_s, nt, preferred_element_type=F32), a)
        s *= 2
        lvl += 1
    o = o + jnp.dot(a.astype(BF16), iv, preferred_element_type=F32)
    return o, st_new


def _hgrn_kernel(q_ref, f_ref, i_ref, g_ref, lb_ref, gain_ref, tri_ref, o_ref, st_ref, *, chunk, rows):
    c = pl.program_id(1)

    @pl.when(c == 0)
    def _():
        st_ref[...] = jnp.zeros_like(st_ref)

    ri = lax.broadcasted_iota(jnp.int32, (chunk, chunk), 0)
    ci = lax.broadcasted_iota(jnp.int32, (chunk, chunk), 1)
    blk = lambda v, s: lax.shift_right_logical(v, s.bit_length() - 1)
    masks = [(blk(ri, HG_SUB) == blk(ci, HG_SUB)) & (ci <= ri)]
    s = HG_SUB
    while s < chunk:
        masks.append((blk(ri, 2 * s) == blk(ci, 2 * s)) & ((blk(ri, s) & 1) == 1) & ((blk(ci, s) & 1) == 0))
        s *= 2
    tri = tri_ref[...]

    for n in range(rows // chunk):
        rs = slice(n * chunk, (n + 1) * chunk)
        seq_row = c * rows + n * chunk + lax.broadcasted_iota(jnp.int32, (chunk, HG_D), 0)
        valid = seq_row >= N_PAD
        for h in range(HG_HEADS):
            hs = slice(h * HG_D, (h + 1) * HG_D)
            lb = lb_ref[:, hs]
            f = lb + (1.0 - lb) * jax.nn.sigmoid(f_ref[rs, hs].astype(F32))
            lf = jnp.where(valid, jnp.log(f), 0.0)
            k = jnp.where(valid, 1.0 - f, 0.0)
            q = q_ref[rs, hs].astype(F32)
            o, st_new = _hgrn_chunk(q, k, lf, i_ref[rs, hs], st_ref[h], tri, masks, chunk)
            st_ref[h] = st_new
            ms = jnp.mean(o * o, axis=-1, keepdims=True)
            o = o * lax.rsqrt(ms + EPS) * gain_ref[:, hs]
            gate = g_ref[rs, hs].astype(F32)
            o_ref[0, rs, hs] = (o * (gate * jax.nn.sigmoid(gate))).astype(o_ref.dtype)


def _hgrn(hg, lb, gain, batch, p_rows, chunk=64, rows=128):
    nc = p_rows // rows
    lead_blocks = LEAD // rows
    tri = jnp.asarray(np.tril(np.ones((chunk, chunk), np.float32)), BF16)
    col = lambda j: pl.BlockSpec((rows, HG_WIDTH), lambda b, c: (b * nc + c, j))
    vec = pl.BlockSpec((1, HG_WIDTH), lambda b, c: (0, 0))
    return pl.pallas_call(
        functools.partial(_hgrn_kernel, chunk=chunk, rows=rows),
        grid=(batch, nc),
        in_specs=[col(0), col(1), col(2), col(3), vec, vec,
                  pl.BlockSpec((chunk, chunk), lambda b, c: (0, 0))],
        out_specs=pl.BlockSpec((1, rows, HG_WIDTH), lambda b, c: (b, jnp.maximum(c - lead_blocks, 0), 0)),
        out_shape=jax.ShapeDtypeStruct((batch, p_rows - LEAD, HG_WIDTH), BF16),
        scratch_shapes=[pltpu.VMEM((HG_HEADS, HG_D, HG_D), F32)],
        compiler_params=_cparams(("parallel", "arbitrary")),
        name="hgrn2",
    )(hg, hg, hg, hg, lb, gain, tri)


def _rms(x, gain):
    ms = jnp.mean(x * x, axis=-1, keepdims=True)
    return x * lax.rsqrt(ms + EPS) * gain


def _qprep_kernel(cq_ref, gain_ref, w_ref, cos_ref, sin_ref, o_ref):
    xn = _rms(cq_ref[...].astype(F32), gain_ref[...]).astype(BF16)
    cos = cos_ref[...]
    sin = sin_ref[...]
    for h in range(MLA_HEADS):
        y = jnp.dot(xn, w_ref[h], preferred_element_type=F32)
        o_ref[0, h, :, :MLA_NOPE] = (y[:, :MLA_NOPE] * MLA_SCALE).astype(o_ref.dtype)
        roped = y[:, MLA_NOPE:2 * MLA_NOPE] * cos + y[:, 2 * MLA_NOPE:] * sin
        o_ref[0, h, :, MLA_NOPE:] = (roped * MLA_SCALE).astype(o_ref.dtype)


def _qprep(cq, gain, wq, cos, sin, batch, seq):
    tm = _pick(seq, 512)
    nb = seq // tm
    return pl.pallas_call(
        _qprep_kernel,
        grid=(batch, nb),
        in_specs=[pl.BlockSpec((tm, MLA_Q_RANK), lambda b, i: (b * nb + i, 0)),
                  pl.BlockSpec((1, MLA_Q_RANK), lambda b, i: (0, 0)),
                  pl.BlockSpec((MLA_HEADS, MLA_Q_RANK, 3 * MLA_NOPE), lambda b, i: (0, 0, 0)),
                  pl.BlockSpec((tm, MLA_NOPE), lambda b, i: (i, 0)),
                  pl.BlockSpec((tm, MLA_NOPE), lambda b, i: (i, 0))],
        out_specs=pl.BlockSpec((1, MLA_HEADS, tm, MLA_QK), lambda b, i: (b, 0, i, 0)),
        out_shape=jax.ShapeDtypeStruct((batch, MLA_HEADS, seq, MLA_QK), BF16),
        compiler_params=_cparams(("parallel", "parallel")),
        name="mla_q",
    )(cq, gain, wq, cos, sin)


def _kvprep_kernel(kv_ref, gain_ref, w_ref, cos_ref, sin_ref, k_ref, v_ref):
    x = kv_ref[...]
    xn = _rms(x[:, :MLA_KV_RANK].astype(F32), gain_ref[...]).astype(BF16)
    kr = (x[:, MLA_KV_RANK:MLA_KV_RANK + MLA_NOPE].astype(F32) * cos_ref[...]
          + x[:, MLA_KV_RANK + MLA_NOPE:].astype(F32) * sin_ref[...]).astype(k_ref.dtype)
    for h in range(MLA_HEADS):
        y = jnp.dot(xn, w_ref[h], preferred_element_type=F32)
        k_ref[0, h, :, :MLA_NOPE] = y[:, :MLA_NOPE].astype(k_ref.dtype)
        k_ref[0, h, :, MLA_NOPE:] = kr
        v_ref[0, h] = y[:, MLA_NOPE:].astype(v_ref.dtype)


def _kvprep(kvp, gain, wkv, cos, sin, batch, p_rows):
    tm = p_rows // 2
    return pl.pallas_call(
        _kvprep_kernel,
        grid=(batch, 2),
        in_specs=[pl.BlockSpec((tm, MLA_KV_RANK + 2 * MLA_NOPE), lambda b, i: (b * 2 + i, 0)),
                  pl.BlockSpec((1, MLA_KV_RANK), lambda b, i: (0, 0)),
                  pl.BlockSpec((MLA_HEADS, MLA_KV_RANK, MLA_NOPE + MLA_V), lambda b, i: (0, 0, 0)),
                  pl.BlockSpec((tm, MLA_NOPE), lambda b, i: (i, 0)),
                  pl.BlockSpec((tm, MLA_NOPE), lambda b, i: (i, 0))],
        out_specs=[pl.BlockSpec((1, MLA_HEADS, tm, MLA_QK), lambda b, i: (b, 0, i, 0)),
                   pl.BlockSpec((1, MLA_HEADS, tm, MLA_V), lambda b, i: (b, 0, i, 0))],
        out_shape=[jax.ShapeDtypeStruct((batch, MLA_HEADS, p_rows, MLA_QK), BF16),
                   jax.ShapeDtypeStruct((batch, MLA_HEADS, p_rows, MLA_V), BF16)],
        compiler_params=_cparams(("parallel", "parallel")),
        name="mla_kv",
    )(kvp, gain, wkv, cos, sin)


def _flash_kernel(q_ref, k_ref, v_ref, o_ref, *, tq, nsplit):
    qi = pl.program_id(2)
    hq = tq // nsplit
    nt = (((1,), (1,)), ((), ()))

    def step(q, kb, vb, carry, mask):
        m, l, acc = carry
        s = lax.dot_general(q, kb, nt, preferred_element_type=F32)
        if mask is not None:
            s = jnp.where(mask, s, NEG)
        m_new = jnp.maximum(m, jnp.max(s, axis=-1, keepdims=True))
        p = jnp.exp(s - m_new)
        corr = jnp.exp(m - m_new)
        l = corr * l + jnp.sum(p, axis=-1, keepdims=True)
        acc = corr * acc + jnp.dot(p.astype(BF16), vb, preferred_element_type=F32)
        return m_new, l, acc

    qs = [q_ref[0, 0, i * hq:(i + 1) * hq, :] for i in range(nsplit)]
    init = (jnp.full((hq, 1), NEG, F32), jnp.zeros((hq, 1), F32), jnp.zeros((hq, MLA_V), F32))
    lead_mask = lax.broadcasted_iota(jnp.int32, (hq, LEAD), 1) >= N_PAD
    k_lead = k_ref[0, 0, 0:LEAD, :]
    v_lead = v_ref[0, 0, 0:LEAD, :]
    carries = tuple(step(qs[i], k_lead, v_lead, init, lead_mask) for i in range(nsplit))

    def body(j, carries):
        k0 = pl.multiple_of(LEAD + j * tq, LEAD)
        kb = k_ref[0, 0, pl.ds(k0, tq), :]
        vb = v_ref[0, 0, pl.ds(k0, tq), :]
        return tuple(step(qs[i], kb, vb, carries[i], None) for i in range(nsplit))

    carries = lax.fori_loop(0, qi, body, carries)

    d0 = pl.multiple_of(LEAD + qi * tq, LEAD)
    for i in range(nsplit):
        nk = (i + 1) * hq
        row = lax.broadcasted_iota(jnp.int32, (hq, nk), 0) + i * hq
        col = lax.broadcasted_iota(jnp.int32, (hq, nk), 1)
        _, l, acc = step(qs[i], k_ref[0, 0, pl.ds(d0, nk), :], v_ref[0, 0, pl.ds(d0, nk), :],
                         carries[i], col <= row)
        o_ref[0, i * hq:(i + 1) * hq, :] = (acc / l).astype(o_ref.dtype)


def _flash(q, k, v, batch, seq, p_rows, tq=512, nsplit=2):
    tq = min(tq, seq)
    return pl.pallas_call(
        functools.partial(_flash_kernel, tq=tq, nsplit=nsplit),
        grid=(batch, MLA_HEADS, seq // tq),
        in_specs=[pl.BlockSpec((1, 1, tq, MLA_QK), lambda b, h, i: (b, h, i, 0)),
                  pl.BlockSpec((1, 1, p_rows, MLA_QK), lambda b, h, i: (b, h, 0, 0)),
                  pl.BlockSpec((1, 1, p_rows, MLA_V), lambda b, h, i: (b, h, 0, 0))],
        out_specs=pl.BlockSpec((1, tq, MLA_V), lambda b, h, i: (b, i, h)),
        out_shape=jax.ShapeDtypeStruct((batch, seq, MLA_HEADS * MLA_V), BF16),
        compiler_params=_cparams(("parallel", "parallel", "arbitrary")),
        name="mla_flash",
    )(q, k, v)


def _layer_norm(z, gain, bias):
    mu = jnp.mean(z, axis=-1, keepdims=True)
    zc = z - mu
    var = jnp.mean(zc * zc, axis=-1, keepdims=True)
    return zc * lax.rsqrt(var + EPS) * gain + bias


def _mix_kernel(x_ref, hg_ref, at_ref, ga_ref, gb_ref, wph_ref, wpm_ref, wo_ref, gain_ref, bias_ref,
                h_ref, ht_ref):
    ya = jnp.dot(hg_ref[...], wph_ref[...], preferred_element_type=F32)
    yb = jnp.dot(at_ref[...], wpm_ref[...], preferred_element_type=F32)
    mix = (jax.nn.sigmoid(ga_ref[...].astype(F32)) * ya + jax.nn.sigmoid(gb_ref[...].astype(F32)) * yb)
    mixed = jnp.dot(mix.astype(BF16), wo_ref[...], preferred_element_type=F32)
    h = _layer_norm(ALPHA * x_ref[...] + mixed, gain_ref[...], bias_ref[...])
    h_ref[...] = h
    ht_ref[...] = h.T.astype(ht_ref.dtype)


def _mix(x2, hg_o, at_o, gates, wph, wpm, wo, gain, bias):
    t = x2.shape[0]
    tm = _pick(t, 512)
    full = lambda a: pl.BlockSpec(a.shape, lambda i: (0,) * a.ndim)
    return pl.pallas_call(
        _mix_kernel,
        grid=(t // tm,),
        in_specs=[pl.BlockSpec((tm, D_MODEL), lambda i: (i, 0)),
                  pl.BlockSpec((tm, HG_WIDTH), lambda i: (i, 0)),
                  pl.BlockSpec((tm, MLA_HEADS * MLA_V), lambda i: (i, 0)),
                  pl.BlockSpec((tm, D_MODEL), lambda i: (i, 0)),
                  pl.BlockSpec((tm, D_MODEL), lambda i: (i, 1)),
                  full(wph), full(wpm), full(wo), full(gain), full(bias)],
        out_specs=[pl.BlockSpec((tm, D_MODEL), lambda i: (i, 0)),
                   pl.BlockSpec((D_MODEL, tm), lambda i: (0, i))],
        out_shape=[jax.ShapeDtypeStruct((t, D_MODEL), F32),
                   jax.ShapeDtypeStruct((D_MODEL, t), BF16)],
        compiler_params=_cparams(("parallel",)),
        name="mix_ln1",
    )(x2, hg_o, at_o, gates, gates, wph, wpm, wo, gain, bias)


N_TOP = PEER_TOPK + 1
TOP_ROWS = 24


def _top_rows(x, n):
    vals = []
    for _ in range(n):
        m = jnp.max(x, axis=0, keepdims=True)
        vals.append(m)
        x = jnp.where(x == m, -jnp.inf, x)
    return vals


def _stack_rows(rows, tokens):
    ri = lax.broadcasted_iota(jnp.int32, (TOP_ROWS, tokens), 0)
    out = jnp.full((TOP_ROWS, tokens), -jnp.inf, F32)
    for r, v in enumerate(rows):
        out = jnp.where(ri == r, v, out)
    return out


def _route_kernel(ht_ref, wq_ref, sk_ref, thr_ref, e1_ref, u2_ref, e2_ref):
    tokens = ht_ref.shape[1]
    qp = jnp.dot(wq_ref[...], ht_ref[...], preferred_element_type=F32).astype(BF16)
    for h in range(PEER_HEADS):
        s1 = jnp.dot(sk_ref[2 * h], qp[(2 * h) * PEER_HALF:(2 * h + 1) * PEER_HALF],
                     preferred_element_type=F32)
        s2 = jnp.dot(sk_ref[2 * h + 1], qp[(2 * h + 1) * PEER_HALF:(2 * h + 2) * PEER_HALF],
                     preferred_element_type=F32)
        t1 = _top_rows(s1, N_TOP)
        t2 = _stack_rows(_top_rows(s2, N_TOP), tokens)
        cand = jnp.concatenate([t1[a] + t2 for a in range(N_TOP)], axis=0)
        best = _top_rows(cand, N_TOP)
        m = best[0]
        z = jnp.zeros_like(m)
        for r in range(PEER_TOPK):
            z = z + jnp.exp(best[r] - m)
        tau = 0.5 * (best[PEER_TOPK - 1] + best[PEER_TOPK]) - m
        u1 = s1 - t1[0]
        u2 = s2 - t2[0:1, :]
        thr_ref[h] = tau - u1
        e1_ref[h] = jnp.exp(u1 - jnp.log(z))
        u2_ref[h] = u2
        e2_ref[h] = jnp.exp(u2)


def _route(ht, wqt, sk):
    t = ht.shape[1]
    tm = _pick(t, 256)
    out = jax.ShapeDtypeStruct((PEER_HEADS, PEER_NKEYS, t), F32)
    ospec = pl.BlockSpec((PEER_HEADS, PEER_NKEYS, tm), lambda i: (0, 0, i))
    return pl.pallas_call(
        _route_kernel,
        grid=(t // tm,),
        in_specs=[pl.BlockSpec((D_MODEL, tm), lambda i: (0, i)),
                  pl.BlockSpec(wqt.shape, lambda i: (0, 0)),
                  pl.BlockSpec(sk.shape, lambda i: (0, 0, 0))],
        out_specs=[ospec] * 4,
        out_shape=[out] * 4,
        compiler_params=_cparams(("parallel",)),
        name="peer_route",
    )(ht, wqt, sk)


def _gelu(a):
    return 0.5 * a * (1.0 + jnp.tanh(0.7978845608028654 * (a + 0.044715 * (a * a * a))))


def _expert_kernel(ht_ref, h_ref, u_ref, vt_ref, thr_ref, e1_ref, u2_ref, e2_ref, gain_ref, bias_ref,
                   o_ref, acc_ref, act_ref, *, rows_per_tile):
    e = pl.program_id(1)

    @pl.when(e == 0)
    def _():
        acc_ref[...] = jnp.zeros_like(acc_ref)

    for r in range(rows_per_tile):
        rs = slice(r * PEER_NKEYS, (r + 1) * PEER_NKEYS)
        a = jnp.dot(u_ref[rs, :], ht_ref[...], preferred_element_type=F32)
        g = jnp.zeros_like(a)
        for h in range(PEER_HEADS):
            sel = u2_ref[h] >= thr_ref[h, r:r + 1, :]
            g = g + jnp.where(sel, e2_ref[h] * e1_ref[h, r:r + 1, :], 0.0)
        act_ref[rs, :] = (_gelu(a) * g).astype(act_ref.dtype)
    acc_ref[...] += jnp.dot(vt_ref[...], act_ref[...], preferred_element_type=F32)

    @pl.when(e == pl.num_programs(1) - 1)
    def _():
        z = ALPHA * h_ref[...] + acc_ref[...].T
        o_ref[...] = _layer_norm(z, gain_ref[...], bias_ref[...])


def _experts(ht, h1, u, vt, thr, e1, u2, e2, gain, bias, tm=512, te=1024):
    t = h1.shape[0]
    tm = _pick(t, tm)
    rows_per_tile = te // PEER_NKEYS
    hspec = pl.BlockSpec((PEER_HEADS, PEER_NKEYS, tm), lambda i, e: (0, 0, i))
    rspec = pl.BlockSpec((PEER_HEADS, rows_per_tile, tm), lambda i, e: (0, e, i))
    vec = pl.BlockSpec((1, D_MODEL), lambda i, e: (0, 0))
    return pl.pallas_call(
        functools.partial(_expert_kernel, rows_per_tile=rows_per_tile),
        grid=(t // tm, PEER_N // te),
        in_specs=[pl.BlockSpec((D_MODEL, tm), lambda i, e: (0, i)),
                  pl.BlockSpec((tm, D_MODEL), lambda i, e: (i, 0)),
                  pl.BlockSpec((te, D_MODEL), lambda i, e: (e, 0)),
                  pl.BlockSpec((D_MODEL, te), lambda i, e: (0, e)),
                  rspec, rspec, hspec, hspec, vec, vec],
        out_specs=pl.BlockSpec((tm, D_MODEL), lambda i, e: (i, 0)),
        out_shape=jax.ShapeDtypeStruct((t, D_MODEL), F32),
        scratch_shapes=[pltpu.VMEM((D_MODEL, tm), F32), pltpu.VMEM((te, tm), BF16)],
        compiler_params=_cparams(("parallel", "arbitrary")),
        name="peer_experts",
    )(ht, h1, u, vt, thr, e1, u2, e2, gain, bias)


def _rope_tables(pos):
    half = MLA_ROPE // 2
    inv_freq = ROPE_BASE ** (-jnp.arange(half, dtype=F32) / half)
    ang = pos.astype(F32)[:, None] * inv_freq[None, :]
    zeros = jnp.zeros((pos.shape[0], MLA_NOPE - MLA_ROPE), F32)
    cos, sin = jnp.cos(ang), jnp.sin(ang)
    return jnp.concatenate([cos, cos, zeros], axis=1), jnp.concatenate([sin, sin, zeros], axis=1)


def _rot_cols(w):
    half = MLA_ROPE // 2
    return jnp.concatenate([-w[..., half:], w[..., :half]], axis=-1)


def kernel(x, meta_tokens, hgrn_lb_logits, w_in, q_norm_gain, kv_norm_gain, w_uq, w_ukv, hgrn_norm_gain,
           w_proj_hgrn, w_proj_mla, w_out, ln1_gain, ln1_bias, peer_query, peer_sub_keys, peer_u, peer_v,
           ln2_gain, ln2_bias):
    batch, seq, d = x.shape
    p_rows = LEAD + seq
    t_real = batch * seq
    l = 0

    h_all = jnp.concatenate([jnp.zeros((batch, N_PAD, d), BF16),
                             jnp.broadcast_to(meta_tokens.astype(BF16)[None], (batch, N_META, d)),
                             x.astype(BF16)], axis=1).reshape(batch * p_rows, d)
    x2 = x.reshape(t_real, d)
    x_bf = x2.astype(BF16)

    lower_bounds = jnp.cumsum(jax.nn.softmax(hgrn_lb_logits.astype(F32), axis=0), axis=0)
    rows = jnp.arange(p_rows)
    cos_all, sin_all = _rope_tables(rows - N_PAD)

    w = w_in[l]
    o_hg = 4 * HG_WIDTH
    o_cq = o_hg + MLA_Q_RANK
    o_ckv = o_cq + MLA_KV_RANK
    o_kr = o_ckv + MLA_ROPE
    w_hg = w[:, :o_hg].astype(BF16)
    w_cq = w[:, o_hg:o_cq].astype(BF16)
    w_kr = w[:, o_ckv:o_kr]
    zpad = jnp.zeros((d, MLA_NOPE - MLA_ROPE), F32)
    w_kv = jnp.concatenate([w[:, o_cq:o_ckv], w_kr, zpad, _rot_cols(w_kr), zpad], axis=1).astype(BF16)
    w_g = w[:, o_kr:].astype(BF16)

    wq3 = w_uq[l].reshape(MLA_Q_RANK, MLA_HEADS, MLA_NOPE + MLA_ROPE)
    zq = jnp.zeros((MLA_Q_RANK, MLA_HEADS, MLA_NOPE - MLA_ROPE), F32)
    wq_rope = wq3[..., MLA_NOPE:]
    wq = jnp.concatenate([wq3[..., :MLA_NOPE], wq_rope, zq, _rot_cols(wq_rope), zq], axis=-1)
    wq = wq.transpose(1, 0, 2).astype(BF16)
    wkv = w_ukv[l].reshape(MLA_KV_RANK, MLA_HEADS, MLA_NOPE + MLA_V).transpose(1, 0, 2).astype(BF16)

    tm_all = _pick(batch * p_rows, 1024)
    tm_real = _pick(t_real, 1024)
    hg = _matmul(h_all, w_hg, tm_all, 1024)
    kvp = _matmul(h_all, w_kv, tm_all, w_kv.shape[1])
    cq = _matmul(x_bf, w_cq, tm_real, MLA_Q_RANK)
    gates = _matmul(x_bf, w_g, tm_real, 1024)

    hg_o = _hgrn(hg, lower_bounds[l][None, :], hgrn_norm_gain[l][None, :], batch, p_rows)

    q = _qprep(cq, q_norm_gain[l][None, :], wq, cos_all[LEAD:], sin_all[LEAD:], batch, seq)
    k, v = _kvprep(kvp, kv_norm_gain[l][None, :], wkv, cos_all, sin_all, batch, p_rows)
    at_o = _flash(q, k, v, batch, seq, p_rows)

    h1, h1t = _mix(x2, hg_o.reshape(t_real, HG_WIDTH), at_o.reshape(t_real, MLA_HEADS * MLA_V), gates,
                   w_proj_hgrn[l].astype(BF16), w_proj_mla[l].astype(BF16), w_out[l].astype(BF16),
                   ln1_gain[l][None, :], ln1_bias[l][None, :])

    wqt = peer_query[l].T.astype(BF16)
    sk = peer_sub_keys[l].reshape(PEER_HEADS * 2, PEER_NKEYS, PEER_HALF).astype(BF16)
    thr, e1, u2, e2 = _route(h1t, wqt, sk)

    out = _experts(h1t, h1, peer_u[l].astype(BF16), peer_v[l].T.astype(BF16), thr, e1, u2, e2,
                   ln2_gain[l][None, :], ln2_bias[l][None, :])
    return out.reshape(batch, seq, d)
```

```python
import functools

import numpy as np
import jax
import jax.numpy as jnp
from jax import lax
from jax.experimental import pallas as pl
from jax.experimental.pallas import tpu as pltpu

F32 = jnp.float32
BF16 = jnp.bfloat16

D_MODEL = 1024
DEPTH = 1
N_META = 16
LEAD = 128
N_PAD = LEAD - N_META

HG_HEADS = 8
HG_D = 128
HG_WIDTH = HG_HEADS * HG_D
HG_SUB = 16

MLA_HEADS = 16
MLA_NOPE = 128
MLA_ROPE = 64
MLA_V = 128
MLA_Q_RANK = 384
MLA_KV_RANK = 256
MLA_QK = 256
MLA_SCALE = (MLA_NOPE + MLA_ROPE) ** -0.5
ROPE_BASE = 10000.0

PEER_HEADS = 8
PEER_NKEYS = 128
PEER_N = PEER_NKEYS * PEER_NKEYS
PEER_HALF = 128
PEER_TOPK = 16

ALPHA = (2 * DEPTH) ** 0.25
EPS = 1e-5
NEG = -1e30

VMEM_LIMIT = 56 * 1024 * 1024


def _cparams(sem):
    return pltpu.CompilerParams(dimension_semantics=sem, vmem_limit_bytes=VMEM_LIMIT)


def _pick(n, pref):
    t = min(n, pref)
    while n % t:
        t -= 128
    return t


def _mm_kernel(x_ref, w_ref, o_ref):
    o_ref[...] = jnp.dot(x_ref[...], w_ref[...], preferred_element_type=F32).astype(o_ref.dtype)


def _matmul(x, w, tm, tn):
    m, k = x.shape
    out_dtype = x.dtype
    n = w.shape[1]
    return pl.pallas_call(
        _mm_kernel,
        grid=(m // tm, n // tn),
        in_specs=[pl.BlockSpec((tm, k), lambda i, j: (i, 0)),
                  pl.BlockSpec((k, tn), lambda i, j: (0, j))],
        out_specs=pl.BlockSpec((tm, tn), lambda i, j: (i, j)),
        out_shape=jax.ShapeDtypeStruct((m, n), out_dtype),
        compiler_params=_cparams(("parallel", "parallel")),
        name="in_proj",
    )(x, w)


def _hgrn_chunk(q, k, lf, iv, st, tri, masks, chunk):
    hi = lf.astype(BF16)
    r1 = lf - hi.astype(F32)
    mid = r1.astype(BF16)
    lo = (r1 - mid.astype(F32)).astype(BF16)
    g = (jnp.dot(tri, hi, preferred_element_type=F32)
         + jnp.dot(tri, mid, preferred_element_type=F32)
         + jnp.dot(tri, lo, preferred_element_type=F32))

    def bcast_row(r, n):
        return jnp.broadcast_to(g[r:r + 1, :], (n, HG_D))

    def prev_end(s):
        parts = [jnp.zeros((s, HG_D), F32)] + [bcast_row(b * s - 1, s) for b in range(1, chunk // s)]
        return parts[0] if len(parts) == 1 else jnp.concatenate(parts, axis=0)

    def own_end(s):
        parts = [bcast_row((b + 1) * s - 1, s) for b in range(chunk // s)]
        return parts[0] if len(parts) == 1 else jnp.concatenate(parts, axis=0)

    nt = (((1,), (1,)), ((), ()))
    tn = (((0,), (0,)), ((), ()))

    q_c = (q * jnp.exp(g)).astype(BF16)
    o = lax.dot_general(q_c, st.astype(BF16), nt, preferred_element_type=F32)
    k_c = (k * jnp.exp(own_end(chunk) - g)).astype(BF16)
    upd = lax.dot_general(iv, k_c, tn, preferred_element_type=F32)
    st_new = st * jnp.exp(g[chunk - 1:chunk, :]) + upd

    p16 = prev_end(HG_SUB)
    q_d = (q * jnp.exp(g - p16)).astype(BF16)
    k_d = (k * jnp.exp(p16 - g)).astype(BF16)
    a = jnp.where(masks[0], lax.dot_general(q_d, k_d, nt, preferred_element_type=F32), 0.0)
    s = HG_SUB
    lvl = 1
    while s < chunk:
        q_s = q_d if s == HG_SUB else (q * jnp.exp(g - prev_end(s))).astype(BF16)
        k_s = (k * jnp.exp(own_end(s) - g)).astype(BF16)
        a = jnp.where(masks[lvl], lax.dot_general(q_s, k_s, nt, preferred_element_type=F32), a)
        s *= 2
        lvl += 1
    o = o + jnp.dot(a.astype(BF16), iv, preferred_element_type=F32)
    return o, st_new


def _hgrn_kernel(q_ref, f_ref, i_ref, g_ref, lb_ref, gain_ref, tri_ref, o_ref, st_ref, *, chunk, rows):
    c = pl.program_id(1)

    @pl.when(c == 0)
    def _():
        st_ref[...] = jnp.zeros_like(st_ref)

    ri = lax.broadcasted_iota(jnp.int32, (chunk, chunk), 0)
    ci = lax.broadcasted_iota(jnp.int32, (chunk, chunk), 1)
    blk = lambda v, s: lax.shift_right_logical(v, s.bit_length() - 1)
    masks = [(blk(ri, HG_SUB) == blk(ci, HG_SUB)) & (ci <= ri)]
    s = HG_SUB
    while s < chunk:
        masks.append((blk(ri, 2 * s) == blk(ci, 2 * s)) & ((blk(ri, s) & 1) == 1) & ((blk(ci, s) & 1) == 0))
        s *= 2
    tri = tri_ref[...]

    for n in range(rows // chunk):
        rs = slice(n * chunk, (n + 1) * chunk)
        seq_row = c * rows + n * chunk + lax.broadcasted_iota(jnp.int32, (chunk, HG_D), 0)
        valid = seq_row >= N_PAD
        for h in range(HG_HEADS):
            hs = slice(h * HG_D, (h + 1) * HG_D)
            lb = lb_ref[:, hs]
            f = lb + (1.0 - lb) * jax.nn.sigmoid(f_ref[rs, hs].astype(F32))
            lf = jnp.where(valid, jnp.log(f), 0.0)
            k = jnp.where(valid, 1.0 - f, 0.0)
            q = q_ref[rs, hs].astype(F32)
            o, st_new = _hgrn_chunk(q, k, lf, i_ref[rs, hs], st_ref[h], tri, masks, chunk)
            st_ref[h] = st_new
            ms = jnp.mean(o * o, axis=-1, keepdims=True)
            o = o * lax.rsqrt(ms + EPS) * gain_ref[:, hs]
            gate = g_ref[rs, hs].astype(F32)
            o_ref[0, rs, hs] = (o * (gate * jax.nn.sigmoid(gate))).astype(o_ref.dtype)


def _hgrn(hg, lb, gain, batch, p_rows, chunk=64, rows=128):
    nc = p_rows // rows
    lead_blocks = LEAD // rows
    tri = jnp.asarray(np.tril(np.ones((chunk, chunk), np.float32)), BF16)
    col = lambda j: pl.BlockSpec((rows, HG_WIDTH), lambda b, c: (b * nc + c, j))
    vec = pl.BlockSpec((1, HG_WIDTH), lambda b, c: (0, 0))
    return pl.pallas_call(
        functools.partial(_hgrn_kernel, chunk=chunk, rows=rows),
        grid=(batch, nc),
        in_specs=[col(0), col(1), col(2), col(3), vec, vec,
                  pl.BlockSpec((chunk, chunk), lambda b, c: (0, 0))],
        out_specs=pl.BlockSpec((1, rows, HG_WIDTH), lambda b, c: (b, jnp.maximum(c - lead_blocks, 0), 0)),
        out_shape=jax.ShapeDtypeStruct((batch, p_rows - LEAD, HG_WIDTH), BF16),
        scratch_shapes=[pltpu.VMEM((HG_HEADS, HG_D, HG_D), F32)],
        compiler_params=_cparams(("parallel", "arbitrary")),
        name="hgrn2",
    )(hg, hg, hg, hg, lb, gain, tri)


def _rms(x, gain):
    ms = jnp.mean(x * x, axis=-1, keepdims=True)
    return x * lax.rsqrt(ms + EPS) * gain


def _qprep_kernel(cq_ref, gain_ref, w_ref, cos_ref, sin_ref, o_ref):
    xn = _rms(cq_ref[...].astype(F32), gain_ref[...]).astype(BF16)
    cos = cos_ref[...]
    sin = sin_ref[...]
    for h in range(MLA_HEADS):
        y = jnp.dot(xn, w_ref[h], preferred_element_type=F32)
        o_ref[0, h, :, :MLA_NOPE] = (y[:, :MLA_NOPE] * MLA_SCALE).astype(o_ref.dtype)
        roped = y[:, MLA_NOPE:2 * MLA_NOPE] * cos + y[:, 2 * MLA_NOPE:] * sin
        o_ref[0, h, :, MLA_NOPE:] = (roped * MLA_SCALE).astype(o_ref.dtype)


def _qprep(cq, gain, wq, cos, sin, batch, seq):
    tm = _pick(seq, 512)
    nb = seq // tm
    return pl.pallas_call(
        _qprep_kernel,
        grid=(batch, nb),
        in_specs=[pl.BlockSpec((tm, MLA_Q_RANK), lambda b, i: (b * nb + i, 0)),
                  pl.BlockSpec((1, MLA_Q_RANK), lambda b, i: (0, 0)),
                  pl.BlockSpec((MLA_HEADS, MLA_Q_RANK, 3 * MLA_NOPE), lambda b, i: (0, 0, 0)),
                  pl.BlockSpec((tm, MLA_NOPE), lambda b, i: (i, 0)),
                  pl.BlockSpec((tm, MLA_NOPE), lambda b, i: (i, 0))],
        out_specs=pl.BlockSpec((1, MLA_HEADS, tm, MLA_QK), lambda b, i: (b, 0, i, 0)),
        out_shape=jax.ShapeDtypeStruct((batch, MLA_HEADS, seq, MLA_QK), BF16),
        compiler_params=_cparams(("parallel", "parallel")),
        name="mla_q",
    )(cq, gain, wq, cos, sin)


def _kvprep_kernel(kv_ref, gain_ref, w_ref, cos_ref, sin_ref, k_ref, v_ref):
    x = kv_ref[...]
    xn = _rms(x[:, :MLA_KV_RANK].astype(F32), gain_ref[...]).astype(BF16)
    kr = (x[:, MLA_KV_RANK:MLA_KV_RANK + MLA_NOPE].astype(F32) * cos_ref[...]
          + x[:, MLA_KV_RANK + MLA_NOPE:].astype(F32) * sin_ref[...]).astype(k_ref.dtype)
    for h in range(MLA_HEADS):
        y = jnp.dot(xn, w_ref[h], preferred_element_type=F32)
        k_ref[0, h, :, :MLA_NOPE] = y[:, :MLA_NOPE].astype(k_ref.dtype)
        k_ref[0, h, :, MLA_NOPE:] = kr
        v_ref[0, h] = y[:, MLA_NOPE:].astype(v_ref.dtype)


def _kvprep(kvp, gain, wkv, cos, sin, batch, p_rows):
    tm = p_rows // 2
    return pl.pallas_call(
        _kvprep_kernel,
        grid=(batch, 2),
        in_specs=[pl.BlockSpec((tm, MLA_KV_RANK + 2 * MLA_NOPE), lambda b, i: (b * 2 + i, 0)),
                  pl.BlockSpec((1, MLA_KV_RANK), lambda b, i: (0, 0)),
                  pl.BlockSpec((MLA_HEADS, MLA_KV_RANK, MLA_NOPE + MLA_V), lambda b, i: (0, 0, 0)),
                  pl.BlockSpec((tm, MLA_NOPE), lambda b, i: (i, 0)),
                  pl.BlockSpec((tm, MLA_NOPE), lambda b, i: (i, 0))],
        out_specs=[pl.BlockSpec((1, MLA_HEADS, tm, MLA_QK), lambda b, i: (b, 0, i, 0)),
                   pl.BlockSpec((1, MLA_HEADS, tm, MLA_V), lambda b, i: (b, 0, i, 0))],
        out_shape=[jax.ShapeDtypeStruct((batch, MLA_HEADS, p_rows, MLA_QK), BF16),
                   jax.ShapeDtypeStruct((batch, MLA_HEADS, p_rows, MLA_V), BF16)],
        compiler_params=_cparams(("parallel", "parallel")),
        name="mla_kv",
    )(kvp, gain, wkv, cos, sin)


def _flash_kernel(q_ref, k_ref, v_ref, o_ref, *, tq, nsplit):
    qi = pl.program_id(2)
    hq = tq // nsplit
    nt = (((1,), (1,)), ((), ()))

    def step(q, kb, vb, carry, mask):
        m, l, acc = carry
        s = lax.dot_general(q, kb, nt, preferred_element_type=F32)
        if mask is not None:
            s = jnp.where(mask, s, NEG)
        m_new = jnp.maximum(m, jnp.max(s, axis=-1, keepdims=True))
        p = jnp.exp(s - m_new)
        corr = jnp.exp(m - m_new)
        l = corr * l + jnp.sum(p, axis=-1, keepdims=True)
        acc = corr * acc + jnp.dot(p.astype(BF16), vb, preferred_element_type=F32)
        return m_new, l, acc

    qs = [q_ref[0, 0, i * hq:(i + 1) * hq, :] for i in range(nsplit)]
    init = (jnp.full((hq, 1), NEG, F32), jnp.zeros((hq, 1), F32), jnp.zeros((hq, MLA_V), F32))
    lead_mask = lax.broadcasted_iota(jnp.int32, (hq, LEAD), 1) >= N_PAD
    k_lead = k_ref[0, 0, 0:LEAD, :]
    v_lead = v_ref[0, 0, 0:LEAD, :]
    carries = tuple(step(qs[i], k_lead, v_lead, init, lead_mask) for i in range(nsplit))

    def body(j, carries):
        k0 = pl.multiple_of(LEAD + j * tq, LEAD)
        kb = k_ref[0, 0, pl.ds(k0, tq), :]
        vb = v_ref[0, 0, pl.ds(k0, tq), :]
        return tuple(step(qs[i], kb, vb, carries[i], None) for i in range(nsplit))

    carries = lax.fori_loop(0, qi, body, carries)

    d0 = pl.multiple_of(LEAD + qi * tq, LEAD)
    for i in range(nsplit):
        nk = (i + 1) * hq
        row = lax.broadcasted_iota(jnp.int32, (hq, nk), 0) + i * hq
        col = lax.broadcasted_iota(jnp.int32, (hq, nk), 1)
        _, l, acc = step(qs[i], k_ref[0, 0, pl.ds(d0, nk), :], v_ref[0, 0, pl.ds(d0, nk), :],
                         carries[i], col <= row)
        o_ref[0, i * hq:(i + 1) * hq, :] = (acc / l).astype(o_ref.dtype)


def _flash(q, k, v, batch, seq, p_rows, tq=512, nsplit=2):
    tq = min(tq, seq)
    return pl.pallas_call(
        functools.partial(_flash_kernel, tq=tq, nsplit=nsplit),
        grid=(batch, MLA_HEADS, seq // tq),
        in_specs=[pl.BlockSpec((1, 1, tq, MLA_QK), lambda b, h, i: (b, h, i, 0)),
                  pl.BlockSpec((1, 1, p_rows, MLA_QK), lambda b, h, i: (b, h, 0, 0)),
                  pl.BlockSpec((1, 1, p_rows, MLA_V), lambda b, h, i: (b, h, 0, 0))],
        out_specs=pl.BlockSpec((1, tq, MLA_V), lambda b, h, i: (b, i, h)),
        out_shape=jax.ShapeDtypeStruct((batch, seq, MLA_HEADS * MLA_V), BF16),
        compiler_params=_cparams(("parallel", "parallel", "arbitrary")),
        name="mla_flash",
    )(q, k, v)


def _layer_norm(z, gain, bias):
    mu = jnp.mean(z, axis=-1, keepdims=True)
    zc = z - mu
    var = jnp.mean(zc * zc, axis=-1, keepdims=True)
    return zc * lax.rsqrt(var + EPS) * gain + bias


def _mix_kernel(x_ref, hg_ref, at_ref, ga_ref, gb_ref, wph_ref, wpm_ref, wo_ref, gain_ref, bias_ref,
                h_ref, ht_ref):
    ya = jnp.dot(hg_ref[...], wph_ref[...], preferred_element_type=F32)
    yb = jnp.dot(at_ref[...], wpm_ref[...], preferred_element_type=F32)
    mix = (jax.nn.sigmoid(ga_ref[...].astype(F32)) * ya + jax.nn.sigmoid(gb_ref[...].astype(F32)) * yb)
    mixed = jnp.dot(mix.astype(BF16), wo_ref[...], preferred_element_type=F32)
    h = _layer_norm(ALPHA * x_ref[...] + mixed, gain_ref[...], bias_ref[...])
    h_ref[...] = h
    ht_ref[...] = h.T.astype(ht_ref.dtype)


def _mix(x2, hg_o, at_o, gates, wph, wpm, wo, gain, bias):
    t = x2.shape[0]
    tm = _pick(t, 512)
    full = lambda a: pl.BlockSpec(a.shape, lambda i: (0,) * a.ndim)
    return pl.pallas_call(
        _mix_kernel,
        grid=(t // tm,),
        in_specs=[pl.BlockSpec((tm, D_MODEL), lambda i: (i, 0)),
                  pl.BlockSpec((tm, HG_WIDTH), lambda i: (i, 0)),
                  pl.BlockSpec((tm, MLA_HEADS * MLA_V), lambda i: (i, 0)),
                  pl.BlockSpec((tm, D_MODEL), lambda i: (i, 0)),
                  pl.BlockSpec((tm, D_MODEL), lambda i: (i, 1)),
                  full(wph), full(wpm), full(wo), full(gain), full(bias)],
        out_specs=[pl.BlockSpec((tm, D_MODEL), lambda i: (i, 0)),
                   pl.BlockSpec((D_MODEL, tm), lambda i: (0, i))],
        out_shape=[jax.ShapeDtypeStruct((t, D_MODEL), F32),
                   jax.ShapeDtypeStruct((D_MODEL, t), BF16)],
        compiler_params=_cparams(("parallel",)),
        name="mix_ln1",
    )(x2, hg_o, at_o, gates, gates, wph, wpm, wo, gain, bias)


N_TOP = PEER_TOPK + 1
TOP_ROWS = 24
NO_RANK = 255.0
PAIR_LIMIT = [N_TOP // (a + 1) for a in range(N_TOP)]
N_WIDE = sum(1 for n in PAIR_LIMIT if n > 1)


def _top_rows(x, n, ranked=False):
    vals = []
    rank = jnp.full(x.shape, NO_RANK, F32) if ranked else None
    for r in range(n):
        m = jnp.max(x, axis=0, keepdims=True)
        vals.append(m)
        hit = x == m
        if ranked:
            rank = jnp.where(hit, float(r), rank)
        x = jnp.where(hit, -jnp.inf, x)
    return (vals, rank) if ranked else vals


def _stack_rows(rows, tokens):
    ri = lax.broadcasted_iota(jnp.int32, (TOP_ROWS, tokens), 0)
    out = jnp.full((TOP_ROWS, tokens), -jnp.inf, F32)
    for r, v in enumerate(rows):
        out = jnp.where(ri == r, v, out)
    return out


def _route_kernel(ht_ref, wq_ref, sk_ref, r2_ref, c1_ref, e1_ref, e2_ref):
    tokens = ht_ref.shape[1]
    ri = lax.broadcasted_iota(jnp.int32, (TOP_ROWS, tokens), 0)
    qp = jnp.dot(wq_ref[...], ht_ref[...], preferred_element_type=F32).astype(BF16)
    for h in range(PEER_HEADS):
        s1 = jnp.dot(sk_ref[2 * h], qp[(2 * h) * PEER_HALF:(2 * h + 1) * PEER_HALF],
                     preferred_element_type=F32)
        s2 = jnp.dot(sk_ref[2 * h + 1], qp[(2 * h + 1) * PEER_HALF:(2 * h + 2) * PEER_HALF],
                     preferred_element_type=F32)
        t1 = _top_rows(s1, N_TOP)
        t2, rank2 = _top_rows(s2, N_TOP, ranked=True)
        t1s = _stack_rows(t1, tokens)
        t2s = _stack_rows(t2, tokens)
        cands = [jnp.where(ri < PAIR_LIMIT[a], t1[a] + t2s, -jnp.inf) for a in range(N_WIDE)]
        cands.append(jnp.where((ri >= N_WIDE) & (ri < N_TOP), t1s + t2[0], -jnp.inf))
        best = _top_rows(jnp.concatenate(cands, axis=0), N_TOP)
        m = best[0]
        z = jnp.zeros_like(m)
        for r in range(PEER_TOPK):
            z = z + jnp.exp(best[r] - m)
        thr = 0.5 * (best[PEER_TOPK - 1] + best[PEER_TOPK]) - s1
        c1 = jnp.zeros_like(s1)
        for b in range(PEER_TOPK):
            c1 = c1 + jnp.where(t2[b] > thr, 1.0, 0.0)
        r2_ref[h] = rank2.astype(r2_ref.dtype)
        c1_ref[h] = c1.astype(c1_ref.dtype)
        e1_ref[h] = jnp.exp(s1 - t1[0] - jnp.log(z)).astype(e1_ref.dtype)
        e2_ref[h] = jnp.exp(s2 - t2[0]).astype(e2_ref.dtype)


def _route(ht, wqt, sk):
    t = ht.shape[1]
    tm = _pick(t, 256)
    out = jax.ShapeDtypeStruct((PEER_HEADS, PEER_NKEYS, t), ht.dtype)
    ospec = pl.BlockSpec((PEER_HEADS, PEER_NKEYS, tm), lambda i: (0, 0, i))
    return pl.pallas_call(
        _route_kernel,
        grid=(t // tm,),
        in_specs=[pl.BlockSpec((D_MODEL, tm), lambda i: (0, i)),
                  pl.BlockSpec(wqt.shape, lambda i: (0, 0)),
                  pl.BlockSpec(sk.shape, lambda i: (0, 0, 0))],
        out_specs=[ospec] * 4,
        out_shape=[out] * 4,
        compiler_params=_cparams(("parallel",)),
        name="peer_route",
    )(ht, wqt, sk)


GELU_C0 = 0.7978845608028654
GELU_C1 = GELU_C0 * 0.044715


def _gelu(a):
    half = 0.5 * a
    return half + half * jnp.tanh(a * (GELU_C0 + GELU_C1 * (a * a)))


def _expert_kernel(ht_ref, h_ref, u_ref, vt_ref, r2_ref, c1_ref, e1_ref, e2_ref, gain_ref, bias_ref,
                   o_ref, acc_ref, a_ref, act_ref, *, rows_per_tile):
    e = pl.program_id(1)

    @pl.when(e == 0)
    def _():
        acc_ref[...] = jnp.zeros_like(acc_ref)

    a_ref[...] = jnp.dot(u_ref[...], ht_ref[...], preferred_element_type=F32)
    for r in range(rows_per_tile):
        rs = slice(r * PEER_NKEYS, (r + 1) * PEER_NKEYS)
        g = None
        for h in range(PEER_HEADS):
            gate = jnp.where(r2_ref[h] < c1_ref[h, r:r + 1, :], e2_ref[h] * e1_ref[h, r:r + 1, :], 0.0)
            g = gate if g is None else g + gate
        act_ref[rs, :] = _gelu(a_ref[rs, :]).astype(act_ref.dtype) * g
    acc_ref[...] += jnp.dot(vt_ref[...], act_ref[...], preferred_element_type=F32)

    @pl.when(e == pl.num_programs(1) - 1)
    def _():
        z = ALPHA * h_ref[...] + acc_ref[...].T
        o_ref[...] = _layer_norm(z, gain_ref[...], bias_ref[...])


def _experts(ht, h1, u, vt, r2, c1, e1, e2, gain, bias, tm=512, te=1024):
    t = h1.shape[0]
    tm = _pick(t, tm)
    rows_per_tile = te // PEER_NKEYS
    hspec = pl.BlockSpec((PEER_HEADS, PEER_NKEYS, tm), lambda i, e: (0, 0, i))
    rspec = pl.BlockSpec((PEER_HEADS, rows_per_tile, tm), lambda i, e: (0, e, i))
    vec = pl.BlockSpec((1, D_MODEL), lambda i, e: (0, 0))
    return pl.pallas_call(
        functools.partial(_expert_kernel, rows_per_tile=rows_per_tile),
        grid=(t // tm, PEER_N // te),
        in_specs=[pl.BlockSpec((D_MODEL, tm), lambda i, e: (0, i)),
                  pl.BlockSpec((tm, D_MODEL), lambda i, e: (i, 0)),
                  pl.BlockSpec((te, D_MODEL), lambda i, e: (e, 0)),
                  pl.BlockSpec((D_MODEL, te), lambda i, e: (0, e)),
                  hspec, rspec, rspec, hspec, vec, vec],
        out_specs=pl.BlockSpec((tm, D_MODEL), lambda i, e: (i, 0)),
        out_shape=jax.ShapeDtypeStruct((t, D_MODEL), F32),
        scratch_shapes=[pltpu.VMEM((D_MODEL, tm), F32), pltpu.VMEM((te, tm), F32),
                        pltpu.VMEM((te, tm), ht.dtype)],
        compiler_params=_cparams(("parallel", "arbitrary")),
        name="peer_experts",
    )(ht, h1, u, vt, r2, c1, e1, e2, gain, bias)


def _rope_tables(pos):
    half = MLA_ROPE // 2
    inv_freq = ROPE_BASE ** (-jnp.arange(half, dtype=F32) / half)
    ang = pos.astype(F32)[:, None] * inv_freq[None, :]
    zeros = jnp.zeros((pos.shape[0], MLA_NOPE - MLA_ROPE), F32)
    cos, sin = jnp.cos(ang), jnp.sin(ang)
    return jnp.concatenate([cos, cos, zeros], axis=1), jnp.concatenate([sin, sin, zeros], axis=1)


def _rot_cols(w):
    half = MLA_ROPE // 2
    return jnp.concatenate([-w[..., half:], w[..., :half]], axis=-1)


def kernel(x, meta_tokens, hgrn_lb_logits, w_in, q_norm_gain, kv_norm_gain, w_uq, w_ukv, hgrn_norm_gain,
           w_proj_hgrn, w_proj_mla, w_out, ln1_gain, ln1_bias, peer_query, peer_sub_keys, peer_u, peer_v,
           ln2_gain, ln2_bias):
    batch, seq, d = x.shape
    p_rows = LEAD + seq
    t_real = batch * seq
    l = 0

    h_all = jnp.concatenate([jnp.zeros((batch, N_PAD, d), BF16),
                             jnp.broadcast_to(meta_tokens.astype(BF16)[None], (batch, N_META, d)),
                             x.astype(BF16)], axis=1).reshape(batch * p_rows, d)
    x2 = x.reshape(t_real, d)
    x_bf = x2.astype(BF16)

    lower_bounds = jnp.cumsum(jax.nn.softmax(hgrn_lb_logits.astype(F32), axis=0), axis=0)
    rows = jnp.arange(p_rows)
    cos_all, sin_all = _rope_tables(rows - N_PAD)

    w = w_in[l]
    o_hg = 4 * HG_WIDTH
    o_cq = o_hg + MLA_Q_RANK
    o_ckv = o_cq + MLA_KV_RANK
    o_kr = o_ckv + MLA_ROPE
    w_hg = w[:, :o_hg].astype(BF16)
    w_cq = w[:, o_hg:o_cq].astype(BF16)
    w_kr = w[:, o_ckv:o_kr]
    zpad = jnp.zeros((d, MLA_NOPE - MLA_ROPE), F32)
    w_kv = jnp.concatenate([w[:, o_cq:o_ckv], w_kr, zpad, _rot_cols(w_kr), zpad], axis=1).astype(BF16)
    w_g = w[:, o_kr:].astype(BF16)

    wq3 = w_uq[l].reshape(MLA_Q_RANK, MLA_HEADS, MLA_NOPE + MLA_ROPE)
    zq = jnp.zeros((MLA_Q_RANK, MLA_HEADS, MLA_NOPE - MLA_ROPE), F32)
    wq_rope = wq3[..., MLA_NOPE:]
    wq = jnp.concatenate([wq3[..., :MLA_NOPE], wq_rope, zq, _rot_cols(wq_rope), zq], axis=-1)
    wq = wq.transpose(1, 0, 2).astype(BF16)
    wkv = w_ukv[l].reshape(MLA_KV_RANK, MLA_HEADS, MLA_NOPE + MLA_V).transpose(1, 0, 2).astype(BF16)

    tm_all = _pick(batch * p_rows, 1024)
    tm_real = _pick(t_real, 1024)
    hg = _matmul(h_all, w_hg, tm_all, 1024)
    kvp = _matmul(h_all, w_kv, tm_all, w_kv.shape[1])
    cq = _matmul(x_bf, w_cq, tm_real, MLA_Q_RANK)
    gates = _matmul(x_bf, w_g, tm_real, 1024)

    hg_o = _hgrn(hg, lower_bounds[l][None, :], hgrn_norm_gain[l][None, :], batch, p_rows)

    q = _qprep(cq, q_norm_gain[l][None, :], wq, cos_all[LEAD:], sin_all[LEAD:], batch, seq)
    k, v = _kvprep(kvp, kv_norm_gain[l][None, :], wkv, cos_all, sin_all, batch, p_rows)
    at_o = _flash(q, k, v, batch, seq, p_rows)

    h1, h1t = _mix(x2, hg_o.reshape(t_real, HG_WIDTH), at_o.reshape(t_real, MLA_HEADS * MLA_V), gates,
                   w_proj_hgrn[l].astype(BF16), w_proj_mla[l].astype(BF16), w_out[l].astype(BF16),
                   ln1_gain[l][None, :], ln1_bias[l][None, :])

    wqt = peer_query[l].T.astype(BF16)
    sk = peer_sub_keys[l].reshape(PEER_HEADS * 2, PEER_NKEYS, PEER_HALF).astype(BF16)
    r2, c1, e1, e2 = _route(h1t, wqt, sk)

    out = _experts(h1t, h1, peer_u[l].astype(BF16), peer_v[l].T.astype(BF16), r2, c1, e1, e2,
                   ln2_gain[l][None, :], ln2_bias[l][None, :])
    return out.reshape(batch, seq, d)
```

```python
import functools

import numpy as np
import jax
import jax.numpy as jnp
from jax import lax
from jax.experimental import pallas as pl
from jax.experimental.pallas import tpu as pltpu

F32 = jnp.float32
BF16 = jnp.bfloat16

D_MODEL = 1024
DEPTH = 1
N_META = 16
LEAD = 128
N_PAD = LEAD - N_META

HG_HEADS = 8
HG_D = 128
HG_WIDTH = HG_HEADS * HG_D
HG_SUB = 16

MLA_HEADS = 16
MLA_NOPE = 128
MLA_ROPE = 64
MLA_V = 128
MLA_Q_RANK = 384
MLA_KV_RANK = 256
MLA_QK = 256
MLA_SCALE = (MLA_NOPE + MLA_ROPE) ** -0.5
ROPE_BASE = 10000.0

PEER_HEADS = 8
PEER_NKEYS = 128
PEER_N = PEER_NKEYS * PEER_NKEYS
PEER_HALF = 128
PEER_TOPK = 16

ALPHA = (2 * DEPTH) ** 0.25
EPS = 1e-5
NEG = -1e30

VMEM_LIMIT = 56 * 1024 * 1024


def _cparams(sem):
    return pltpu.CompilerParams(dimension_semantics=sem, vmem_limit_bytes=VMEM_LIMIT)


def _pick(n, pref):
    t = min(n, pref)
    while n % t:
        t -= 128
    return t


def _mm_kernel(x_ref, w_ref, o_ref):
    o_ref[...] = jnp.dot(x_ref[...], w_ref[...], preferred_element_type=F32).astype(o_ref.dtype)


def _matmul(x, w, tm, tn):
    m, k = x.shape
    out_dtype = x.dtype
    n = w.shape[1]
    return pl.pallas_call(
        _mm_kernel,
        grid=(m // tm, n // tn),
        in_specs=[pl.BlockSpec((tm, k), lambda i, j: (i, 0)),
                  pl.BlockSpec((k, tn), lambda i, j: (0, j))],
        out_specs=pl.BlockSpec((tm, tn), lambda i, j: (i, j)),
        out_shape=jax.ShapeDtypeStruct((m, n), out_dtype),
        compiler_params=_cparams(("parallel", "parallel")),
        name="in_proj",
    )(x, w)


def _hgrn_chunk(q, k, lf, iv, st, tri, masks, chunk):
    hi = lf.astype(BF16)
    r1 = lf - hi.astype(F32)
    mid = r1.astype(BF16)
    lo = (r1 - mid.astype(F32)).astype(BF16)
    g = (jnp.dot(tri, hi, preferred_element_type=F32)
         + jnp.dot(tri, mid, preferred_element_type=F32)
         + jnp.dot(tri, lo, preferred_element_type=F32))

    def bcast_row(r, n):
        return jnp.broadcast_to(g[r:r + 1, :], (n, HG_D))

    def prev_end(s):
        parts = [jnp.zeros((s, HG_D), F32)] + [bcast_row(b * s - 1, s) for b in range(1, chunk // s)]
        return parts[0] if len(parts) == 1 else jnp.concatenate(parts, axis=0)

    def own_end(s):
        parts = [bcast_row((b + 1) * s - 1, s) for b in range(chunk // s)]
        return parts[0] if len(parts) == 1 else jnp.concatenate(parts, axis=0)

    nt = (((1,), (1,)), ((), ()))
    tn = (((0,), (0,)), ((), ()))

    q_c = (q * jnp.exp(g)).astype(BF16)
    o = lax.dot_general(q_c, st.astype(BF16), nt, preferred_element_type=F32)
    k_c = (k * jnp.exp(own_end(chunk) - g)).astype(BF16)
    upd = lax.dot_general(iv, k_c, tn, preferred_element_type=F32)
    st_new = st * jnp.exp(g[chunk - 1:chunk, :]) + upd

    p16 = prev_end(HG_SUB)
    q_d = (q * jnp.exp(g - p16)).astype(BF16)
    k_d = (k * jnp.exp(p16 - g)).astype(BF16)
    a = jnp.where(masks[0], lax.dot_general(q_d, k_d, nt, preferred_element_type=F32), 0.0)
    s = HG_SUB
    lvl = 1
    while s < chunk:
        q_s = q_d if s == HG_SUB else (q * jnp.exp(g - prev_end(s))).astype(BF16)
        k_s = (k * jnp.exp(own_end(s) - g)).astype(BF16)
        a = jnp.where(masks[lvl], lax.dot_general(q_s, k_s, nt, preferred_element_type=F32), a)
        s *= 2
        lvl += 1
    o = o + jnp.dot(a.astype(BF16), iv, preferred_element_type=F32)
    return o, st_new


def _hgrn_kernel(q_ref, f_ref, i_ref, g_ref, lb_ref, gain_ref, tri_ref, o_ref, st_ref, *, chunk, rows):
    c = pl.program_id(1)

    @pl.when(c == 0)
    def _():
        st_ref[...] = jnp.zeros_like(st_ref)

    ri = lax.broadcasted_iota(jnp.int32, (chunk, chunk), 0)
    ci = lax.broadcasted_iota(jnp.int32, (chunk, chunk), 1)
    blk = lambda v, s: lax.shift_right_logical(v, s.bit_length() - 1)
    masks = [(blk(ri, HG_SUB) == blk(ci, HG_SUB)) & (ci <= ri)]
    s = HG_SUB
    while s < chunk:
        masks.append((blk(ri, 2 * s) == blk(ci, 2 * s)) & ((blk(ri, s) & 1) == 1) & ((blk(ci, s) & 1) == 0))
        s *= 2
    tri = tri_ref[...]

    for n in range(rows // chunk):
        rs = slice(n * chunk, (n + 1) * chunk)
        seq_row = c * rows + n * chunk + lax.broadcasted_iota(jnp.int32, (chunk, HG_D), 0)
        valid = seq_row >= N_PAD
        for h in range(HG_HEADS):
            hs = slice(h * HG_D, (h + 1) * HG_D)
            lb = lb_ref[:, hs]
            f = lb + (1.0 - lb) * jax.nn.sigmoid(f_ref[rs, hs].astype(F32))
            lf = jnp.where(valid, jnp.log(f), 0.0)
            k = jnp.where(valid, 1.0 - f, 0.0)
            q = q_ref[rs, hs].astype(F32)
            o, st_new = _hgrn_chunk(q, k, lf, i_ref[rs, hs], st_ref[h], tri, masks, chunk)
            st_ref[h] = st_new
            ms = jnp.mean(o * o, axis=-1, keepdims=True)
            o = o * lax.rsqrt(ms + EPS) * gain_ref[:, hs]
            gate = g_ref[rs, hs].astype(F32)
            o_ref[0, rs, hs] = (o * (gate * jax.nn.sigmoid(gate))).astype(o_ref.dtype)


def _hgrn(hg, lb, gain, batch, p_rows, chunk=64, rows=128):
    nc = p_rows // rows
    lead_blocks = LEAD // rows
    tri = jnp.asarray(np.tril(np.ones((chunk, chunk), np.float32)), BF16)
    col = lambda j: pl.BlockSpec((rows, HG_WIDTH), lambda b, c: (b * nc + c, j))
    vec = pl.BlockSpec((1, HG_WIDTH), lambda b, c: (0, 0))
    return pl.pallas_call(
        functools.partial(_hgrn_kernel, chunk=chunk, rows=rows),
        grid=(batch, nc),
        in_specs=[col(0), col(1), col(2), col(3), vec, vec,
                  pl.BlockSpec((chunk, chunk), lambda b, c: (0, 0))],
        out_specs=pl.BlockSpec((1, rows, HG_WIDTH), lambda b, c: (b, jnp.maximum(c - lead_blocks, 0), 0)),
        out_shape=jax.ShapeDtypeStruct((batch, p_rows - LEAD, HG_WIDTH), BF16),
        scratch_shapes=[pltpu.VMEM((HG_HEADS, HG_D, HG_D), F32)],
        compiler_params=_cparams(("parallel", "arbitrary")),
        name="hgrn2",
    )(hg, hg, hg, hg, lb, gain, tri)


def _rms(x, gain):
    ms = jnp.mean(x * x, axis=-1, keepdims=True)
    return x * lax.rsqrt(ms + EPS) * gain


def _qprep_kernel(cq_ref, gain_ref, w_ref, cos_ref, sin_ref, o_ref):
    xn = _rms(cq_ref[...].astype(F32), gain_ref[...]).astype(BF16)
    cos = cos_ref[...]
    sin = sin_ref[...]
    for h in range(MLA_HEADS):
        y = jnp.dot(xn, w_ref[h], preferred_element_type=F32)
        o_ref[0, h, :, :MLA_NOPE] = (y[:, :MLA_NOPE] * MLA_SCALE).astype(o_ref.dtype)
        roped = y[:, MLA_NOPE:2 * MLA_NOPE] * cos + y[:, 2 * MLA_NOPE:] * sin
        o_ref[0, h, :, MLA_NOPE:] = (roped * MLA_SCALE).astype(o_ref.dtype)


def _qprep(cq, gain, wq, cos, sin, batch, seq):
    tm = _pick(seq, 512)
    nb = seq // tm
    return pl.pallas_call(
        _qprep_kernel,
        grid=(batch, nb),
        in_specs=[pl.BlockSpec((tm, MLA_Q_RANK), lambda b, i: (b * nb + i, 0)),
                  pl.BlockSpec((1, MLA_Q_RANK), lambda b, i: (0, 0)),
                  pl.BlockSpec((MLA_HEADS, MLA_Q_RANK, 3 * MLA_NOPE), lambda b, i: (0, 0, 0)),
                  pl.BlockSpec((tm, MLA_NOPE), lambda b, i: (i, 0)),
                  pl.BlockSpec((tm, MLA_NOPE), lambda b, i: (i, 0))],
        out_specs=pl.BlockSpec((1, MLA_HEADS, tm, MLA_QK), lambda b, i: (b, 0, i, 0)),
        out_shape=jax.ShapeDtypeStruct((batch, MLA_HEADS, seq, MLA_QK), BF16),
        compiler_params=_cparams(("parallel", "parallel")),
        name="mla_q",
    )(cq, gain, wq, cos, sin)


def _kvprep_kernel(kv_ref, gain_ref, w_ref, cos_ref, sin_ref, k_ref, v_ref):
    x = kv_ref[...]
    xn = _rms(x[:, :MLA_KV_RANK].astype(F32), gain_ref[...]).astype(BF16)
    kr = (x[:, MLA_KV_RANK:MLA_KV_RANK + MLA_NOPE].astype(F32) * cos_ref[...]
          + x[:, MLA_KV_RANK + MLA_NOPE:].astype(F32) * sin_ref[...]).astype(k_ref.dtype)
    for h in range(MLA_HEADS):
        y = jnp.dot(xn, w_ref[h], preferred_element_type=F32)
        k_ref[0, h, :, :MLA_NOPE] = y[:, :MLA_NOPE].astype(k_ref.dtype)
        k_ref[0, h, :, MLA_NOPE:] = kr
        v_ref[0, h] = y[:, MLA_NOPE:].astype(v_ref.dtype)


def _kvprep(kvp, gain, wkv, cos, sin, batch, p_rows):
    tm = p_rows // 2
    return pl.pallas_call(
        _kvprep_kernel,
        grid=(batch, 2),
        in_specs=[pl.BlockSpec((tm, MLA_KV_RANK + 2 * MLA_NOPE), lambda b, i: (b * 2 + i, 0)),
                  pl.BlockSpec((1, MLA_KV_RANK), lambda b, i: (0, 0)),
                  pl.BlockSpec((MLA_HEADS, MLA_KV_RANK, MLA_NOPE + MLA_V), lambda b, i: (0, 0, 0)),
                  pl.BlockSpec((tm, MLA_NOPE), lambda b, i: (i, 0)),
                  pl.BlockSpec((tm, MLA_NOPE), lambda b, i: (i, 0))],
        out_specs=[pl.BlockSpec((1, MLA_HEADS, tm, MLA_QK), lambda b, i: (b, 0, i, 0)),
                   pl.BlockSpec((1, MLA_HEADS, tm, MLA_V), lambda b, i: (b, 0, i, 0))],
        out_shape=[jax.ShapeDtypeStruct((batch, MLA_HEADS, p_rows, MLA_QK), BF16),
                   jax.ShapeDtypeStruct((batch, MLA_HEADS, p_rows, MLA_V), BF16)],
        compiler_params=_cparams(("parallel", "parallel")),
        name="mla_kv",
    )(kvp, gain, wkv, cos, sin)


def _flash_segments(seq, tq, tk):
    segs = []
    for qi in range(seq // tq):
        pos = 0
        while pos < qi * tq:
            n = min(tk, qi * tq - pos)
            segs.append((qi, False, pos, n))
            pos += n
        segs.append((qi, True, qi * tq, tq))
    return segs


def _flash_kernel(q_ref, k_ref, v_ref, o_ref, *, tq, tk):
    seq = q_ref.shape[2]
    nt = (((1,), (1,)), ((), ()))
    causal = (lax.broadcasted_iota(jnp.int32, (tq, tq), 1) <= lax.broadcasted_iota(jnp.int32, (tq, tq), 0))
    lead_mask = lax.broadcasted_iota(jnp.int32, (tq, LEAD), 1) >= N_PAD

    def scores(seg):
        qi, diag, k0, n = seg
        q = q_ref[0, 0, qi * tq:(qi + 1) * tq, :]
        s = lax.dot_general(q, k_ref[0, 0, LEAD + k0:LEAD + k0 + n, :], nt, preferred_element_type=F32)
        if diag:
            s_lead = lax.dot_general(q, k_ref[0, 0, 0:LEAD, :], nt, preferred_element_type=F32)
            s = jnp.concatenate([jnp.where(causal, s, NEG), jnp.where(lead_mask, s_lead, NEG)], axis=1)
        return s

    segs = _flash_segments(seq, tq, tk)
    s_cur = scores(segs[0])
    m = l = acc = None
    for i, (qi, diag, k0, n) in enumerate(segs):
        s_next = scores(segs[i + 1]) if i + 1 < len(segs) else None
        first = k0 == 0
        m_new = jnp.max(s_cur, axis=-1, keepdims=True)
        if not first:
            m_new = jnp.maximum(m, m_new)
        p = jnp.exp(s_cur - m_new)
        p_sum = jnp.sum(p, axis=-1, keepdims=True)
        p = p.astype(BF16)
        if diag:
            pv = (jnp.dot(p[:, :tq], v_ref[0, 0, LEAD + k0:LEAD + k0 + n, :], preferred_element_type=F32)
                  + jnp.dot(p[:, tq:], v_ref[0, 0, 0:LEAD, :], preferred_element_type=F32))
        else:
            pv = jnp.dot(p, v_ref[0, 0, LEAD + k0:LEAD + k0 + n, :], preferred_element_type=F32)
        if first:
            l, acc = p_sum, pv
        else:
            corr = jnp.exp(m - m_new)
            l = corr * l + p_sum
            acc = corr * acc + pv
        m = m_new
        if diag:
            o_ref[0, qi * tq:(qi + 1) * tq, :] = (acc / l).astype(o_ref.dtype)
        s_cur = s_next


def _flash(q, k, v, batch, seq, p_rows, tq=256, tk=512):
    tq = min(tq, seq)
    return pl.pallas_call(
        functools.partial(_flash_kernel, tq=tq, tk=tk),
        grid=(batch, MLA_HEADS),
        in_specs=[pl.BlockSpec((1, 1, seq, MLA_QK), lambda b, h: (b, h, 0, 0)),
                  pl.BlockSpec((1, 1, p_rows, MLA_QK), lambda b, h: (b, h, 0, 0)),
                  pl.BlockSpec((1, 1, p_rows, MLA_V), lambda b, h: (b, h, 0, 0))],
        out_specs=pl.BlockSpec((1, seq, MLA_V), lambda b, h: (b, 0, h)),
        out_shape=jax.ShapeDtypeStruct((batch, seq, MLA_HEADS * MLA_V), BF16),
        compiler_params=_cparams(("parallel", "parallel")),
        name="mla_flash",
    )(q, k, v)


def _layer_norm(z, gain, bias):
    mu = jnp.mean(z, axis=-1, keepdims=True)
    zc = z - mu
    var = jnp.mean(zc * zc, axis=-1, keepdims=True)
    return zc * lax.rsqrt(var + EPS) * gain + bias


def _mix_kernel(x_ref, hg_ref, at_ref, ga_ref, gb_ref, wph_ref, wpm_ref, wo_ref, gain_ref, bias_ref,
                h_ref, ht_ref):
    ya = jnp.dot(hg_ref[...], wph_ref[...], preferred_element_type=F32)
    yb = jnp.dot(at_ref[...], wpm_ref[...], preferred_element_type=F32)
    mix = (jax.nn.sigmoid(ga_ref[...].astype(F32)) * ya + jax.nn.sigmoid(gb_ref[...].astype(F32)) * yb)
    mixed = jnp.dot(mix.astype(BF16), wo_ref[...], preferred_element_type=F32)
    h = _layer_norm(ALPHA * x_ref[...] + mixed, gain_ref[...], bias_ref[...])
    h_ref[...] = h
    ht_ref[...] = h.T.astype(ht_ref.dtype)


def _mix(x2, hg_o, at_o, gates, wph, wpm, wo, gain, bias):
    t = x2.shape[0]
    tm = _pick(t, 512)
    full = lambda a: pl.BlockSpec(a.shape, lambda i: (0,) * a.ndim)
    return pl.pallas_call(
        _mix_kernel,
        grid=(t // tm,),
        in_specs=[pl.BlockSpec((tm, D_MODEL), lambda i: (i, 0)),
                  pl.BlockSpec((tm, HG_WIDTH), lambda i: (i, 0)),
                  pl.BlockSpec((tm, MLA_HEADS * MLA_V), lambda i: (i, 0)),
                  pl.BlockSpec((tm, D_MODEL), lambda i: (i, 0)),
                  pl.BlockSpec((tm, D_MODEL), lambda i: (i, 1)),
                  full(wph), full(wpm), full(wo), full(gain), full(bias)],
        out_specs=[pl.BlockSpec((tm, D_MODEL), lambda i: (i, 0)),
                   pl.BlockSpec((D_MODEL, tm), lambda i: (0, i))],
        out_shape=[jax.ShapeDtypeStruct((t, D_MODEL), F32),
                   jax.ShapeDtypeStruct((D_MODEL, t), BF16)],
        compiler_params=_cparams(("parallel",)),
        name="mix_ln1",
    )(x2, hg_o, at_o, gates, gates, wph, wpm, wo, gain, bias)


N_TOP = PEER_TOPK + 1
TOP_ROWS = 24
NO_RANK = 255.0
PAIR_LIMIT = [N_TOP // (a + 1) for a in range(N_TOP)]
N_WIDE = sum(1 for n in PAIR_LIMIT if n > 1)


def _top_rows(x, n, ranked=False):
    vals = []
    rank = jnp.full(x.shape, NO_RANK, F32) if ranked else None
    for r in range(n):
        m = jnp.max(x, axis=0, keepdims=True)
        vals.append(m)
        hit = x == m
        if ranked:
            rank = jnp.where(hit, float(r), rank)
        x = jnp.where(hit, -jnp.inf, x)
    return (vals, rank) if ranked else vals


def _stack_rows(rows, tokens):
    ri = lax.broadcasted_iota(jnp.int32, (TOP_ROWS, tokens), 0)
    out = jnp.full((TOP_ROWS, tokens), -jnp.inf, F32)
    for r, v in enumerate(rows):
        out = jnp.where(ri == r, v, out)
    return out


def _route_kernel(ht_ref, wq_ref, sk_ref, r2_ref, c1_ref, e1_ref, e2_ref):
    tokens = ht_ref.shape[1]
    ri = lax.broadcasted_iota(jnp.int32, (TOP_ROWS, tokens), 0)
    qp = jnp.dot(wq_ref[...], ht_ref[...], preferred_element_type=F32).astype(BF16)
    for h in range(PEER_HEADS):
        s1 = jnp.dot(sk_ref[2 * h], qp[(2 * h) * PEER_HALF:(2 * h + 1) * PEER_HALF],
                     preferred_element_type=F32)
        s2 = jnp.dot(sk_ref[2 * h + 1], qp[(2 * h + 1) * PEER_HALF:(2 * h + 2) * PEER_HALF],
                     preferred_element_type=F32)
        t1 = _top_rows(s1, N_TOP)
        t2, rank2 = _top_rows(s2, N_TOP, ranked=True)
        t1s = _stack_rows(t1, tokens)
        t2s = _stack_rows(t2, tokens)
        cands = [jnp.where(ri < PAIR_LIMIT[a], t1[a] + t2s, -jnp.inf) for a in range(N_WIDE)]
        cands.append(jnp.where((ri >= N_WIDE) & (ri < N_TOP), t1s + t2[0], -jnp.inf))
        best = _top_rows(jnp.concatenate(cands, axis=0), N_TOP)
        m = best[0]
        z = jnp.zeros_like(m)
        for r in range(PEER_TOPK):
            z = z + jnp.exp(best[r] - m)
        thr = 0.5 * (best[PEER_TOPK - 1] + best[PEER_TOPK]) - s1
        c1 = jnp.zeros_like(s1)
        for b in range(PEER_TOPK):
            c1 = c1 + jnp.where(t2[b] > thr, 1.0, 0.0)
        r2_ref[h] = rank2.astype(r2_ref.dtype)
        c1_ref[h] = c1
        e1_ref[h] = jnp.exp(s1 - t1[0] - jnp.log(z))
        e2_ref[h] = jnp.exp(s2 - t2[0]).astype(e2_ref.dtype)


def _route(ht, wqt, sk):
    t = ht.shape[1]
    tm = _pick(t, 256)
    out = jax.ShapeDtypeStruct((PEER_HEADS, PEER_NKEYS, t), ht.dtype)
    out32 = jax.ShapeDtypeStruct((PEER_HEADS, PEER_NKEYS, t), F32)
    ospec = pl.BlockSpec((PEER_HEADS, PEER_NKEYS, tm), lambda i: (0, 0, i))
    return pl.pallas_call(
        _route_kernel,
        grid=(t // tm,),
        in_specs=[pl.BlockSpec((D_MODEL, tm), lambda i: (0, i)),
                  pl.BlockSpec(wqt.shape, lambda i: (0, 0)),
                  pl.BlockSpec(sk.shape, lambda i: (0, 0, 0))],
        out_specs=[ospec] * 4,
        out_shape=[out, out32, out32, out],
        compiler_params=_cparams(("parallel",)),
        name="peer_route",
    )(ht, wqt, sk)


GELU_C0 = 0.7978845608028654
GELU_C1 = GELU_C0 * 0.044715


def _gelu(a):
    half = 0.5 * a
    return half + half * jnp.tanh(a * (GELU_C0 + GELU_C1 * (a * a)))


def _expert_kernel(ht_ref, h_ref, u_ref, vt_ref, r2_ref, c1_ref, e1_ref, e2_ref, gain_ref, bias_ref,
                   o_ref, acc_ref, act_ref, *, chunk):
    e = pl.program_id(1)
    te, tm = act_ref.shape
    packed_rows = 16

    @pl.when(e == 0)
    def _():
        acc_ref[...] = jnp.zeros_like(acc_ref)

    def row_bcast(ref, h, r):
        row = jnp.broadcast_to(ref[h, r:r + 1, :], (packed_rows, tm)).astype(act_ref.dtype)
        return pltpu.repeat(row, PEER_NKEYS // packed_rows, axis=0)

    for c in range(te // chunk):
        a = jnp.dot(u_ref[c * chunk:(c + 1) * chunk, :], ht_ref[...], preferred_element_type=F32)
        for rr in range(chunk // PEER_NKEYS):
            r = c * (chunk // PEER_NKEYS) + rr
            g = None
            for h in range(PEER_HEADS):
                gate = jnp.where(r2_ref[h] < row_bcast(c1_ref, h, r), e2_ref[h] * row_bcast(e1_ref, h, r), 0.0)
                g = gate if g is None else g + gate
            a_r = a[rr * PEER_NKEYS:(rr + 1) * PEER_NKEYS, :]
            act_ref[r * PEER_NKEYS:(r + 1) * PEER_NKEYS, :] = _gelu(a_r).astype(act_ref.dtype) * g
    acc_ref[...] += jnp.dot(vt_ref[...], act_ref[...], preferred_element_type=F32)

    @pl.when(e == pl.num_programs(1) - 1)
    def _():
        z = ALPHA * h_ref[...] + acc_ref[...].T
        o_ref[...] = _layer_norm(z, gain_ref[...], bias_ref[...])


def _experts(ht, h1, u, vt, r2, c1, e1, e2, gain, bias, tm=512, te=1024, chunk=256):
    t = h1.shape[0]
    tm = _pick(t, tm)
    hspec = pl.BlockSpec((PEER_HEADS, PEER_NKEYS, tm), lambda i, e: (0, 0, i))
    rspec = pl.BlockSpec((PEER_HEADS, te // PEER_NKEYS, tm), lambda i, e: (0, e, i))
    vec = pl.BlockSpec((1, D_MODEL), lambda i, e: (0, 0))
    return pl.pallas_call(
        functools.partial(_expert_kernel, chunk=chunk),
        grid=(t // tm, PEER_N // te),
        in_specs=[pl.BlockSpec((D_MODEL, tm), lambda i, e: (0, i)),
                  pl.BlockSpec((tm, D_MODEL), lambda i, e: (i, 0)),
                  pl.BlockSpec((te, D_MODEL), lambda i, e: (e, 0)),
                  pl.BlockSpec((D_MODEL, te), lambda i, e: (0, e)),
                  hspec, rspec, rspec, hspec, vec, vec],
        out_specs=pl.BlockSpec((tm, D_MODEL), lambda i, e: (i, 0)),
        out_shape=jax.ShapeDtypeStruct((t, D_MODEL), F32),
        scratch_shapes=[pltpu.VMEM((D_MODEL, tm), F32), pltpu.VMEM((te, tm), ht.dtype)],
        compiler_params=_cparams(("parallel", "arbitrary")),
        name="peer_experts",
    )(ht, h1, u, vt, r2, c1, e1, e2, gain, bias)


def _rope_tables(pos):
    half = MLA_ROPE // 2
    inv_freq = ROPE_BASE ** (-jnp.arange(half, dtype=F32) / half)
    ang = pos.astype(F32)[:, None] * inv_freq[None, :]
    zeros = jnp.zeros((pos.shape[0], MLA_NOPE - MLA_ROPE), F32)
    cos, sin = jnp.cos(ang), jnp.sin(ang)
    return jnp.concatenate([cos, cos, zeros], axis=1), jnp.concatenate([sin, sin, zeros], axis=1)


def _rot_cols(w):
    half = MLA_ROPE // 2
    return jnp.concatenate([-w[..., half:], w[..., :half]], axis=-1)


def kernel(x, meta_tokens, hgrn_lb_logits, w_in, q_norm_gain, kv_norm_gain, w_uq, w_ukv, hgrn_norm_gain,
           w_proj_hgrn, w_proj_mla, w_out, ln1_gain, ln1_bias, peer_query, peer_sub_keys, peer_u, peer_v,
           ln2_gain, ln2_bias):
    batch, seq, d = x.shape
    p_rows = LEAD + seq
    t_real = batch * seq
    l = 0

    h_all = jnp.concatenate([jnp.zeros((batch, N_PAD, d), BF16),
                             jnp.broadcast_to(meta_tokens.astype(BF16)[None], (batch, N_META, d)),
                             x.astype(BF16)], axis=1).reshape(batch * p_rows, d)
    x2 = x.reshape(t_real, d)
    x_bf = x2.astype(BF16)

    lower_bounds = jnp.cumsum(jax.nn.softmax(hgrn_lb_logits.astype(F32), axis=0), axis=0)
    rows = jnp.arange(p_rows)
    cos_all, sin_all = _rope_tables(rows - N_PAD)

    w = w_in[l]
    o_hg = 4 * HG_WIDTH
    o_cq = o_hg + MLA_Q_RANK
    o_ckv = o_cq + MLA_KV_RANK
    o_kr = o_ckv + MLA_ROPE
    w_hg = w[:, :o_hg].astype(BF16)
    w_cq = w[:, o_hg:o_cq].astype(BF16)
    w_kr = w[:, o_ckv:o_kr]
    zpad = jnp.zeros((d, MLA_NOPE - MLA_ROPE), F32)
    w_kv = jnp.concatenate([w[:, o_cq:o_ckv], w_kr, zpad, _rot_cols(w_kr), zpad], axis=1).astype(BF16)
    w_g = w[:, o_kr:].astype(BF16)

    wq3 = w_uq[l].reshape(MLA_Q_RANK, MLA_HEADS, MLA_NOPE + MLA_ROPE)
    zq = jnp.zeros((MLA_Q_RANK, MLA_HEADS, MLA_NOPE - MLA_ROPE), F32)
    wq_rope = wq3[..., MLA_NOPE:]
    wq = jnp.concatenate([wq3[..., :MLA_NOPE], wq_rope, zq, _rot_cols(wq_rope), zq], axis=-1)
    wq = wq.transpose(1, 0, 2).astype(BF16)
    wkv = w_ukv[l].reshape(MLA_KV_RANK, MLA_HEADS, MLA_NOPE + MLA_V).transpose(1, 0, 2).astype(BF16)

    tm_all = _pick(batch * p_rows, 1024)
    tm_real = _pick(t_real, 1024)
    hg = _matmul(h_all, w_hg, tm_all, 1024)
    kvp = _matmul(h_all, w_kv, tm_all, w_kv.shape[1])
    cq = _matmul(x_bf, w_cq, tm_real, MLA_Q_RANK)
    gates = _matmul(x_bf, w_g, tm_real, 1024)

    hg_o = _hgrn(hg, lower_bounds[l][None, :], hgrn_norm_gain[l][None, :], batch, p_rows)

    q = _qprep(cq, q_norm_gain[l][None, :], wq, cos_all[LEAD:], sin_all[LEAD:], batch, seq)
    k, v = _kvprep(kvp, kv_norm_gain[l][None, :], wkv, cos_all, sin_all, batch, p_rows)
    at_o = _flash(q, k, v, batch, seq, p_rows)

    h1, h1t = _mix(x2, hg_o.reshape(t_real, HG_WIDTH), at_o.reshape(t_real, MLA_HEADS * MLA_V), gates,
                   w_proj_hgrn[l].astype(BF16), w_proj_mla[l].astype(BF16), w_out[l].astype(BF16),
                   ln1_gain[l][None, :], ln1_bias[l][None, :])

    wqt = peer_query[l].T.astype(BF16)
    sk = peer_sub_keys[l].reshape(PEER_HEADS * 2, PEER_NKEYS, PEER_HALF).astype(BF16)
    r2, c1, e1, e2 = _route(h1t, wqt, sk)

    out = _experts(h1t, h1, peer_u[l].astype(BF16), peer_v[l].T.astype(BF16), r2, c1, e1, e2,
                   ln2_gain[l][None, :], ln2_bias[l][None, :])
    return out.reshape(batch, seq, d)
```

```python
import functools

import numpy as np
import jax
import jax.numpy as jnp
from jax import lax
from jax.experimental import pallas as pl
from jax.experimental.pallas import tpu as pltpu

F32 = jnp.float32
BF16 = jnp.bfloat16

D_MODEL = 1024
DEPTH = 1
N_META = 16
LEAD = 128
N_PAD = LEAD - N_META

HG_HEADS = 8
HG_D = 128
HG_WIDTH = HG_HEADS * HG_D
HG_SUB = 16

MLA_HEADS = 16
MLA_NOPE = 128
MLA_ROPE = 64
MLA_V = 128
MLA_Q_RANK = 384
MLA_KV_RANK = 256
MLA_QK = 256
MLA_SCALE = (MLA_NOPE + MLA_ROPE) ** -0.5
ROPE_BASE = 10000.0

PEER_HEADS = 8
PEER_NKEYS = 128
PEER_N = PEER_NKEYS * PEER_NKEYS
PEER_HALF = 128
PEER_TOPK = 16

ALPHA = (2 * DEPTH) ** 0.25
EPS = 1e-5
NEG = -1e30

VMEM_LIMIT = 56 * 1024 * 1024


def _cparams(sem):
    return pltpu.CompilerParams(dimension_semantics=sem, vmem_limit_bytes=VMEM_LIMIT)


def _pick(n, pref):
    t = min(n, pref)
    while n % t:
        t -= 128
    return t


def _mm_kernel(x_ref, w_ref, o_ref):
    o_ref[...] = jnp.dot(x_ref[...], w_ref[...], preferred_element_type=F32).astype(o_ref.dtype)


def _matmul(x, w, tm, tn):
    m, k = x.shape
    out_dtype = x.dtype
    n = w.shape[1]
    return pl.pallas_call(
        _mm_kernel,
        grid=(m // tm, n // tn),
        in_specs=[pl.BlockSpec((tm, k), lambda i, j: (i, 0)),
                  pl.BlockSpec((k, tn), lambda i, j: (0, j))],
        out_specs=pl.BlockSpec((tm, tn), lambda i, j: (i, j)),
        out_shape=jax.ShapeDtypeStruct((m, n), out_dtype),
        compiler_params=_cparams(("parallel", "parallel")),
        name="in_proj",
    )(x, w)


def _hgrn_chunk(q, k, lf, iv, st_ref, tri, masks, chunk):
    width = q.shape[1]
    hi = lf.astype(BF16)
    r1 = lf - hi.astype(F32)
    mid = r1.astype(BF16)
    lo = (r1 - mid.astype(F32)).astype(BF16)
    g = (jnp.dot(tri, hi, preferred_element_type=F32)
         + jnp.dot(tri, mid, preferred_element_type=F32)
         + jnp.dot(tri, lo, preferred_element_type=F32))

    def bcast_row(r, n):
        return jnp.broadcast_to(g[r:r + 1, :], (n, width))

    def prev_end(s):
        parts = [jnp.zeros((s, width), F32)] + [bcast_row(b * s - 1, s) for b in range(1, chunk // s)]
        return parts[0] if len(parts) == 1 else jnp.concatenate(parts, axis=0)

    def own_end(s):
        parts = [bcast_row((b + 1) * s - 1, s) for b in range(chunk // s)]
        return parts[0] if len(parts) == 1 else jnp.concatenate(parts, axis=0)

    q_c = (q * jnp.exp(g)).astype(BF16)
    k_c = (k * jnp.exp(own_end(chunk) - g)).astype(BF16)
    dec = jnp.exp(g[chunk - 1:chunk, :])
    p16 = prev_end(HG_SUB)
    q_lv = [(q * jnp.exp(g - p16)).astype(BF16)]
    k_lv = [(k * jnp.exp(p16 - g)).astype(BF16)]
    s = HG_SUB
    while s < chunk:
        q_lv.append(q_lv[0] if s == HG_SUB else (q * jnp.exp(g - prev_end(s))).astype(BF16))
        k_lv.append((k * jnp.exp(own_end(s) - g)).astype(BF16))
        s *= 2

    nt = (((1,), (1,)), ((), ()))
    tn = (((0,), (0,)), ((), ()))
    heads = range(HG_HEADS)
    hs = [slice(h * HG_D, (h + 1) * HG_D) for h in heads]
    st = [st_ref[h] for h in heads]
    o_inter = [lax.dot_general(q_c[:, hs[h]], st[h].astype(BF16), nt, preferred_element_type=F32) for h in heads]
    upd = [lax.dot_general(iv[:, hs[h]], k_c[:, hs[h]], tn, preferred_element_type=F32) for h in heads]
    scores = [[lax.dot_general(ql[:, hs[h]], kl[:, hs[h]], nt, preferred_element_type=F32) for h in heads]
              for ql, kl in zip(q_lv, k_lv)]
    outs, st_new = [], []
    for h in heads:
        a = jnp.where(masks[0], scores[0][h], 0.0)
        for lvl in range(1, len(masks)):
            a = jnp.where(masks[lvl], scores[lvl][h], a)
        outs.append(o_inter[h] + jnp.dot(a.astype(BF16), iv[:, hs[h]], preferred_element_type=F32))
        st_new.append(st[h] * dec[:, hs[h]] + upd[h])
    return outs, st_new


def _hgrn_kernel(q_ref, f_ref, i_ref, g_ref, lb_ref, gain_ref, tri_ref, o_ref, st_ref, *, chunk, rows):
    c = pl.program_id(1)

    @pl.when(c == 0)
    def _():
        st_ref[...] = jnp.zeros_like(st_ref)

    ri = lax.broadcasted_iota(jnp.int32, (chunk, chunk), 0)
    ci = lax.broadcasted_iota(jnp.int32, (chunk, chunk), 1)
    blk = lambda v, s: lax.shift_right_logical(v, s.bit_length() - 1)
    masks = [(blk(ri, HG_SUB) == blk(ci, HG_SUB)) & (ci <= ri)]
    s = HG_SUB
    while s < chunk:
        masks.append((blk(ri, 2 * s) == blk(ci, 2 * s)) & ((blk(ri, s) & 1) == 1) & ((blk(ci, s) & 1) == 0))
        s *= 2
    tri = tri_ref[...]
    lb = lb_ref[...]

    for n in range(rows // chunk):
        rs = slice(n * chunk, (n + 1) * chunk)
        seq_row = c * rows + n * chunk + lax.broadcasted_iota(jnp.int32, (chunk, HG_WIDTH), 0)
        valid = seq_row >= N_PAD
        f = lb + (1.0 - lb) * jax.nn.sigmoid(f_ref[rs, :].astype(F32))
        lf = jnp.where(valid, jnp.log(f), 0.0)
        k = jnp.where(valid, 1.0 - f, 0.0)
        outs, st_new = _hgrn_chunk(q_ref[rs, :].astype(F32), k, lf, i_ref[rs, :], st_ref, tri, masks, chunk)
        gate = g_ref[rs, :].astype(F32)
        gate = gate * jax.nn.sigmoid(gate)
        for h in range(HG_HEADS):
            hs = slice(h * HG_D, (h + 1) * HG_D)
            st_ref[h] = st_new[h]
            o = outs[h]
            ms = jnp.mean(o * o, axis=-1, keepdims=True)
            o = o * lax.rsqrt(ms + EPS) * gain_ref[:, hs]
            o_ref[0, rs, hs] = (o * gate[:, hs]).astype(o_ref.dtype)


def _hgrn(hg, lb, gain, batch, p_rows, chunk=64, rows=128):
    nc = p_rows // rows
    lead_blocks = LEAD // rows
    tri = jnp.asarray(np.tril(np.ones((chunk, chunk), np.float32)), BF16)
    col = lambda j: pl.BlockSpec((rows, HG_WIDTH), lambda b, c: (b * nc + c, j))
    vec = pl.BlockSpec((1, HG_WIDTH), lambda b, c: (0, 0))
    return pl.pallas_call(
        functools.partial(_hgrn_kernel, chunk=chunk, rows=rows),
        grid=(batch, nc),
        in_specs=[col(0), col(1), col(2), col(3), vec, vec,
                  pl.BlockSpec((chunk, chunk), lambda b, c: (0, 0))],
        out_specs=pl.BlockSpec((1, rows, HG_WIDTH), lambda b, c: (b, jnp.maximum(c - lead_blocks, 0), 0)),
        out_shape=jax.ShapeDtypeStruct((batch, p_rows - LEAD, HG_WIDTH), BF16),
        scratch_shapes=[pltpu.VMEM((HG_HEADS, HG_D, HG_D), F32)],
        compiler_params=_cparams(("parallel", "arbitrary")),
        name="hgrn2",
    )(hg, hg, hg, hg, lb, gain, tri)


def _rms(x, gain):
    ms = jnp.mean(x * x, axis=-1, keepdims=True)
    return x * lax.rsqrt(ms + EPS) * gain


def _qprep_kernel(cq_ref, gain_ref, w_ref, cos_ref, sin_ref, o_ref):
    xn = _rms(cq_ref[...].astype(F32), gain_ref[...]).astype(BF16)
    cos = cos_ref[...]
    sin = sin_ref[...]
    for h in range(MLA_HEADS):
        y = jnp.dot(xn, w_ref[h], preferred_element_type=F32)
        o_ref[0, h, :, :MLA_NOPE] = (y[:, :MLA_NOPE] * MLA_SCALE).astype(o_ref.dtype)
        roped = y[:, MLA_NOPE:2 * MLA_NOPE] * cos + y[:, 2 * MLA_NOPE:] * sin
        o_ref[0, h, :, MLA_NOPE:] = (roped * MLA_SCALE).astype(o_ref.dtype)


def _qprep(cq, gain, wq, cos, sin, batch, seq):
    tm = _pick(seq, 512)
    nb = seq // tm
    return pl.pallas_call(
        _qprep_kernel,
        grid=(batch, nb),
        in_specs=[pl.BlockSpec((tm, MLA_Q_RANK), lambda b, i: (b * nb + i, 0)),
                  pl.BlockSpec((1, MLA_Q_RANK), lambda b, i: (0, 0)),
                  pl.BlockSpec((MLA_HEADS, MLA_Q_RANK, 3 * MLA_NOPE), lambda b, i: (0, 0, 0)),
                  pl.BlockSpec((tm, MLA_NOPE), lambda b, i: (i, 0)),
                  pl.BlockSpec((tm, MLA_NOPE), lambda b, i: (i, 0))],
        out_specs=pl.BlockSpec((1, MLA_HEADS, tm, MLA_QK), lambda b, i: (b, 0, i, 0)),
        out_shape=jax.ShapeDtypeStruct((batch, MLA_HEADS, seq, MLA_QK), BF16),
        compiler_params=_cparams(("parallel", "parallel")),
        name="mla_q",
    )(cq, gain, wq, cos, sin)


def _kvprep_kernel(kv_ref, gain_ref, w_ref, cos_ref, sin_ref, k_ref, v_ref):
    x = kv_ref[...]
    xn = _rms(x[:, :MLA_KV_RANK].astype(F32), gain_ref[...]).astype(BF16)
    kr = (x[:, MLA_KV_RANK:MLA_KV_RANK + MLA_NOPE].astype(F32) * cos_ref[...]
          + x[:, MLA_KV_RANK + MLA_NOPE:].astype(F32) * sin_ref[...]).astype(k_ref.dtype)
    for h in range(MLA_HEADS):
        y = jnp.dot(xn, w_ref[h], preferred_element_type=F32)
        k_ref[0, h, :, :MLA_NOPE] = y[:, :MLA_NOPE].astype(k_ref.dtype)
        k_ref[0, h, :, MLA_NOPE:] = kr
        v_ref[0, h] = y[:, MLA_NOPE:].astype(v_ref.dtype)


def _kvprep(kvp, gain, wkv, cos, sin, batch, p_rows):
    tm = p_rows // 2
    return pl.pallas_call(
        _kvprep_kernel,
        grid=(batch, 2),
        in_specs=[pl.BlockSpec((tm, MLA_KV_RANK + 2 * MLA_NOPE), lambda b, i: (b * 2 + i, 0)),
                  pl.BlockSpec((1, MLA_KV_RANK), lambda b, i: (0, 0)),
                  pl.BlockSpec((MLA_HEADS, MLA_KV_RANK, MLA_NOPE + MLA_V), lambda b, i: (0, 0, 0)),
                  pl.BlockSpec((tm, MLA_NOPE), lambda b, i: (i, 0)),
                  pl.BlockSpec((tm, MLA_NOPE), lambda b, i: (i, 0))],
        out_specs=[pl.BlockSpec((1, MLA_HEADS, tm, MLA_QK), lambda b, i: (b, 0, i, 0)),
                   pl.BlockSpec((1, MLA_HEADS, tm, MLA_V), lambda b, i: (b, 0, i, 0))],
        out_shape=[jax.ShapeDtypeStruct((batch, MLA_HEADS, p_rows, MLA_QK), BF16),
                   jax.ShapeDtypeStruct((batch, MLA_HEADS, p_rows, MLA_V), BF16)],
        compiler_params=_cparams(("parallel", "parallel")),
        name="mla_kv",
    )(kvp, gain, wkv, cos, sin)


def _flash_segments(seq, tq, tk):
    segs = []
    for qi in range(seq // tq):
        pos = 0
        while pos < qi * tq:
            n = min(tk, qi * tq - pos)
            segs.append((qi, False, pos, n))
            pos += n
        segs.append((qi, True, qi * tq, tq))
    return segs


def _flash_kernel(q_ref, k_ref, v_ref, o_ref, *, tq, tk):
    seq = q_ref.shape[2]
    nt = (((1,), (1,)), ((), ()))
    causal = (lax.broadcasted_iota(jnp.int32, (tq, tq), 1) <= lax.broadcasted_iota(jnp.int32, (tq, tq), 0))
    lead_mask = lax.broadcasted_iota(jnp.int32, (tq, LEAD), 1) >= N_PAD

    def scores(seg):
        qi, diag, k0, n = seg
        q = q_ref[0, 0, qi * tq:(qi + 1) * tq, :]
        s = lax.dot_general(q, k_ref[0, 0, LEAD + k0:LEAD + k0 + n, :], nt, preferred_element_type=F32)
        if diag:
            s_lead = lax.dot_general(q, k_ref[0, 0, 0:LEAD, :], nt, preferred_element_type=F32)
            s = jnp.concatenate([jnp.where(causal, s, NEG), jnp.where(lead_mask, s_lead, NEG)], axis=1)
        return s

    segs = _flash_segments(seq, tq, tk)
    s_cur = scores(segs[0])
    m = l = acc = None
    for i, (qi, diag, k0, n) in enumerate(segs):
        s_next = scores(segs[i + 1]) if i + 1 < len(segs) else None
        first = k0 == 0
        m_new = jnp.max(s_cur, axis=-1, keepdims=True)
        if not first:
            m_new = jnp.maximum(m, m_new)
        p = jnp.exp(s_cur - m_new)
        p_sum = jnp.sum(p, axis=-1, keepdims=True)
        p = p.astype(BF16)
        if diag:
            pv = (jnp.dot(p[:, :tq], v_ref[0, 0, LEAD + k0:LEAD + k0 + n, :], preferred_element_type=F32)
                  + jnp.dot(p[:, tq:], v_ref[0, 0, 0:LEAD, :], preferred_element_type=F32))
        else:
            pv = jnp.dot(p, v_ref[0, 0, LEAD + k0:LEAD + k0 + n, :], preferred_element_type=F32)
        if first:
            l, acc = p_sum, pv
        else:
            corr = jnp.exp(m - m_new)
            l = corr * l + p_sum
            acc = corr * acc + pv
        m = m_new
        if diag:
            o_ref[0, qi * tq:(qi + 1) * tq, :] = (acc / l).astype(o_ref.dtype)
        s_cur = s_next


def _flash(q, k, v, batch, seq, p_rows, tq=256, tk=512):
    tq = min(tq, seq)
    return pl.pallas_call(
        functools.partial(_flash_kernel, tq=tq, tk=tk),
        grid=(batch, MLA_HEADS),
        in_specs=[pl.BlockSpec((1, 1, seq, MLA_QK), lambda b, h: (b, h, 0, 0)),
                  pl.BlockSpec((1, 1, p_rows, MLA_QK), lambda b, h: (b, h, 0, 0)),
                  pl.BlockSpec((1, 1, p_rows, MLA_V), lambda b, h: (b, h, 0, 0))],
        out_specs=pl.BlockSpec((1, seq, MLA_V), lambda b, h: (b, 0, h)),
        out_shape=jax.ShapeDtypeStruct((batch, seq, MLA_HEADS * MLA_V), BF16),
        compiler_params=_cparams(("parallel", "parallel")),
        name="mla_flash",
    )(q, k, v)


def _layer_norm(z, gain, bias):
    mu = jnp.mean(z, axis=-1, keepdims=True)
    zc = z - mu
    var = jnp.mean(zc * zc, axis=-1, keepdims=True)
    return zc * lax.rsqrt(var + EPS) * gain + bias


def _mix_kernel(x_ref, hg_ref, at_ref, ga_ref, gb_ref, wph_ref, wpm_ref, wo_ref, gain_ref, bias_ref,
                h_ref, ht_ref):
    ya = jnp.dot(hg_ref[...], wph_ref[...], preferred_element_type=F32)
    yb = jnp.dot(at_ref[...], wpm_ref[...], preferred_element_type=F32)
    mix = (jax.nn.sigmoid(ga_ref[...].astype(F32)) * ya + jax.nn.sigmoid(gb_ref[...].astype(F32)) * yb)
    mixed = jnp.dot(mix.astype(BF16), wo_ref[...], preferred_element_type=F32)
    h = _layer_norm(ALPHA * x_ref[...] + mixed, gain_ref[...], bias_ref[...])
    h_ref[...] = h
    ht_ref[...] = h.T.astype(ht_ref.dtype)


def _mix(x2, hg_o, at_o, gates, wph, wpm, wo, gain, bias):
    t = x2.shape[0]
    tm = _pick(t, 512)
    full = lambda a: pl.BlockSpec(a.shape, lambda i: (0,) * a.ndim)
    return pl.pallas_call(
        _mix_kernel,
        grid=(t // tm,),
        in_specs=[pl.BlockSpec((tm, D_MODEL), lambda i: (i, 0)),
                  pl.BlockSpec((tm, HG_WIDTH), lambda i: (i, 0)),
                  pl.BlockSpec((tm, MLA_HEADS * MLA_V), lambda i: (i, 0)),
                  pl.BlockSpec((tm, D_MODEL), lambda i: (i, 0)),
                  pl.BlockSpec((tm, D_MODEL), lambda i: (i, 1)),
                  full(wph), full(wpm), full(wo), full(gain), full(bias)],
        out_specs=[pl.BlockSpec((tm, D_MODEL), lambda i: (i, 0)),
                   pl.BlockSpec((D_MODEL, tm), lambda i: (0, i))],
        out_shape=[jax.ShapeDtypeStruct((t, D_MODEL), F32),
                   jax.ShapeDtypeStruct((D_MODEL, t), BF16)],
        compiler_params=_cparams(("parallel",)),
        name="mix_ln1",
    )(x2, hg_o, at_o, gates, gates, wph, wpm, wo, gain, bias)


N_TOP = PEER_TOPK + 1
TOP_ROWS = 24
NO_RANK = 255.0
PAIR_LIMIT = [N_TOP // (a + 1) for a in range(N_TOP)]
N_WIDE = sum(1 for n in PAIR_LIMIT if n > 1)
ROUTE_GROUP = 2


def _top_rows(xs, n, ranked):
    xs = list(xs)
    vals = [[] for _ in xs]
    ranks = [jnp.full(x.shape, NO_RANK, F32) if rk else None for x, rk in zip(xs, ranked)]
    for r in range(n):
        for i, x in enumerate(xs):
            m = jnp.max(x, axis=0, keepdims=True)
            vals[i].append(m)
            hit = x == m
            if ranked[i]:
                ranks[i] = jnp.where(hit, float(r), ranks[i])
            xs[i] = jnp.where(hit, -jnp.inf, x)
    return vals, ranks


def _stack_rows(rows, tokens):
    ri = lax.broadcasted_iota(jnp.int32, (TOP_ROWS, tokens), 0)
    out = jnp.full((TOP_ROWS, tokens), -jnp.inf, F32)
    for r, v in enumerate(rows):
        out = jnp.where(ri == r, v, out)
    return out


def _route_kernel(ht_ref, wq_ref, sk_ref, r2_ref, c1_ref, e1_ref, e2_ref):
    tokens = ht_ref.shape[1]
    ri8 = lax.broadcasted_iota(jnp.int32, (N_WIDE, tokens), 0)
    qp = jnp.dot(wq_ref[...], ht_ref[...], preferred_element_type=F32).astype(BF16)
    for h0 in range(0, PEER_HEADS, ROUTE_GROUP):
        heads = range(h0, h0 + ROUTE_GROUP)
        s = [jnp.dot(sk_ref[hp], qp[hp * PEER_HALF:(hp + 1) * PEER_HALF], preferred_element_type=F32)
             for hp in range(2 * h0, 2 * (h0 + ROUTE_GROUP))]
        tops, ranks = _top_rows(s, N_TOP, [False, True] * ROUTE_GROUP)
        cand = []
        for g in range(ROUTE_GROUP):
            t1, t2 = tops[2 * g], tops[2 * g + 1]
            t1s = _stack_rows(t1, tokens)
            t2s = _stack_rows(t2, tokens)
            parts = [jnp.where(ri8 < min(PAIR_LIMIT[a], N_WIDE), t1[a] + t2s[:N_WIDE], -jnp.inf)
                     for a in range(N_WIDE)]
            parts.append(t1[0] + t2s[N_WIDE:])
            parts.append(t1s[N_WIDE:] + t2[0])
            cand.append(jnp.concatenate(parts, axis=0))
        bests, _ = _top_rows(cand, N_TOP, [False] * ROUTE_GROUP)
        for g, h in enumerate(heads):
            s1, s2 = s[2 * g], s[2 * g + 1]
            t1, t2, best = tops[2 * g], tops[2 * g + 1], bests[g]
            m = best[0]
            z = jnp.zeros_like(m)
            for r in range(PEER_TOPK):
                z = z + jnp.exp(best[r] - m)
            thr = 0.5 * (best[PEER_TOPK - 1] + best[PEER_TOPK]) - s1
            c1 = jnp.zeros_like(s1)
            for b in range(PEER_TOPK):
                c1 = c1 + jnp.where(t2[b] > thr, 1.0, 0.0)
            r2_ref[h] = ranks[2 * g + 1].astype(r2_ref.dtype)
            c1_ref[h] = c1
            e1_ref[h] = jnp.exp(s1 - t1[0] - jnp.log(z))
            e2_ref[h] = jnp.exp(s2 - t2[0]).astype(e2_ref.dtype)


def _route(ht, wqt, sk):
    t = ht.shape[1]
    tm = _pick(t, 256)
    out = jax.ShapeDtypeStruct((PEER_HEADS, PEER_NKEYS, t), ht.dtype)
    out32 = jax.ShapeDtypeStruct((PEER_HEADS, PEER_NKEYS, t), F32)
    ospec = pl.BlockSpec((PEER_HEADS, PEER_NKEYS, tm), lambda i: (0, 0, i))
    return pl.pallas_call(
        _route_kernel,
        grid=(t // tm,),
        in_specs=[pl.BlockSpec((D_MODEL, tm), lambda i: (0, i)),
                  pl.BlockSpec(wqt.shape, lambda i: (0, 0)),
                  pl.BlockSpec(sk.shape, lambda i: (0, 0, 0))],
        out_specs=[ospec] * 4,
        out_shape=[out, out32, out32, out],
        compiler_params=_cparams(("parallel",)),
        name="peer_route",
    )(ht, wqt, sk)


GELU_C0 = 0.7978845608028654
GELU_C1 = GELU_C0 * 0.044715


def _gelu(a):
    half = 0.5 * a
    return half + half * jnp.tanh(a * (GELU_C0 + GELU_C1 * (a * a)))


def _expert_kernel(ht_ref, h_ref, u_ref, vt_ref, r2_ref, c1_ref, e1_ref, e2_ref, gain_ref, bias_ref,
                   o_ref, acc_ref, act_ref, *, chunk):
    e = pl.program_id(1)
    te, tm = act_ref.shape
    packed_rows = 16

    @pl.when(e == 0)
    def _():
        acc_ref[...] = jnp.zeros_like(acc_ref)

    def row_bcast(ref, h, r):
        row = jnp.broadcast_to(ref[h, r:r + 1, :], (packed_rows, tm)).astype(act_ref.dtype)
        return pltpu.repeat(row, PEER_NKEYS // packed_rows, axis=0)

    for c in range(te // chunk):
        a = jnp.dot(u_ref[c * chunk:(c + 1) * chunk, :], ht_ref[...], preferred_element_type=F32)
        for rr in range(chunk // PEER_NKEYS):
            r = c * (chunk // PEER_NKEYS) + rr
            g = None
            for h in range(PEER_HEADS):
                gate = jnp.where(r2_ref[h] < row_bcast(c1_ref, h, r), e2_ref[h] * row_bcast(e1_ref, h, r), 0.0)
                g = gate if g is None else g + gate
            a_r = a[rr * PEER_NKEYS:(rr + 1) * PEER_NKEYS, :]
            act_ref[r * PEER_NKEYS:(r + 1) * PEER_NKEYS, :] = _gelu(a_r).astype(act_ref.dtype) * g
    acc_ref[...] += jnp.dot(vt_ref[...], act_ref[...], preferred_element_type=F32)

    @pl.when(e == pl.num_programs(1) - 1)
    def _():
        z = ALPHA * h_ref[...] + acc_ref[...].T
        o_ref[...] = _layer_norm(z, gain_ref[...], bias_ref[...])


def _experts(ht, h1, u, vt, r2, c1, e1, e2, gain, bias, tm=512, te=1024, chunk=256):
    t = h1.shape[0]
    tm = _pick(t, tm)
    hspec = pl.BlockSpec((PEER_HEADS, PEER_NKEYS, tm), lambda i, e: (0, 0, i))
    rspec = pl.BlockSpec((PEER_HEADS, te // PEER_NKEYS, tm), lambda i, e: (0, e, i))
    vec = pl.BlockSpec((1, D_MODEL), lambda i, e: (0, 0))
    return pl.pallas_call(
        functools.partial(_expert_kernel, chunk=chunk),
        grid=(t // tm, PEER_N // te),
        in_specs=[pl.BlockSpec((D_MODEL, tm), lambda i, e: (0, i)),
                  pl.BlockSpec((tm, D_MODEL), lambda i, e: (i, 0)),
                  pl.BlockSpec((te, D_MODEL), lambda i, e: (e, 0)),
                  pl.BlockSpec((D_MODEL, te), lambda i, e: (0, e)),
                  hspec, rspec, rspec, hspec, vec, vec],
        out_specs=pl.BlockSpec((tm, D_MODEL), lambda i, e: (i, 0)),
        out_shape=jax.ShapeDtypeStruct((t, D_MODEL), F32),
        scratch_shapes=[pltpu.VMEM((D_MODEL, tm), F32), pltpu.VMEM((te, tm), ht.dtype)],
        compiler_params=_cparams(("parallel", "arbitrary")),
        name="peer_experts",
    )(ht, h1, u, vt, r2, c1, e1, e2, gain, bias)


def _rope_tables(pos):
    half = MLA_ROPE // 2
    inv_freq = ROPE_BASE ** (-jnp.arange(half, dtype=F32) / half)
    ang = pos.astype(F32)[:, None] * inv_freq[None, :]
    zeros = jnp.zeros((pos.shape[0], MLA_NOPE - MLA_ROPE), F32)
    cos, sin = jnp.cos(ang), jnp.sin(ang)
    return jnp.concatenate([cos, cos, zeros], axis=1), jnp.concatenate([sin, sin, zeros], axis=1)


def _rot_cols(w):
    half = MLA_ROPE // 2
    return jnp.concatenate([-w[..., half:], w[..., :half]], axis=-1)


def kernel(x, meta_tokens, hgrn_lb_logits, w_in, q_norm_gain, kv_norm_gain, w_uq, w_ukv, hgrn_norm_gain,
           w_proj_hgrn, w_proj_mla, w_out, ln1_gain, ln1_bias, peer_query, peer_sub_keys, peer_u, peer_v,
           ln2_gain, ln2_bias):
    batch, seq, d = x.shape
    p_rows = LEAD + seq
    t_real = batch * seq
    l = 0

    h_all = jnp.concatenate([jnp.zeros((batch, N_PAD, d), BF16),
                             jnp.broadcast_to(meta_tokens.astype(BF16)[None], (batch, N_META, d)),
                             x.astype(BF16)], axis=1).reshape(batch * p_rows, d)
    x2 = x.reshape(t_real, d)
    x_bf = x2.astype(BF16)

    lower_bounds = jnp.cumsum(jax.nn.softmax(hgrn_lb_logits.astype(F32), axis=0), axis=0)
    rows = jnp.arange(p_rows)
    cos_all, sin_all = _rope_tables(rows - N_PAD)

    w = w_in[l]
    o_hg = 4 * HG_WIDTH
    o_cq = o_hg + MLA_Q_RANK
    o_ckv = o_cq + MLA_KV_RANK
    o_kr = o_ckv + MLA_ROPE
    w_hg = w[:, :o_hg].astype(BF16)
    w_cq = w[:, o_hg:o_cq].astype(BF16)
    w_kr = w[:, o_ckv:o_kr]
    zpad = jnp.zeros((d, MLA_NOPE - MLA_ROPE), F32)
    w_kv = jnp.concatenate([w[:, o_cq:o_ckv], w_kr, zpad, _rot_cols(w_kr), zpad], axis=1).astype(BF16)
    w_g = w[:, o_kr:].astype(BF16)

    wq3 = w_uq[l].reshape(MLA_Q_RANK, MLA_HEADS, MLA_NOPE + MLA_ROPE)
    zq = jnp.zeros((MLA_Q_RANK, MLA_HEADS, MLA_NOPE - MLA_ROPE), F32)
    wq_rope = wq3[..., MLA_NOPE:]
    wq = jnp.concatenate([wq3[..., :MLA_NOPE], wq_rope, zq, _rot_cols(wq_rope), zq], axis=-1)
    wq = wq.transpose(1, 0, 2).astype(BF16)
    wkv = w_ukv[l].reshape(MLA_KV_RANK, MLA_HEADS, MLA_NOPE + MLA_V).transpose(1, 0, 2).astype(BF16)

    tm_all = _pick(batch * p_rows, 1024)
    tm_real = _pick(t_real, 1024)
    hg = _matmul(h_all, w_hg, tm_all, 1024)
    kvp = _matmul(h_all, w_kv, tm_all, w_kv.shape[1])
    cq = _matmul(x_bf, w_cq, tm_real, MLA_Q_RANK)
    gates = _matmul(x_bf, w_g, tm_real, 1024)

    hg_o = _hgrn(hg, lower_bounds[l][None, :], hgrn_norm_gain[l][None, :], batch, p_rows)

    q = _qprep(cq, q_norm_gain[l][None, :], wq, cos_all[LEAD:], sin_all[LEAD:], batch, seq)
    k, v = _kvprep(kvp, kv_norm_gain[l][None, :], wkv, cos_all, sin_all, batch, p_rows)
    at_o = _flash(q, k, v, batch, seq, p_rows)

    h1, h1t = _mix(x2, hg_o.reshape(t_real, HG_WIDTH), at_o.reshape(t_real, MLA_HEADS * MLA_V), gates,
                   w_proj_hgrn[l].astype(BF16), w_proj_mla[l].astype(BF16), w_out[l].astype(BF16),
                   ln1_gain[l][None, :], ln1_bias[l][None, :])

    wqt = peer_query[l].T.astype(BF16)
    sk = peer_sub_keys[l].reshape(PEER_HEADS * 2, PEER_NKEYS, PEER_HALF).astype(BF16)
    r2, c1, e1, e2 = _route(h1t, wqt, sk)

    out = _experts(h1t, h1, peer_u[l].astype(BF16), peer_v[l].T.astype(BF16), r2, c1, e1, e2,
                   ln2_gain[l][None, :], ln2_bias[l][None, :])
    return out.reshape(batch, seq, d)
```

```python
import functools

import numpy as np
import jax
import jax.numpy as jnp
from jax import lax
from jax.experimental import pallas as pl
from jax.experimental.pallas import tpu as pltpu

F32 = jnp.float32
BF16 = jnp.bfloat16

D_MODEL = 1024
DEPTH = 1
N_META = 16
LEAD = 128
N_PAD = LEAD - N_META

HG_HEADS = 8
HG_D = 128
HG_WIDTH = HG_HEADS * HG_D
HG_SUB = 16

MLA_HEADS = 16
MLA_NOPE = 128
MLA_ROPE = 64
MLA_V = 128
MLA_Q_RANK = 384
MLA_KV_RANK = 256
MLA_QK = 256
MLA_SCALE = (MLA_NOPE + MLA_ROPE) ** -0.5
ROPE_BASE = 10000.0

PEER_HEADS = 8
PEER_NKEYS = 128
PEER_N = PEER_NKEYS * PEER_NKEYS
PEER_HALF = 128
PEER_TOPK = 16

ALPHA = (2 * DEPTH) ** 0.25
EPS = 1e-5
NEG = -1e30

VMEM_LIMIT = 56 * 1024 * 1024


def _cparams(sem):
    return pltpu.CompilerParams(dimension_semantics=sem, vmem_limit_bytes=VMEM_LIMIT)


def _pick(n, pref):
    t = min(n, pref)
    while n % t:
        t -= 128
    return t


def _mm_kernel(x_ref, w_ref, o_ref):
    o_ref[...] = jnp.dot(x_ref[...], w_ref[...], preferred_element_type=F32).astype(o_ref.dtype)


def _matmul(x, w, tm, tn):
    m, k = x.shape
    out_dtype = x.dtype
    n = w.shape[1]
    return pl.pallas_call(
        _mm_kernel,
        grid=(m // tm, n // tn),
        in_specs=[pl.BlockSpec((tm, k), lambda i, j: (i, 0)),
                  pl.BlockSpec((k, tn), lambda i, j: (0, j))],
        out_specs=pl.BlockSpec((tm, tn), lambda i, j: (i, j)),
        out_shape=jax.ShapeDtypeStruct((m, n), out_dtype),
        compiler_params=_cparams(("parallel", "parallel")),
        name="in_proj",
    )(x, w)


def _hgrn_chunk(q, k, lf, iv, st_ref, tri, masks, chunk):
    width = q.shape[1]
    hi = lf.astype(BF16)
    r1 = lf - hi.astype(F32)
    mid = r1.astype(BF16)
    lo = (r1 - mid.astype(F32)).astype(BF16)
    g = (jnp.dot(tri, hi, preferred_element_type=F32)
         + jnp.dot(tri, mid, preferred_element_type=F32)
         + jnp.dot(tri, lo, preferred_element_type=F32))

    def bcast_row(r, n):
        return jnp.broadcast_to(g[r:r + 1, :], (n, width))

    def prev_end(s):
        parts = [jnp.zeros((s, width), F32)] + [bcast_row(b * s - 1, s) for b in range(1, chunk // s)]
        return parts[0] if len(parts) == 1 else jnp.concatenate(parts, axis=0)

    def own_end(s):
        parts = [bcast_row((b + 1) * s - 1, s) for b in range(chunk // s)]
        return parts[0] if len(parts) == 1 else jnp.concatenate(parts, axis=0)

    q_c = (q * jnp.exp(g)).astype(BF16)
    k_c = (k * jnp.exp(own_end(chunk) - g)).astype(BF16)
    dec = jnp.exp(g[chunk - 1:chunk, :])
    p16 = prev_end(HG_SUB)
    q_lv = [(q * jnp.exp(g - p16)).astype(BF16)]
    k_lv = [(k * jnp.exp(p16 - g)).astype(BF16)]
    s = HG_SUB
    while s < chunk:
        q_lv.append(q_lv[0] if s == HG_SUB else (q * jnp.exp(g - prev_end(s))).astype(BF16))
        k_lv.append((k * jnp.exp(own_end(s) - g)).astype(BF16))
        s *= 2

    nt = (((1,), (1,)), ((), ()))
    tn = (((0,), (0,)), ((), ()))
    heads = range(HG_HEADS)
    hs = [slice(h * HG_D, (h + 1) * HG_D) for h in heads]
    st = [st_ref[h] for h in heads]
    o_inter = [lax.dot_general(q_c[:, hs[h]], st[h].astype(BF16), nt, preferred_element_type=F32) for h in heads]
    upd = [lax.dot_general(iv[:, hs[h]], k_c[:, hs[h]], tn, preferred_element_type=F32) for h in heads]
    scores = [[lax.dot_general(ql[:, hs[h]], kl[:, hs[h]], nt, preferred_element_type=F32) for h in heads]
              for ql, kl in zip(q_lv, k_lv)]
    outs, st_new = [], []
    for h in heads:
        a = jnp.where(masks[0], scores[0][h], 0.0)
        for lvl in range(1, len(masks)):
            a = jnp.where(masks[lvl], scores[lvl][h], a)
        outs.append(o_inter[h] + jnp.dot(a.astype(BF16), iv[:, hs[h]], preferred_element_type=F32))
        st_new.append(st[h] * dec[:, hs[h]] + upd[h])
    return outs, st_new


def _hgrn_kernel(q_ref, f_ref, i_ref, g_ref, lb_ref, gain_ref, tri_ref, o_ref, st_ref, *, chunk, rows):
    c = pl.program_id(1)

    @pl.when(c == 0)
    def _():
        st_ref[...] = jnp.zeros_like(st_ref)

    ri = lax.broadcasted_iota(jnp.int32, (chunk, chunk), 0)
    ci = lax.broadcasted_iota(jnp.int32, (chunk, chunk), 1)
    blk = lambda v, s: lax.shift_right_logical(v, s.bit_length() - 1)
    masks = [(blk(ri, HG_SUB) == blk(ci, HG_SUB)) & (ci <= ri)]
    s = HG_SUB
    while s < chunk:
        masks.append((blk(ri, 2 * s) == blk(ci, 2 * s)) & ((blk(ri, s) & 1) == 1) & ((blk(ci, s) & 1) == 0))
        s *= 2
    tri = tri_ref[...]
    lb = lb_ref[...]

    for n in range(rows // chunk):
        rs = slice(n * chunk, (n + 1) * chunk)
        seq_row = c * rows + n * chunk + lax.broadcasted_iota(jnp.int32, (chunk, HG_WIDTH), 0)
        valid = seq_row >= N_PAD
        f = lb + (1.0 - lb) * jax.nn.sigmoid(f_ref[rs, :].astype(F32))
        lf = jnp.where(valid, jnp.log(f), 0.0)
        k = jnp.where(valid, 1.0 - f, 0.0)
        outs, st_new = _hgrn_chunk(q_ref[rs, :].astype(F32), k, lf, i_ref[rs, :], st_ref, tri, masks, chunk)
        gate = g_ref[rs, :].astype(F32)
        gate = gate * jax.nn.sigmoid(gate)
        for h in range(HG_HEADS):
            hs = slice(h * HG_D, (h + 1) * HG_D)
            st_ref[h] = st_new[h]
            o = outs[h]
            ms = jnp.mean(o * o, axis=-1, keepdims=True)
            o = o * lax.rsqrt(ms + EPS) * gain_ref[:, hs]
            o_ref[0, rs, hs] = (o * gate[:, hs]).astype(o_ref.dtype)


def _hgrn(hg, lb, gain, batch, p_rows, chunk=64, rows=128):
    nc = p_rows // rows
    lead_blocks = LEAD // rows
    tri = jnp.asarray(np.tril(np.ones((chunk, chunk), np.float32)), BF16)
    col = lambda j: pl.BlockSpec((rows, HG_WIDTH), lambda b, c: (b * nc + c, j))
    vec = pl.BlockSpec((1, HG_WIDTH), lambda b, c: (0, 0))
    return pl.pallas_call(
        functools.partial(_hgrn_kernel, chunk=chunk, rows=rows),
        grid=(batch, nc),
        in_specs=[col(0), col(1), col(2), col(3), vec, vec,
                  pl.BlockSpec((chunk, chunk), lambda b, c: (0, 0))],
        out_specs=pl.BlockSpec((1, rows, HG_WIDTH), lambda b, c: (b, jnp.maximum(c - lead_blocks, 0), 0)),
        out_shape=jax.ShapeDtypeStruct((batch, p_rows - LEAD, HG_WIDTH), BF16),
        scratch_shapes=[pltpu.VMEM((HG_HEADS, HG_D, HG_D), F32)],
        compiler_params=_cparams(("parallel", "arbitrary")),
        name="hgrn2",
    )(hg, hg, hg, hg, lb, gain, tri)


def _rms(x, gain):
    ms = jnp.mean(x * x, axis=-1, keepdims=True)
    return x * lax.rsqrt(ms + EPS) * gain


def _qprep_kernel(cq_ref, gain_ref, w_ref, cos_ref, sin_ref, o_ref):
    xn = _rms(cq_ref[...].astype(F32), gain_ref[...]).astype(BF16)
    cos = cos_ref[...]
    sin = sin_ref[...]
    for h in range(MLA_HEADS):
        y = jnp.dot(xn, w_ref[h], preferred_element_type=F32)
        o_ref[0, h, :, :MLA_NOPE] = (y[:, :MLA_NOPE] * MLA_SCALE).astype(o_ref.dtype)
        roped = y[:, MLA_NOPE:2 * MLA_NOPE] * cos + y[:, 2 * MLA_NOPE:] * sin
        o_ref[0, h, :, MLA_NOPE:] = (roped * MLA_SCALE).astype(o_ref.dtype)


def _qprep(cq, gain, wq, cos, sin, batch, seq):
    tm = _pick(seq, 512)
    nb = seq // tm
    return pl.pallas_call(
        _qprep_kernel,
        grid=(batch, nb),
        in_specs=[pl.BlockSpec((tm, MLA_Q_RANK), lambda b, i: (b * nb + i, 0)),
                  pl.BlockSpec((1, MLA_Q_RANK), lambda b, i: (0, 0)),
                  pl.BlockSpec((MLA_HEADS, MLA_Q_RANK, 3 * MLA_NOPE), lambda b, i: (0, 0, 0)),
                  pl.BlockSpec((tm, MLA_NOPE), lambda b, i: (i, 0)),
                  pl.BlockSpec((tm, MLA_NOPE), lambda b, i: (i, 0))],
        out_specs=pl.BlockSpec((1, MLA_HEADS, tm, MLA_QK), lambda b, i: (b, 0, i, 0)),
        out_shape=jax.ShapeDtypeStruct((batch, MLA_HEADS, seq, MLA_QK), BF16),
        compiler_params=_cparams(("parallel", "parallel")),
        name="mla_q",
    )(cq, gain, wq, cos, sin)


def _kvprep_kernel(kv_ref, gain_ref, w_ref, cos_ref, sin_ref, k_ref, v_ref):
    x = kv_ref[...]
    xn = _rms(x[:, :MLA_KV_RANK].astype(F32), gain_ref[...]).astype(BF16)
    kr = (x[:, MLA_KV_RANK:MLA_KV_RANK + MLA_NOPE].astype(F32) * cos_ref[...]
          + x[:, MLA_KV_RANK + MLA_NOPE:].astype(F32) * sin_ref[...]).astype(k_ref.dtype)
    for h in range(MLA_HEADS):
        y = jnp.dot(xn, w_ref[h], preferred_element_type=F32)
        k_ref[0, h, :, :MLA_NOPE] = y[:, :MLA_NOPE].astype(k_ref.dtype)
        k_ref[0, h, :, MLA_NOPE:] = kr
        v_ref[0, h] = y[:, MLA_NOPE:].astype(v_ref.dtype)


def _kvprep(kvp, gain, wkv, cos, sin, batch, p_rows):
    tm = p_rows // 2
    return pl.pallas_call(
        _kvprep_kernel,
        grid=(batch, 2),
        in_specs=[pl.BlockSpec((tm, MLA_KV_RANK + 2 * MLA_NOPE), lambda b, i: (b * 2 + i, 0)),
                  pl.BlockSpec((1, MLA_KV_RANK), lambda b, i: (0, 0)),
                  pl.BlockSpec((MLA_HEADS, MLA_KV_RANK, MLA_NOPE + MLA_V), lambda b, i: (0, 0, 0)),
                  pl.BlockSpec((tm, MLA_NOPE), lambda b, i: (i, 0)),
                  pl.BlockSpec((tm, MLA_NOPE), lambda b, i: (i, 0))],
        out_specs=[pl.BlockSpec((1, MLA_HEADS, tm, MLA_QK), lambda b, i: (b, 0, i, 0)),
                   pl.BlockSpec((1, MLA_HEADS, tm, MLA_V), lambda b, i: (b, 0, i, 0))],
        out_shape=[jax.ShapeDtypeStruct((batch, MLA_HEADS, p_rows, MLA_QK), BF16),
                   jax.ShapeDtypeStruct((batch, MLA_HEADS, p_rows, MLA_V), BF16)],
        compiler_params=_cparams(("parallel", "parallel")),
        name="mla_kv",
    )(kvp, gain, wkv, cos, sin)


def _flash_segments(seq, tq, tk):
    segs = []
    for qi in range(seq // tq):
        pos = 0
        while pos < qi * tq:
            n = min(tk, qi * tq - pos)
            segs.append((qi, False, pos, n))
            pos += n
        segs.append((qi, True, qi * tq, tq))
    return segs


def _flash_kernel(q_ref, k_ref, v_ref, o_ref, *, tq, tk):
    seq = q_ref.shape[2]
    nt = (((1,), (1,)), ((), ()))
    causal = (lax.broadcasted_iota(jnp.int32, (tq, tq), 1) <= lax.broadcasted_iota(jnp.int32, (tq, tq), 0))
    lead_mask = lax.broadcasted_iota(jnp.int32, (tq, LEAD), 1) >= N_PAD

    def scores(seg):
        qi, diag, k0, n = seg
        q = q_ref[0, 0, qi * tq:(qi + 1) * tq, :]
        s = lax.dot_general(q, k_ref[0, 0, LEAD + k0:LEAD + k0 + n, :], nt, preferred_element_type=F32)
        if diag:
            s_lead = lax.dot_general(q, k_ref[0, 0, 0:LEAD, :], nt, preferred_element_type=F32)
            s = jnp.concatenate([jnp.where(causal, s, NEG), jnp.where(lead_mask, s_lead, NEG)], axis=1)
        return s

    segs = _flash_segments(seq, tq, tk)
    s_cur = scores(segs[0])
    m = l = acc = None
    for i, (qi, diag, k0, n) in enumerate(segs):
        s_next = scores(segs[i + 1]) if i + 1 < len(segs) else None
        first = k0 == 0
        m_new = jnp.max(s_cur, axis=-1, keepdims=True)
        if not first:
            m_new = jnp.maximum(m, m_new)
        p = jnp.exp(s_cur - m_new)
        p_sum = jnp.sum(p, axis=-1, keepdims=True)
        p = p.astype(BF16)
        if diag:
            pv = (jnp.dot(p[:, :tq], v_ref[0, 0, LEAD + k0:LEAD + k0 + n, :], preferred_element_type=F32)
                  + jnp.dot(p[:, tq:], v_ref[0, 0, 0:LEAD, :], preferred_element_type=F32))
        else:
            pv = jnp.dot(p, v_ref[0, 0, LEAD + k0:LEAD + k0 + n, :], preferred_element_type=F32)
        if first:
            l, acc = p_sum, pv
        else:
            corr = jnp.exp(m - m_new)
            l = corr * l + p_sum
            acc = corr * acc + pv
        m = m_new
        if diag:
            o_ref[0, qi * tq:(qi + 1) * tq, :] = (acc / l).astype(o_ref.dtype)
        s_cur = s_next


def _flash(q, k, v, batch, seq, p_rows, tq=256, tk=512):
    tq = min(tq, seq)
    return pl.pallas_call(
        functools.partial(_flash_kernel, tq=tq, tk=tk),
        grid=(batch, MLA_HEADS),
        in_specs=[pl.BlockSpec((1, 1, seq, MLA_QK), lambda b, h: (b, h, 0, 0)),
                  pl.BlockSpec((1, 1, p_rows, MLA_QK), lambda b, h: (b, h, 0, 0)),
                  pl.BlockSpec((1, 1, p_rows, MLA_V), lambda b, h: (b, h, 0, 0))],
        out_specs=pl.BlockSpec((1, seq, MLA_V), lambda b, h: (b, 0, h)),
        out_shape=jax.ShapeDtypeStruct((batch, seq, MLA_HEADS * MLA_V), BF16),
        compiler_params=_cparams(("parallel", "parallel")),
        name="mla_flash",
    )(q, k, v)


def _layer_norm(z, gain, bias):
    mu = jnp.mean(z, axis=-1, keepdims=True)
    zc = z - mu
    var = jnp.mean(zc * zc, axis=-1, keepdims=True)
    return zc * lax.rsqrt(var + EPS) * gain + bias


def _mix_kernel(x_ref, hg_ref, at_ref, ga_ref, gb_ref, wph_ref, wpm_ref, wo_ref, gain_ref, bias_ref,
                h_ref, ht_ref):
    ya = jnp.dot(hg_ref[...], wph_ref[...], preferred_element_type=F32)
    yb = jnp.dot(at_ref[...], wpm_ref[...], preferred_element_type=F32)
    mix = (jax.nn.sigmoid(ga_ref[...].astype(F32)) * ya + jax.nn.sigmoid(gb_ref[...].astype(F32)) * yb)
    mixed = jnp.dot(mix.astype(BF16), wo_ref[...], preferred_element_type=F32)
    h = _layer_norm(ALPHA * x_ref[...] + mixed, gain_ref[...], bias_ref[...])
    h_ref[...] = h
    ht_ref[...] = h.T.astype(ht_ref.dtype)


def _mix(x2, hg_o, at_o, gates, wph, wpm, wo, gain, bias):
    t = x2.shape[0]
    tm = _pick(t, 512)
    full = lambda a: pl.BlockSpec(a.shape, lambda i: (0,) * a.ndim)
    return pl.pallas_call(
        _mix_kernel,
        grid=(t // tm,),
        in_specs=[pl.BlockSpec((tm, D_MODEL), lambda i: (i, 0)),
                  pl.BlockSpec((tm, HG_WIDTH), lambda i: (i, 0)),
                  pl.BlockSpec((tm, MLA_HEADS * MLA_V), lambda i: (i, 0)),
                  pl.BlockSpec((tm, D_MODEL), lambda i: (i, 0)),
                  pl.BlockSpec((tm, D_MODEL), lambda i: (i, 1)),
                  full(wph), full(wpm), full(wo), full(gain), full(bias)],
        out_specs=[pl.BlockSpec((tm, D_MODEL), lambda i: (i, 0)),
                   pl.BlockSpec((D_MODEL, tm), lambda i: (0, i))],
        out_shape=[jax.ShapeDtypeStruct((t, D_MODEL), F32),
                   jax.ShapeDtypeStruct((D_MODEL, t), BF16)],
        compiler_params=_cparams(("parallel",)),
        name="mix_ln1",
    )(x2, hg_o, at_o, gates, gates, wph, wpm, wo, gain, bias)


N_TOP = PEER_TOPK + 1
TOP_ROWS = 24
NO_RANK = 255.0
PAIR_LIMIT = [N_TOP // (a + 1) for a in range(N_TOP)]
N_WIDE = sum(1 for n in PAIR_LIMIT if n > 1)
ROUTE_GROUP = 2


def _top_rows(xs, n, ranked):
    xs = list(xs)
    vals = [[] for _ in xs]
    ranks = [jnp.full(x.shape, NO_RANK, F32) if rk else None for x, rk in zip(xs, ranked)]
    for r in range(n):
        for i, x in enumerate(xs):
            m = jnp.max(x, axis=0, keepdims=True)
            vals[i].append(m)
            hit = x == m
            if ranked[i]:
                ranks[i] = jnp.where(hit, float(r), ranks[i])
            xs[i] = jnp.where(hit, -jnp.inf, x)
    return vals, ranks


def _count_above(rows, thr):
    count = jnp.zeros_like(thr)
    passed = []
    step = PEER_TOPK // 2
    while step >= 1:
        cands = [rows[base + step - 1] for base in range(0, PEER_TOPK, 2 * step)]
        for m in reversed(passed):
            cands = [jnp.where(m, cands[2 * i + 1], cands[2 * i]) for i in range(len(cands) // 2)]
        m = cands[0] > thr
        count = jnp.where(m, count + float(step), count)
        passed.append(m)
        step //= 2
    return jnp.where(rows[PEER_TOPK - 1] > thr, float(PEER_TOPK), count)


def _stack_rows(rows, tokens):
    ri = lax.broadcasted_iota(jnp.int32, (TOP_ROWS, tokens), 0)
    out = jnp.full((TOP_ROWS, tokens), -jnp.inf, F32)
    for r, v in enumerate(rows):
        out = jnp.where(ri == r, v, out)
    return out


def _route_kernel(ht_ref, wq_ref, sk_ref, r2_ref, c1_ref, e1_ref, e2_ref):
    tokens = ht_ref.shape[1]
    ri8 = lax.broadcasted_iota(jnp.int32, (N_WIDE, tokens), 0)
    qp = jnp.dot(wq_ref[...], ht_ref[...], preferred_element_type=F32).astype(BF16)
    for h0 in range(0, PEER_HEADS, ROUTE_GROUP):
        heads = range(h0, h0 + ROUTE_GROUP)
        s = [jnp.dot(sk_ref[hp], qp[hp * PEER_HALF:(hp + 1) * PEER_HALF], preferred_element_type=F32)
             for hp in range(2 * h0, 2 * (h0 + ROUTE_GROUP))]
        tops, ranks = _top_rows(s, N_TOP, [False, True] * ROUTE_GROUP)
        cand = []
        for g in range(ROUTE_GROUP):
            t1, t2 = tops[2 * g], tops[2 * g + 1]
            t1s = _stack_rows(t1, tokens)
            t2s = _stack_rows(t2, tokens)
            parts = [jnp.where(ri8 < min(PAIR_LIMIT[a], N_WIDE), t1[a] + t2s[:N_WIDE], -jnp.inf)
                     for a in range(N_WIDE)]
            parts.append(t1[0] + t2s[N_WIDE:])
            parts.append(t1s[N_WIDE:] + t2[0])
            cand.append(jnp.concatenate(parts, axis=0))
        bests, _ = _top_rows(cand, N_TOP, [False] * ROUTE_GROUP)
        for g, h in enumerate(heads):
            s1, s2 = s[2 * g], s[2 * g + 1]
            t1, t2, best = tops[2 * g], tops[2 * g + 1], bests[g]
            m = best[0]
            z = jnp.zeros_like(m)
            for r in range(PEER_TOPK):
                z = z + jnp.exp(best[r] - m)
            thr = 0.5 * (best[PEER_TOPK - 1] + best[PEER_TOPK]) - s1
            c1 = _count_above(t2, thr)
            r2_ref[h] = ranks[2 * g + 1].astype(r2_ref.dtype)
            c1_ref[h] = c1
            e1_ref[h] = jnp.exp(s1 - t1[0] - jnp.log(z))
            e2_ref[h] = jnp.exp(s2 - t2[0]).astype(e2_ref.dtype)


def _route(ht, wqt, sk):
    t = ht.shape[1]
    tm = _pick(t, 256)
    out = jax.ShapeDtypeStruct((PEER_HEADS, PEER_NKEYS, t), ht.dtype)
    out32 = jax.ShapeDtypeStruct((PEER_HEADS, PEER_NKEYS, t), F32)
    ospec = pl.BlockSpec((PEER_HEADS, PEER_NKEYS, tm), lambda i: (0, 0, i))
    return pl.pallas_call(
        _route_kernel,
        grid=(t // tm,),
        in_specs=[pl.BlockSpec((D_MODEL, tm), lambda i: (0, i)),
                  pl.BlockSpec(wqt.shape, lambda i: (0, 0)),
                  pl.BlockSpec(sk.shape, lambda i: (0, 0, 0))],
        out_specs=[ospec] * 4,
        out_shape=[out, out32, out32, out],
        compiler_params=_cparams(("parallel",)),
        name="peer_route",
    )(ht, wqt, sk)


GELU_C0 = 0.7978845608028654
GELU_C1 = GELU_C0 * 0.044715
LOG2_E = 1.4426950408889634


def _gelu(a):
    t = jnp.exp2(a * ((-2.0 * LOG2_E * GELU_C0) + (-2.0 * LOG2_E * GELU_C1) * (a * a)))
    return a / (1.0 + t)


def _expert_kernel(ht_ref, h_ref, u_ref, vt_ref, r2_ref, c1_ref, e1_ref, e2_ref, gain_ref, bias_ref,
                   o_ref, acc_ref, act_ref, *, chunk):
    e = pl.program_id(1)
    te, tm = act_ref.shape
    packed_rows = 16

    @pl.when(e == 0)
    def _():
        acc_ref[...] = jnp.zeros_like(acc_ref)

    def row_bcast(ref, h, r):
        row = jnp.broadcast_to(ref[h, r:r + 1, :], (packed_rows, tm)).astype(act_ref.dtype)
        return pltpu.repeat(row, PEER_NKEYS // packed_rows, axis=0)

    for c in range(te // chunk):
        a = jnp.dot(u_ref[c * chunk:(c + 1) * chunk, :], ht_ref[...], preferred_element_type=F32)
        for rr in range(chunk // PEER_NKEYS):
            r = c * (chunk // PEER_NKEYS) + rr
            g = None
            for h in range(PEER_HEADS):
                gate = jnp.where(r2_ref[h] < row_bcast(c1_ref, h, r), e2_ref[h] * row_bcast(e1_ref, h, r), 0.0)
                g = gate if g is None else g + gate
            a_r = a[rr * PEER_NKEYS:(rr + 1) * PEER_NKEYS, :]
            act_ref[r * PEER_NKEYS:(r + 1) * PEER_NKEYS, :] = _gelu(a_r.astype(act_ref.dtype)) * g
    acc_ref[...] += jnp.dot(vt_ref[...], act_ref[...], preferred_element_type=F32)

    @pl.when(e == pl.num_programs(1) - 1)
    def _():
        z = ALPHA * h_ref[...] + acc_ref[...].T
        o_ref[...] = _layer_norm(z, gain_ref[...], bias_ref[...])


def _experts(ht, h1, u, vt, r2, c1, e1, e2, gain, bias, tm=512, te=2048, chunk=512):
    t = h1.shape[0]
    tm = _pick(t, tm)
    hspec = pl.BlockSpec((PEER_HEADS, PEER_NKEYS, tm), lambda i, e: (0, 0, i))
    rspec = pl.BlockSpec((PEER_HEADS, te // PEER_NKEYS, tm), lambda i, e: (0, e, i))
    vec = pl.BlockSpec((1, D_MODEL), lambda i, e: (0, 0))
    return pl.pallas_call(
        functools.partial(_expert_kernel, chunk=chunk),
        grid=(t // tm, PEER_N // te),
        in_specs=[pl.BlockSpec((D_MODEL, tm), lambda i, e: (0, i)),
                  pl.BlockSpec((tm, D_MODEL), lambda i, e: (i, 0)),
                  pl.BlockSpec((te, D_MODEL), lambda i, e: (e, 0)),
                  pl.BlockSpec((D_MODEL, te), lambda i, e: (0, e)),
                  hspec, rspec, rspec, hspec, vec, vec],
        out_specs=pl.BlockSpec((tm, D_MODEL), lambda i, e: (i, 0)),
        out_shape=jax.ShapeDtypeStruct((t, D_MODEL), F32),
        scratch_shapes=[pltpu.VMEM((D_MODEL, tm), F32), pltpu.VMEM((te, tm), ht.dtype)],
        compiler_params=_cparams(("parallel", "arbitrary")),
        name="peer_experts",
    )(ht, h1, u, vt, r2, c1, e1, e2, gain, bias)


def _rope_tables(pos):
    half = MLA_ROPE // 2
    inv_freq = ROPE_BASE ** (-jnp.arange(half, dtype=F32) / half)
    ang = pos.astype(F32)[:, None] * inv_freq[None, :]
    zeros = jnp.zeros((pos.shape[0], MLA_NOPE - MLA_ROPE), F32)
    cos, sin = jnp.cos(ang), jnp.sin(ang)
    return jnp.concatenate([cos, cos, zeros], axis=1), jnp.concatenate([sin, sin, zeros], axis=1)


def _rot_cols(w):
    half = MLA_ROPE // 2
    return jnp.concatenate([-w[..., half:], w[..., :half]], axis=-1)


def kernel(x, meta_tokens, hgrn_lb_logits, w_in, q_norm_gain, kv_norm_gain, w_uq, w_ukv, hgrn_norm_gain,
           w_proj_hgrn, w_proj_mla, w_out, ln1_gain, ln1_bias, peer_query, peer_sub_keys, peer_u, peer_v,
           ln2_gain, ln2_bias):
    batch, seq, d = x.shape
    p_rows = LEAD + seq
    t_real = batch * seq
    l = 0

    h_all = jnp.concatenate([jnp.zeros((batch, N_PAD, d), BF16),
                             jnp.broadcast_to(meta_tokens.astype(BF16)[None], (batch, N_META, d)),
                             x.astype(BF16)], axis=1).reshape(batch * p_rows, d)
    x2 = x.reshape(t_real, d)
    x_bf = x2.astype(BF16)

    lower_bounds = jnp.cumsum(jax.nn.softmax(hgrn_lb_logits.astype(F32), axis=0), axis=0)
    rows = jnp.arange(p_rows)
    cos_all, sin_all = _rope_tables(rows - N_PAD)

    w = w_in[l]
    o_hg = 4 * HG_WIDTH
    o_cq = o_hg + MLA_Q_RANK
    o_ckv = o_cq + MLA_KV_RANK
    o_kr = o_ckv + MLA_ROPE
    w_hg = w[:, :o_hg].astype(BF16)
    w_cq = w[:, o_hg:o_cq].astype(BF16)
    w_kr = w[:, o_ckv:o_kr]
    zpad = jnp.zeros((d, MLA_NOPE - MLA_ROPE), F32)
    w_kv = jnp.concatenate([w[:, o_cq:o_ckv], w_kr, zpad, _rot_cols(w_kr), zpad], axis=1).astype(BF16)
    w_g = w[:, o_kr:].astype(BF16)

    wq3 = w_uq[l].reshape(MLA_Q_RANK, MLA_HEADS, MLA_NOPE + MLA_ROPE)
    zq = jnp.zeros((MLA_Q_RANK, MLA_HEADS, MLA_NOPE - MLA_ROPE), F32)
    wq_rope = wq3[..., MLA_NOPE:]
    wq = jnp.concatenate([wq3[..., :MLA_NOPE], wq_rope, zq, _rot_cols(wq_rope), zq], axis=-1)
    wq = wq.transpose(1, 0, 2).astype(BF16)
    wkv = w_ukv[l].reshape(MLA_KV_RANK, MLA_HEADS, MLA_NOPE + MLA_V).transpose(1, 0, 2).astype(BF16)

    tm_all = _pick(batch * p_rows, 1024)
    tm_real = _pick(t_real, 1024)
    hg = _matmul(h_all, w_hg, tm_all, 1024)
    kvp = _matmul(h_all, w_kv, tm_all, w_kv.shape[1])
    cq = _matmul(x_bf, w_cq, tm_real, MLA_Q_RANK)
    gates = _matmul(x_bf, w_g, tm_real, 1024)

    hg_o = _hgrn(hg, lower_bounds[l][None, :], hgrn_norm_gain[l][None, :], batch, p_rows)

    q = _qprep(cq, q_norm_gain[l][None, :], wq, cos_all[LEAD:], sin_all[LEAD:], batch, seq)
    k, v = _kvprep(kvp, kv_norm_gain[l][None, :], wkv, cos_all, sin_all, batch, p_rows)
    at_o = _flash(q, k, v, batch, seq, p_rows)

    h1, h1t = _mix(x2, hg_o.reshape(t_real, HG_WIDTH), at_o.reshape(t_real, MLA_HEADS * MLA_V), gates,
                   w_proj_hgrn[l].astype(BF16), w_proj_mla[l].astype(BF16), w_out[l].astype(BF16),
                   ln1_gain[l][None, :], ln1_bias[l][None, :])

    wqt = peer_query[l].T.astype(BF16)
    sk = peer_sub_keys[l].reshape(PEER_HEADS * 2, PEER_NKEYS, PEER_HALF).astype(BF16)
    r2, c1, e1, e2 = _route(h1t, wqt, sk)

    out = _experts(h1t, h1, peer_u[l].astype(BF16), peer_v[l].T.astype(BF16), r2, c1, e1, e2,
                   ln2_gain[l][None, :], ln2_bias[l][None, :])
    return out.reshape(batch, seq, d)
```

```python
import functools

import numpy as np
import jax
import jax.numpy as jnp
from jax import lax
from jax.experimental import pallas as pl
from jax.experimental.pallas import tpu as pltpu

F32 = jnp.float32
BF16 = jnp.bfloat16

D_MODEL = 1024
DEPTH = 1
N_META = 16
LEAD = 128
N_PAD = LEAD - N_META

HG_HEADS = 8
HG_D = 128
HG_WIDTH = HG_HEADS * HG_D
HG_SUB = 16

MLA_HEADS = 16
MLA_NOPE = 128
MLA_ROPE = 64
MLA_V = 128
MLA_Q_RANK = 384
MLA_KV_RANK = 256
MLA_QK = 256
MLA_SCALE = (MLA_NOPE + MLA_ROPE) ** -0.5
ROPE_BASE = 10000.0

PEER_HEADS = 8
PEER_NKEYS = 128
PEER_N = PEER_NKEYS * PEER_NKEYS
PEER_HALF = 128
PEER_TOPK = 16

ALPHA = (2 * DEPTH) ** 0.25
EPS = 1e-5
NEG = -1e30

VMEM_LIMIT = 56 * 1024 * 1024


def _cparams(sem):
    return pltpu.CompilerParams(dimension_semantics=sem, vmem_limit_bytes=VMEM_LIMIT)


def _pick(n, pref):
    t = min(n, pref)
    while n % t:
        t -= 128
    return t


def _mm_kernel(x_ref, w_ref, o_ref):
    o_ref[...] = jnp.dot(x_ref[...], w_ref[...], preferred_element_type=F32).astype(o_ref.dtype)


def _matmul(x, w, tm, tn):
    m, k = x.shape
    out_dtype = x.dtype
    n = w.shape[1]
    return pl.pallas_call(
        _mm_kernel,
        grid=(m // tm, n // tn),
        in_specs=[pl.BlockSpec((tm, k), lambda i, j: (i, 0)),
                  pl.BlockSpec((k, tn), lambda i, j: (0, j))],
        out_specs=pl.BlockSpec((tm, tn), lambda i, j: (i, j)),
        out_shape=jax.ShapeDtypeStruct((m, n), out_dtype),
        compiler_params=_cparams(("parallel", "parallel")),
        name="in_proj",
    )(x, w)


def _hgrn_chunk(q, k, lf, iv, st_ref, tri, masks, chunk):
    width = q.shape[1]
    hi = lf.astype(BF16)
    r1 = lf - hi.astype(F32)
    mid = r1.astype(BF16)
    lo = (r1 - mid.astype(F32)).astype(BF16)
    g = (jnp.dot(tri, hi, preferred_element_type=F32)
         + jnp.dot(tri, mid, preferred_element_type=F32)
         + jnp.dot(tri, lo, preferred_element_type=F32))

    def bcast_row(r, n):
        return jnp.broadcast_to(g[r:r + 1, :], (n, width))

    def prev_end(s):
        parts = [jnp.zeros((s, width), F32)] + [bcast_row(b * s - 1, s) for b in range(1, chunk // s)]
        return parts[0] if len(parts) == 1 else jnp.concatenate(parts, axis=0)

    def own_end(s):
        parts = [bcast_row((b + 1) * s - 1, s) for b in range(chunk // s)]
        return parts[0] if len(parts) == 1 else jnp.concatenate(parts, axis=0)

    q_c = (q * jnp.exp(g)).astype(BF16)
    k_c = (k * jnp.exp(own_end(chunk) - g)).astype(BF16)
    dec = jnp.exp(g[chunk - 1:chunk, :])
    p16 = prev_end(HG_SUB)
    q_lv = [(q * jnp.exp(g - p16)).astype(BF16)]
    k_lv = [(k * jnp.exp(p16 - g)).astype(BF16)]
    s = HG_SUB
    while s < chunk:
        q_lv.append(q_lv[0] if s == HG_SUB else (q * jnp.exp(g - prev_end(s))).astype(BF16))
        k_lv.append((k * jnp.exp(own_end(s) - g)).astype(BF16))
        s *= 2

    nt = (((1,), (1,)), ((), ()))
    tn = (((0,), (0,)), ((), ()))
    heads = range(HG_HEADS)
    hs = [slice(h * HG_D, (h + 1) * HG_D) for h in heads]
    st = [st_ref[h] for h in heads]
    o_inter = [lax.dot_general(q_c[:, hs[h]], st[h].astype(BF16), nt, preferred_element_type=F32) for h in heads]
    upd = [lax.dot_general(iv[:, hs[h]], k_c[:, hs[h]], tn, preferred_element_type=F32) for h in heads]
    scores = [[lax.dot_general(ql[:, hs[h]], kl[:, hs[h]], nt, preferred_element_type=F32) for h in heads]
              for ql, kl in zip(q_lv, k_lv)]
    outs, st_new = [], []
    for h in heads:
        a = jnp.where(masks[0], scores[0][h], 0.0)
        for lvl in range(1, len(masks)):
            a = jnp.where(masks[lvl], scores[lvl][h], a)
        outs.append(o_inter[h] + jnp.dot(a.astype(BF16), iv[:, hs[h]], preferred_element_type=F32))
        st_new.append(st[h] * dec[:, hs[h]] + upd[h])
    return outs, st_new


def _hgrn_kernel(q_ref, f_ref, i_ref, g_ref, lb_ref, gain_ref, tri_ref, o_ref, st_ref, *, chunk, rows):
    c = pl.program_id(1)

    @pl.when(c == 0)
    def _():
        st_ref[...] = jnp.zeros_like(st_ref)

    ri = lax.broadcasted_iota(jnp.int32, (chunk, chunk), 0)
    ci = lax.broadcasted_iota(jnp.int32, (chunk, chunk), 1)
    blk = lambda v, s: lax.shift_right_logical(v, s.bit_length() - 1)
    masks = [(blk(ri, HG_SUB) == blk(ci, HG_SUB)) & (ci <= ri)]
    s = HG_SUB
    while s < chunk:
        masks.append((blk(ri, 2 * s) == blk(ci, 2 * s)) & ((blk(ri, s) & 1) == 1) & ((blk(ci, s) & 1) == 0))
        s *= 2
    tri = tri_ref[...]
    lb = lb_ref[...]

    for n in range(rows // chunk):
        rs = slice(n * chunk, (n + 1) * chunk)
        seq_row = c * rows + n * chunk + lax.broadcasted_iota(jnp.int32, (chunk, HG_WIDTH), 0)
        valid = seq_row >= N_PAD
        f = lb + (1.0 - lb) * jax.nn.sigmoid(f_ref[rs, :].astype(F32))
        lf = jnp.where(valid, jnp.log(f), 0.0)
        k = jnp.where(valid, 1.0 - f, 0.0)
        outs, st_new = _hgrn_chunk(q_ref[rs, :].astype(F32), k, lf, i_ref[rs, :], st_ref, tri, masks, chunk)
        gate = g_ref[rs, :].astype(F32)
        gate = gate * jax.nn.sigmoid(gate)
        for h in range(HG_HEADS):
            hs = slice(h * HG_D, (h + 1) * HG_D)
            st_ref[h] = st_new[h]
            o = outs[h]
            ms = jnp.mean(o * o, axis=-1, keepdims=True)
            o = o * lax.rsqrt(ms + EPS) * gain_ref[:, hs]
            o_ref[0, rs, hs] = (o * gate[:, hs]).astype(o_ref.dtype)


def _hgrn(hg, lb, gain, batch, p_rows, chunk=128, rows=128):
    nc = p_rows // rows
    lead_blocks = LEAD // rows
    tri = jnp.asarray(np.tril(np.ones((chunk, chunk), np.float32)), BF16)
    col = lambda j: pl.BlockSpec((rows, HG_WIDTH), lambda b, c: (b * nc + c, j))
    vec = pl.BlockSpec((1, HG_WIDTH), lambda b, c: (0, 0))
    return pl.pallas_call(
        functools.partial(_hgrn_kernel, chunk=chunk, rows=rows),
        grid=(batch, nc),
        in_specs=[col(0), col(1), col(2), col(3), vec, vec,
                  pl.BlockSpec((chunk, chunk), lambda b, c: (0, 0))],
        out_specs=pl.BlockSpec((1, rows, HG_WIDTH), lambda b, c: (b, jnp.maximum(c - lead_blocks, 0), 0)),
        out_shape=jax.ShapeDtypeStruct((batch, p_rows - LEAD, HG_WIDTH), BF16),
        scratch_shapes=[pltpu.VMEM((HG_HEADS, HG_D, HG_D), F32)],
        compiler_params=_cparams(("parallel", "arbitrary")),
        name="hgrn2",
    )(hg, hg, hg, hg, lb, gain, tri)


def _rms(x, gain):
    ms = jnp.mean(x * x, axis=-1, keepdims=True)
    return x * lax.rsqrt(ms + EPS) * gain


def _qprep_kernel(cq_ref, gain_ref, w_ref, cos_ref, sin_ref, o_ref):
    xn = _rms(cq_ref[...].astype(F32), gain_ref[...]).astype(BF16)
    cos = cos_ref[...]
    sin = sin_ref[...]
    for h in range(MLA_HEADS):
        y = jnp.dot(xn, w_ref[h], preferred_element_type=F32)
        o_ref[0, h, :, :MLA_NOPE] = (y[:, :MLA_NOPE] * MLA_SCALE).astype(o_ref.dtype)
        t = y[:, MLA_NOPE:]
        roped = t * cos + pltpu.roll(t, MLA_ROPE, axis=1) * sin
        o_ref[0, h, :, MLA_NOPE:] = (roped * MLA_SCALE).astype(o_ref.dtype)


def _qprep(cq, gain, wq, cos, sin, batch, seq):
    tm = _pick(seq, 512)
    nb = seq // tm
    return pl.pallas_call(
        _qprep_kernel,
        grid=(batch, nb),
        in_specs=[pl.BlockSpec((tm, MLA_Q_RANK), lambda b, i: (b * nb + i, 0)),
                  pl.BlockSpec((1, MLA_Q_RANK), lambda b, i: (0, 0)),
                  pl.BlockSpec((MLA_HEADS, MLA_Q_RANK, MLA_QK), lambda b, i: (0, 0, 0)),
                  pl.BlockSpec((tm, MLA_NOPE), lambda b, i: (i, 0)),
                  pl.BlockSpec((tm, MLA_NOPE), lambda b, i: (i, 0))],
        out_specs=pl.BlockSpec((1, MLA_HEADS, tm, MLA_QK), lambda b, i: (b, 0, i, 0)),
        out_shape=jax.ShapeDtypeStruct((batch, MLA_HEADS, seq, MLA_QK), BF16),
        compiler_params=_cparams(("parallel", "parallel")),
        name="mla_q",
    )(cq, gain, wq, cos, sin)


def _kvprep_kernel(kv_ref, gain_ref, w_ref, cos_ref, sin_ref, k_ref, v_ref):
    x = kv_ref[...]
    xn = _rms(x[:, :MLA_KV_RANK].astype(F32), gain_ref[...]).astype(BF16)
    kr = (x[:, MLA_KV_RANK:MLA_KV_RANK + MLA_NOPE].astype(F32) * cos_ref[...]
          + x[:, MLA_KV_RANK + MLA_NOPE:].astype(F32) * sin_ref[...]).astype(k_ref.dtype)
    for h in range(MLA_HEADS):
        y = jnp.dot(xn, w_ref[h], preferred_element_type=F32)
        k_ref[0, h, :, :MLA_NOPE] = y[:, :MLA_NOPE].astype(k_ref.dtype)
        k_ref[0, h, :, MLA_NOPE:] = kr
        v_ref[0, h] = y[:, MLA_NOPE:].astype(v_ref.dtype)


def _kvprep(kvp, gain, wkv, cos, sin, batch, p_rows):
    tm = p_rows // 2
    return pl.pallas_call(
        _kvprep_kernel,
        grid=(batch, 2),
        in_specs=[pl.BlockSpec((tm, MLA_KV_RANK + 2 * MLA_NOPE), lambda b, i: (b * 2 + i, 0)),
                  pl.BlockSpec((1, MLA_KV_RANK), lambda b, i: (0, 0)),
                  pl.BlockSpec((MLA_HEADS, MLA_KV_RANK, MLA_NOPE + MLA_V), lambda b, i: (0, 0, 0)),
                  pl.BlockSpec((tm, MLA_NOPE), lambda b, i: (i, 0)),
                  pl.BlockSpec((tm, MLA_NOPE), lambda b, i: (i, 0))],
        out_specs=[pl.BlockSpec((1, MLA_HEADS, tm, MLA_QK), lambda b, i: (b, 0, i, 0)),
                   pl.BlockSpec((1, MLA_HEADS, tm, MLA_V), lambda b, i: (b, 0, i, 0))],
        out_shape=[jax.ShapeDtypeStruct((batch, MLA_HEADS, p_rows, MLA_QK), BF16),
                   jax.ShapeDtypeStruct((batch, MLA_HEADS, p_rows, MLA_V), BF16)],
        compiler_params=_cparams(("parallel", "parallel")),
        name="mla_kv",
    )(kvp, gain, wkv, cos, sin)


def _flash_segments(seq, tq, tk):
    segs = []
    for qi in range(seq // tq):
        pos = 0
        while pos < qi * tq:
            n = min(tk, qi * tq - pos)
            segs.append((qi, False, pos, n))
            pos += n
        segs.append((qi, True, qi * tq, tq))
    return segs


def _flash_kernel(q_ref, k_ref, v_ref, o_ref, *, tq, tk):
    seq = q_ref.shape[2]
    nt = (((1,), (1,)), ((), ()))
    causal = (lax.broadcasted_iota(jnp.int32, (tq, tq), 1) <= lax.broadcasted_iota(jnp.int32, (tq, tq), 0))
    lead_mask = lax.broadcasted_iota(jnp.int32, (tq, LEAD), 1) >= N_PAD

    def scores(seg):
        qi, diag, k0, n = seg
        q = q_ref[0, 0, qi * tq:(qi + 1) * tq, :]
        s = lax.dot_general(q, k_ref[0, 0, LEAD + k0:LEAD + k0 + n, :], nt, preferred_element_type=F32)
        if diag:
            s_lead = lax.dot_general(q, k_ref[0, 0, 0:LEAD, :], nt, preferred_element_type=F32)
            s = jnp.concatenate([jnp.where(causal, s, NEG), jnp.where(lead_mask, s_lead, NEG)], axis=1)
        return s

    segs = _flash_segments(seq, tq, tk)
    s_cur = scores(segs[0])
    m = l = acc = None
    for i, (qi, diag, k0, n) in enumerate(segs):
        s_next = scores(segs[i + 1]) if i + 1 < len(segs) else None
        first = k0 == 0
        m_new = jnp.max(s_cur, axis=-1, keepdims=True)
        if not first:
            m_new = jnp.maximum(m, m_new)
        p = jnp.exp(s_cur - m_new)
        p_sum = jnp.sum(p, axis=-1, keepdims=True)
        p = p.astype(BF16)
        if diag:
            pv = (jnp.dot(p[:, :tq], v_ref[0, 0, LEAD + k0:LEAD + k0 + n, :], preferred_element_type=F32)
                  + jnp.dot(p[:, tq:], v_ref[0, 0, 0:LEAD, :], preferred_element_type=F32))
        else:
            pv = jnp.dot(p, v_ref[0, 0, LEAD + k0:LEAD + k0 + n, :], preferred_element_type=F32)
        if first:
            l, acc = p_sum, pv
        else:
            corr = jnp.exp(m - m_new)
            l = corr * l + p_sum
            acc = corr * acc + pv
        m = m_new
        if diag:
            o_ref[0, qi * tq:(qi + 1) * tq, :] = (acc / l).astype(o_ref.dtype)
        s_cur = s_next


def _flash(q, k, v, batch, seq, p_rows, tq=256, tk=512):
    tq = min(tq, seq)
    return pl.pallas_call(
        functools.partial(_flash_kernel, tq=tq, tk=tk),
        grid=(batch, MLA_HEADS),
        in_specs=[pl.BlockSpec((1, 1, seq, MLA_QK), lambda b, h: (b, h, 0, 0)),
                  pl.BlockSpec((1, 1, p_rows, MLA_QK), lambda b, h: (b, h, 0, 0)),
                  pl.BlockSpec((1, 1, p_rows, MLA_V), lambda b, h: (b, h, 0, 0))],
        out_specs=pl.BlockSpec((1, seq, MLA_V), lambda b, h: (b, 0, h)),
        out_shape=jax.ShapeDtypeStruct((batch, seq, MLA_HEADS * MLA_V), BF16),
        compiler_params=_cparams(("parallel", "parallel")),
        name="mla_flash",
    )(q, k, v)


def _layer_norm(z, gain, bias):
    mu = jnp.mean(z, axis=-1, keepdims=True)
    zc = z - mu
    var = jnp.mean(zc * zc, axis=-1, keepdims=True)
    return zc * lax.rsqrt(var + EPS) * gain + bias


def _mix_kernel(x_ref, hg_ref, at_ref, ga_ref, gb_ref, wph_ref, wpm_ref, wo_ref, gain_ref, bias_ref,
                h_ref, ht_ref):
    ya = jnp.dot(hg_ref[...], wph_ref[...], preferred_element_type=F32)
    yb = jnp.dot(at_ref[...], wpm_ref[...], preferred_element_type=F32)
    mix = (jax.nn.sigmoid(ga_ref[...].astype(F32)) * ya + jax.nn.sigmoid(gb_ref[...].astype(F32)) * yb)
    mixed = jnp.dot(mix.astype(BF16), wo_ref[...], preferred_element_type=F32)
    h = _layer_norm(ALPHA * x_ref[...] + mixed, gain_ref[...], bias_ref[...])
    h_ref[...] = h
    ht_ref[...] = h.T.astype(ht_ref.dtype)


def _mix(x2, hg_o, at_o, gates, wph, wpm, wo, gain, bias):
    t = x2.shape[0]
    tm = _pick(t, 512)
    full = lambda a: pl.BlockSpec(a.shape, lambda i: (0,) * a.ndim)
    return pl.pallas_call(
        _mix_kernel,
        grid=(t // tm,),
        in_specs=[pl.BlockSpec((tm, D_MODEL), lambda i: (i, 0)),
                  pl.BlockSpec((tm, HG_WIDTH), lambda i: (i, 0)),
                  pl.BlockSpec((tm, MLA_HEADS * MLA_V), lambda i: (i, 0)),
                  pl.BlockSpec((tm, D_MODEL), lambda i: (i, 0)),
                  pl.BlockSpec((tm, D_MODEL), lambda i: (i, 1)),
                  full(wph), full(wpm), full(wo), full(gain), full(bias)],
        out_specs=[pl.BlockSpec((tm, D_MODEL), lambda i: (i, 0)),
                   pl.BlockSpec((D_MODEL, tm), lambda i: (0, i))],
        out_shape=[jax.ShapeDtypeStruct((t, D_MODEL), F32),
                   jax.ShapeDtypeStruct((D_MODEL, t), BF16)],
        compiler_params=_cparams(("parallel",)),
        name="mix_ln1",
    )(x2, hg_o, at_o, gates, gates, wph, wpm, wo, gain, bias)


N_TOP = PEER_TOPK + 1
SUBLANES = 8
TOP_ROWS = 24
PAIR_LIMIT = [N_TOP // (a + 1) for a in range(N_TOP)]
N_WIDE = sum(1 for n in PAIR_LIMIT if n > 1)
ROUTE_GROUP = 2


def _sort_network(n):
    def merge(lo, hi, r):
        step = r * 2
        if step < hi - lo:
            yield from merge(lo, hi, step)
            yield from merge(lo + r, hi, step)
            yield from [(i, i + r) for i in range(lo + r, hi - r, step)]
        else:
            yield (lo, lo + r)

    def sort(lo, hi):
        if hi - lo >= 1:
            mid = lo + (hi - lo) // 2
            yield from sort(lo, mid)
            yield from sort(mid + 1, hi)
            yield from merge(lo, hi, 1)

    return list(sort(0, n - 1))


def _sorted_slabs(slabs):
    n = 1 << (len(slabs) - 1).bit_length()
    v = list(slabs) + [None] * (n - len(slabs))
    for i, j in _sort_network(n):
        if v[j] is None:
            continue
        if v[i] is None:
            v[i], v[j] = v[j], None
        else:
            v[i], v[j] = jnp.maximum(v[i], v[j]), jnp.minimum(v[i], v[j])
    return v[:len(slabs)]


def _top_rows(slab_sets, n):
    sets = [list(c) for c in slab_sets]
    vals = [[] for _ in sets]
    for r in range(n):
        for i, cols in enumerate(sets):
            m = jnp.max(cols[0], axis=0, keepdims=True)
            vals[i].append(m)
            hit = cols[0] == m
            keep = min(len(cols), n - r - 1)
            sets[i] = [jnp.where(hit, cols[k + 1] if k + 1 < len(cols) else -jnp.inf, cols[k])
                       for k in range(keep)]
    return vals


def _count_above(rows, thr):
    count = jnp.zeros_like(thr)
    passed = []
    step = PEER_TOPK // 2
    while step >= 1:
        cands = [rows[base + step - 1] for base in range(0, PEER_TOPK, 2 * step)]
        for m in reversed(passed):
            cands = [jnp.where(m, cands[2 * i + 1], cands[2 * i]) for i in range(len(cands) // 2)]
        m = cands[0] > thr
        count = jnp.where(m, count + float(step), count)
        passed.append(m)
        step //= 2
    return jnp.where(rows[PEER_TOPK - 1] > thr, float(PEER_TOPK), count)


def _stack_rows(rows, tokens):
    ri = lax.broadcasted_iota(jnp.int32, (TOP_ROWS, tokens), 0)
    out = jnp.full((TOP_ROWS, tokens), -jnp.inf, F32)
    for r, v in enumerate(rows):
        out = jnp.where(ri == r, v, out)
    return out


def _route_kernel(ht_ref, wq_ref, sk_ref, r2_ref, c1_ref, e1_ref, e2_ref):
    tokens = ht_ref.shape[1]
    ri8 = lax.broadcasted_iota(jnp.int32, (N_WIDE, tokens), 0)
    qp = jnp.dot(wq_ref[...], ht_ref[...], preferred_element_type=F32).astype(BF16)
    for h0 in range(0, PEER_HEADS, ROUTE_GROUP):
        heads = range(h0, h0 + ROUTE_GROUP)
        s = [jnp.dot(sk_ref[hp], qp[hp * PEER_HALF:(hp + 1) * PEER_HALF], preferred_element_type=F32)
             for hp in range(2 * h0, 2 * (h0 + ROUTE_GROUP))]
        slabs = lambda x: [x[k:k + SUBLANES] for k in range(0, x.shape[0], SUBLANES)]
        tops = _top_rows([_sorted_slabs(slabs(x)) for x in s], N_TOP)
        cand = []
        for g in range(ROUTE_GROUP):
            t1, t2 = tops[2 * g], tops[2 * g + 1]
            t1s = _stack_rows(t1, tokens)
            t2s = _stack_rows(t2, tokens)
            parts = [jnp.where(ri8 < min(PAIR_LIMIT[a], N_WIDE), t1[a] + t2s[:N_WIDE], -jnp.inf)
                     for a in range(N_WIDE)]
            parts += slabs(t1[0] + t2s[N_WIDE:]) + slabs(t1s[N_WIDE:] + t2[0])
            cand.append(_sorted_slabs(parts))
        bests = _top_rows(cand, N_TOP)
        for g, h in enumerate(heads):
            s1, s2 = s[2 * g], s[2 * g + 1]
            t1, t2, best = tops[2 * g], tops[2 * g + 1], bests[g]
            m = best[0]
            z = jnp.zeros_like(m)
            for r in range(PEER_TOPK):
                z = z + jnp.exp(best[r] - m)
            thr = 0.5 * (best[PEER_TOPK - 1] + best[PEER_TOPK]) - s1
            c1 = _count_above(t2, thr)
            r2_ref[h] = _count_above(t2, s2).astype(r2_ref.dtype)
            c1_ref[h] = c1
            e1_ref[h] = jnp.exp(s1 - t1[0] - jnp.log(z))
            e2_ref[h] = jnp.exp(s2 - t2[0]).astype(e2_ref.dtype)


def _route(ht, wqt, sk):
    t = ht.shape[1]
    tm = _pick(t, 256)
    out = jax.ShapeDtypeStruct((PEER_HEADS, PEER_NKEYS, t), ht.dtype)
    out32 = jax.ShapeDtypeStruct((PEER_HEADS, PEER_NKEYS, t), F32)
    ospec = pl.BlockSpec((PEER_HEADS, PEER_NKEYS, tm), lambda i: (0, 0, i))
    return pl.pallas_call(
        _route_kernel,
        grid=(t // tm,),
        in_specs=[pl.BlockSpec((D_MODEL, tm), lambda i: (0, i)),
                  pl.BlockSpec(wqt.shape, lambda i: (0, 0)),
                  pl.BlockSpec(sk.shape, lambda i: (0, 0, 0))],
        out_specs=[ospec] * 4,
        out_shape=[out, out32, out32, out],
        compiler_params=_cparams(("parallel",)),
        name="peer_route",
    )(ht, wqt, sk)


GELU_C0 = 0.7978845608028654
GELU_C1 = GELU_C0 * 0.044715
LOG2_E = 1.4426950408889634


def _gelu(a):
    t = jnp.exp2(a * ((-2.0 * LOG2_E * GELU_C0) + (-2.0 * LOG2_E * GELU_C1) * (a * a)))
    return a / (1.0 + t)


def _expert_kernel(ht_ref, h_ref, u_ref, vt_ref, r2_ref, c1_ref, e1_ref, e2_ref, gain_ref, bias_ref,
                   o_ref, acc_ref, act_ref, *, chunk):
    e = pl.program_id(1)
    te, tm = act_ref.shape
    packed_rows = 16

    @pl.when(e == 0)
    def _():
        acc_ref[...] = jnp.zeros_like(acc_ref)

    def row_bcast(ref, h, r):
        row = jnp.broadcast_to(ref[h, r:r + 1, :], (packed_rows, tm)).astype(act_ref.dtype)
        return pltpu.repeat(row, PEER_NKEYS // packed_rows, axis=0)

    for c in range(te // chunk):
        a = jnp.dot(u_ref[c * chunk:(c + 1) * chunk, :], ht_ref[...], preferred_element_type=F32)
        for rr in range(chunk // PEER_NKEYS):
            r = c * (chunk // PEER_NKEYS) + rr
            g = None
            for h in range(PEER_HEADS):
                gate = jnp.where(r2_ref[h] < row_bcast(c1_ref, h, r), e2_ref[h] * row_bcast(e1_ref, h, r), 0.0)
                g = gate if g is None else g + gate
            a_r = a[rr * PEER_NKEYS:(rr + 1) * PEER_NKEYS, :]
            act_ref[r * PEER_NKEYS:(r + 1) * PEER_NKEYS, :] = _gelu(a_r.astype(act_ref.dtype)) * g
    acc_ref[...] += jnp.dot(vt_ref[...], act_ref[...], preferred_element_type=F32)

    @pl.when(e == pl.num_programs(1) - 1)
    def _():
        z = ALPHA * h_ref[...] + acc_ref[...].T
        o_ref[...] = _layer_norm(z, gain_ref[...], bias_ref[...])


def _experts(ht, h1, u, vt, r2, c1, e1, e2, gain, bias, tm=512, te=2048, chunk=512):
    t = h1.shape[0]
    tm = _pick(t, tm)
    hspec = pl.BlockSpec((PEER_HEADS, PEER_NKEYS, tm), lambda i, e: (0, 0, i))
    rspec = pl.BlockSpec((PEER_HEADS, te // PEER_NKEYS, tm), lambda i, e: (0, e, i))
    vec = pl.BlockSpec((1, D_MODEL), lambda i, e: (0, 0))
    return pl.pallas_call(
        functools.partial(_expert_kernel, chunk=chunk),
        grid=(t // tm, PEER_N // te),
        in_specs=[pl.BlockSpec((D_MODEL, tm), lambda i, e: (0, i)),
                  pl.BlockSpec((tm, D_MODEL), lambda i, e: (i, 0)),
                  pl.BlockSpec((te, D_MODEL), lambda i, e: (e, 0)),
                  pl.BlockSpec((D_MODEL, te), lambda i, e: (0, e)),
                  hspec, rspec, rspec, hspec, vec, vec],
        out_specs=pl.BlockSpec((tm, D_MODEL), lambda i, e: (i, 0)),
        out_shape=jax.ShapeDtypeStruct((t, D_MODEL), F32),
        scratch_shapes=[pltpu.VMEM((D_MODEL, tm), F32), pltpu.VMEM((te, tm), ht.dtype)],
        compiler_params=_cparams(("parallel", "arbitrary")),
        name="peer_experts",
    )(ht, h1, u, vt, r2, c1, e1, e2, gain, bias)


def _rope_tables(pos):
    half = MLA_ROPE // 2
    inv_freq = ROPE_BASE ** (-jnp.arange(half, dtype=F32) / half)
    ang = pos.astype(F32)[:, None] * inv_freq[None, :]
    zeros = jnp.zeros((pos.shape[0], MLA_NOPE - MLA_ROPE), F32)
    cos, sin = jnp.cos(ang), jnp.sin(ang)
    return jnp.concatenate([cos, cos, zeros], axis=1), jnp.concatenate([sin, sin, zeros], axis=1)


def _rot_cols(w):
    half = MLA_ROPE // 2
    return jnp.concatenate([-w[..., half:], w[..., :half]], axis=-1)


def kernel(x, meta_tokens, hgrn_lb_logits, w_in, q_norm_gain, kv_norm_gain, w_uq, w_ukv, hgrn_norm_gain,
           w_proj_hgrn, w_proj_mla, w_out, ln1_gain, ln1_bias, peer_query, peer_sub_keys, peer_u, peer_v,
           ln2_gain, ln2_bias):
    batch, seq, d = x.shape
    p_rows = LEAD + seq
    t_real = batch * seq
    l = 0

    h_all = jnp.concatenate([jnp.zeros((batch, N_PAD, d), BF16),
                             jnp.broadcast_to(meta_tokens.astype(BF16)[None], (batch, N_META, d)),
                             x.astype(BF16)], axis=1).reshape(batch * p_rows, d)
    x2 = x.reshape(t_real, d)
    x_bf = x2.astype(BF16)

    lower_bounds = jnp.cumsum(jax.nn.softmax(hgrn_lb_logits.astype(F32), axis=0), axis=0)
    rows = jnp.arange(p_rows)
    cos_all, sin_all = _rope_tables(rows - N_PAD)

    w = w_in[l]
    o_hg = 4 * HG_WIDTH
    o_cq = o_hg + MLA_Q_RANK
    o_ckv = o_cq + MLA_KV_RANK
    o_kr = o_ckv + MLA_ROPE
    w_hg = w[:, :o_hg].astype(BF16)
    w_cq = w[:, o_hg:o_cq].astype(BF16)
    w_kr = w[:, o_ckv:o_kr]
    zpad = jnp.zeros((d, MLA_NOPE - MLA_ROPE), F32)
    w_kv = jnp.concatenate([w[:, o_cq:o_ckv], w_kr, zpad, _rot_cols(w_kr), zpad], axis=1).astype(BF16)
    w_g = w[:, o_kr:].astype(BF16)

    wq3 = w_uq[l].reshape(MLA_Q_RANK, MLA_HEADS, MLA_NOPE + MLA_ROPE)
    wq_rope = wq3[..., MLA_NOPE:]
    wq = jnp.concatenate([wq3[..., :MLA_NOPE], wq_rope, _rot_cols(wq_rope)], axis=-1)
    wq = wq.transpose(1, 0, 2).astype(BF16)
    wkv = w_ukv[l].reshape(MLA_KV_RANK, MLA_HEADS, MLA_NOPE + MLA_V).transpose(1, 0, 2).astype(BF16)

    tm_all = _pick(batch * p_rows, 1024)
    tm_real = _pick(t_real, 1024)
    hg = _matmul(h_all, w_hg, tm_all, 1024)
    kvp = _matmul(h_all, w_kv, tm_all, w_kv.shape[1])
    cq = _matmul(x_bf, w_cq, tm_real, MLA_Q_RANK)
    gates = _matmul(x_bf, w_g, tm_real, 1024)

    hg_o = _hgrn(hg, lower_bounds[l][None, :], hgrn_norm_gain[l][None, :], batch, p_rows)

    q = _qprep(cq, q_norm_gain[l][None, :], wq, cos_all[LEAD:], sin_all[LEAD:], batch, seq)
    k, v = _kvprep(kvp, kv_norm_gain[l][None, :], wkv, cos_all, sin_all, batch, p_rows)
    at_o = _flash(q, k, v, batch, seq, p_rows)

    h1, h1t = _mix(x2, hg_o.reshape(t_real, HG_WIDTH), at_o.reshape(t_real, MLA_HEADS * MLA_V), gates,
                   w_proj_hgrn[l].astype(BF16), w_proj_mla[l].astype(BF16), w_out[l].astype(BF16),
                   ln1_gain[l][None, :], ln1_bias[l][None, :])

    wqt = peer_query[l].T.astype(BF16)
    sk = peer_sub_keys[l].reshape(PEER_HEADS * 2, PEER_NKEYS, PEER_HALF).astype(BF16)
    r2, c1, e1, e2 = _route(h1t, wqt, sk)

    out = _experts(h1t, h1, peer_u[l].astype(BF16), peer_v[l].T.astype(BF16), r2, c1, e1, e2,
                   ln2_gain[l][None, :], ln2_bias[l][None, :])
    return out.reshape(batch, seq, d)
```

```python
import functools

import numpy as np
import jax
import jax.numpy as jnp
from jax import lax
from jax.experimental import pallas as pl
from jax.experimental.pallas import tpu as pltpu

F32 = jnp.float32
BF16 = jnp.bfloat16

D_MODEL = 1024
DEPTH = 1
N_META = 16
LEAD = 128
N_PAD = LEAD - N_META

HG_HEADS = 8
HG_D = 128
HG_WIDTH = HG_HEADS * HG_D
HG_SUB = 16

MLA_HEADS = 16
MLA_NOPE = 128
MLA_ROPE = 64
MLA_V = 128
MLA_Q_RANK = 384
MLA_KV_RANK = 256
MLA_QK = 256
MLA_SCALE = (MLA_NOPE + MLA_ROPE) ** -0.5
ROPE_BASE = 10000.0

PEER_HEADS = 8
PEER_NKEYS = 128
PEER_N = PEER_NKEYS * PEER_NKEYS
PEER_HALF = 128
PEER_TOPK = 16

ALPHA = (2 * DEPTH) ** 0.25
EPS = 1e-5
NEG = -1e30

VMEM_LIMIT = 56 * 1024 * 1024
MXU_WIDTH = 256


def _cparams(sem):
    return pltpu.CompilerParams(dimension_semantics=sem, vmem_limit_bytes=VMEM_LIMIT)


def _pick(n, pref):
    t = min(n, pref)
    while n % t:
        t -= 128
    return t


def _mm_kernel(x_ref, w_ref, o_ref):
    o_ref[...] = jnp.dot(x_ref[...], w_ref[...], preferred_element_type=F32).astype(o_ref.dtype)


def _matmul(x, w, tm, tn):
    m, k = x.shape
    out_dtype = x.dtype
    n = w.shape[1]
    return pl.pallas_call(
        _mm_kernel,
        grid=(m // tm, n // tn),
        in_specs=[pl.BlockSpec((tm, k), lambda i, j: (i, 0)),
                  pl.BlockSpec((k, tn), lambda i, j: (0, j))],
        out_specs=pl.BlockSpec((tm, tn), lambda i, j: (i, j)),
        out_shape=jax.ShapeDtypeStruct((m, n), out_dtype),
        compiler_params=_cparams(("parallel", "parallel")),
        name="in_proj",
    )(x, w)


def _hgrn_chunk(q, k, lf, iv, st_ref, tri, masks, chunk):
    width = q.shape[1]
    hi = lf.astype(BF16)
    r1 = lf - hi.astype(F32)
    mid = r1.astype(BF16)
    lo = (r1 - mid.astype(F32)).astype(BF16)
    g = (jnp.dot(tri, hi, preferred_element_type=F32)
         + jnp.dot(tri, mid, preferred_element_type=F32)
         + jnp.dot(tri, lo, preferred_element_type=F32))

    def bcast_row(r, n):
        return jnp.broadcast_to(g[r:r + 1, :], (n, width))

    def prev_end(s):
        parts = [jnp.zeros((s, width), F32)] + [bcast_row(b * s - 1, s) for b in range(1, chunk // s)]
        return parts[0] if len(parts) == 1 else jnp.concatenate(parts, axis=0)

    def own_end(s):
        parts = [bcast_row((b + 1) * s - 1, s) for b in range(chunk // s)]
        return parts[0] if len(parts) == 1 else jnp.concatenate(parts, axis=0)

    q_c = (q * jnp.exp(g)).astype(BF16)
    k_c = (k * jnp.exp(own_end(chunk) - g)).astype(BF16)
    dec = jnp.exp(g[chunk - 1:chunk, :])
    p16 = prev_end(HG_SUB)
    q_lv = [(q * jnp.exp(g - p16)).astype(BF16)]
    k_lv = [(k * jnp.exp(p16 - g)).astype(BF16)]
    s = HG_SUB
    while s < chunk:
        q_lv.append(q_lv[0] if s == HG_SUB else (q * jnp.exp(g - prev_end(s))).astype(BF16))
        k_lv.append((k * jnp.exp(own_end(s) - g)).astype(BF16))
        s *= 2

    nt = (((1,), (1,)), ((), ()))
    tn = (((0,), (0,)), ((), ()))
    heads = range(HG_HEADS)
    hs = [slice(h * HG_D, (h + 1) * HG_D) for h in heads]
    st = [st_ref[h] for h in heads]
    o_inter = [lax.dot_general(q_c[:, hs[h]], st[h].astype(BF16), nt, preferred_element_type=F32) for h in heads]
    upd = [lax.dot_general(iv[:, hs[h]], k_c[:, hs[h]], tn, preferred_element_type=F32) for h in heads]
    scores = [[lax.dot_general(ql[:, hs[h]], kl[:, hs[h]], nt, preferred_element_type=F32) for h in heads]
              for ql, kl in zip(q_lv, k_lv)]
    outs, st_new = [], []
    for h in heads:
        a = jnp.where(masks[0], scores[0][h], 0.0)
        for lvl in range(1, len(masks)):
            a = jnp.where(masks[lvl], scores[lvl][h], a)
        outs.append(o_inter[h] + jnp.dot(a.astype(BF16), iv[:, hs[h]], preferred_element_type=F32))
        st_new.append(st[h] * dec[:, hs[h]] + upd[h])
    return outs, st_new


def _hgrn_kernel(q_ref, f_ref, i_ref, g_ref, lb_ref, gain_ref, tri_ref, o_ref, st_ref, *, chunk, rows):
    c = pl.program_id(1)

    @pl.when(c == 0)
    def _():
        st_ref[...] = jnp.zeros_like(st_ref)

    ri = lax.broadcasted_iota(jnp.int32, (chunk, chunk), 0)
    ci = lax.broadcasted_iota(jnp.int32, (chunk, chunk), 1)
    blk = lambda v, s: lax.shift_right_logical(v, s.bit_length() - 1)
    masks = [(blk(ri, HG_SUB) == blk(ci, HG_SUB)) & (ci <= ri)]
    s = HG_SUB
    while s < chunk:
        masks.append((blk(ri, 2 * s) == blk(ci, 2 * s)) & ((blk(ri, s) & 1) == 1) & ((blk(ci, s) & 1) == 0))
        s *= 2
    tri = tri_ref[...]
    lb = lb_ref[...]

    for n in range(rows // chunk):
        rs = slice(n * chunk, (n + 1) * chunk)
        seq_row = c * rows + n * chunk + lax.broadcasted_iota(jnp.int32, (chunk, HG_WIDTH), 0)
        valid = seq_row >= N_PAD
        f = lb + (1.0 - lb) * jax.nn.sigmoid(f_ref[rs, :].astype(F32))
        lf = jnp.where(valid, jnp.log(f), 0.0)
        k = jnp.where(valid, 1.0 - f, 0.0)
        outs, st_new = _hgrn_chunk(q_ref[rs, :].astype(F32), k, lf, i_ref[rs, :], st_ref, tri, masks, chunk)
        gate = g_ref[rs, :].astype(F32)
        gate = gate * jax.nn.sigmoid(gate)
        for h in range(HG_HEADS):
            hs = slice(h * HG_D, (h + 1) * HG_D)
            st_ref[h] = st_new[h]
            o = outs[h]
            ms = jnp.mean(o * o, axis=-1, keepdims=True)
            o = o * lax.rsqrt(ms + EPS) * gain_ref[:, hs]
            o_ref[0, rs, hs] = (o * gate[:, hs]).astype(o_ref.dtype)


def _hgrn(hg, lb, gain, batch, p_rows, chunk=128, rows=128):
    nc = p_rows // rows
    lead_blocks = LEAD // rows
    tri = jnp.asarray(np.tril(np.ones((chunk, chunk), np.float32)), BF16)
    col = lambda j: pl.BlockSpec((rows, HG_WIDTH), lambda b, c: (b * nc + c, j))
    vec = pl.BlockSpec((1, HG_WIDTH), lambda b, c: (0, 0))
    return pl.pallas_call(
        functools.partial(_hgrn_kernel, chunk=chunk, rows=rows),
        grid=(batch, nc),
        in_specs=[col(0), col(1), col(2), col(3), vec, vec,
                  pl.BlockSpec((chunk, chunk), lambda b, c: (0, 0))],
        out_specs=pl.BlockSpec((1, rows, HG_WIDTH), lambda b, c: (b, jnp.maximum(c - lead_blocks, 0), 0)),
        out_shape=jax.ShapeDtypeStruct((batch, p_rows - LEAD, HG_WIDTH), BF16),
        scratch_shapes=[pltpu.VMEM((HG_HEADS, HG_D, HG_D), F32)],
        compiler_params=_cparams(("parallel", "arbitrary")),
        name="hgrn2",
    )(hg, hg, hg, hg, lb, gain, tri)


def _rms(x, gain):
    ms = jnp.mean(x * x, axis=-1, keepdims=True)
    return x * lax.rsqrt(ms + EPS) * gain


def _qprep_kernel(cq_ref, gain_ref, w_ref, cos_ref, sin_ref, o_ref):
    xn = _rms(cq_ref[...].astype(F32), gain_ref[...]).astype(BF16)
    cos = cos_ref[...]
    sin = sin_ref[...]
    for h in range(MLA_HEADS):
        y = jnp.dot(xn, w_ref[h], preferred_element_type=F32)
        o_ref[0, h, :, :MLA_NOPE] = (y[:, :MLA_NOPE] * MLA_SCALE).astype(o_ref.dtype)
        t = y[:, MLA_NOPE:]
        roped = t * cos + pltpu.roll(t, MLA_ROPE, axis=1) * sin
        o_ref[0, h, :, MLA_NOPE:] = (roped * MLA_SCALE).astype(o_ref.dtype)


def _qprep(cq, gain, wq, cos, sin, batch, seq):
    tm = _pick(seq, 512)
    nb = seq // tm
    return pl.pallas_call(
        _qprep_kernel,
        grid=(batch, nb),
        in_specs=[pl.BlockSpec((tm, MLA_Q_RANK), lambda b, i: (b * nb + i, 0)),
                  pl.BlockSpec((1, MLA_Q_RANK), lambda b, i: (0, 0)),
                  pl.BlockSpec((MLA_HEADS, MLA_Q_RANK, MLA_QK), lambda b, i: (0, 0, 0)),
                  pl.BlockSpec((tm, MLA_NOPE), lambda b, i: (i, 0)),
                  pl.BlockSpec((tm, MLA_NOPE), lambda b, i: (i, 0))],
        out_specs=pl.BlockSpec((1, MLA_HEADS, tm, MLA_QK), lambda b, i: (b, 0, i, 0)),
        out_shape=jax.ShapeDtypeStruct((batch, MLA_HEADS, seq, MLA_QK), BF16),
        compiler_params=_cparams(("parallel", "parallel")),
        name="mla_q",
    )(cq, gain, wq, cos, sin)


def _kvprep_kernel(kv_ref, gain_ref, w_ref, cos_ref, sin_ref, k_ref, v_ref):
    x = kv_ref[...]
    xn = _rms(x[:, :MLA_KV_RANK].astype(F32), gain_ref[...]).astype(BF16)
    kr = (x[:, MLA_KV_RANK:MLA_KV_RANK + MLA_NOPE].astype(F32) * cos_ref[...]
          + x[:, MLA_KV_RANK + MLA_NOPE:].astype(F32) * sin_ref[...]).astype(k_ref.dtype)
    for h in range(MLA_HEADS):
        y = jnp.dot(xn, w_ref[h], preferred_element_type=F32)
        k_ref[0, h, :, :MLA_NOPE] = y[:, :MLA_NOPE].astype(k_ref.dtype)
        k_ref[0, h, :, MLA_NOPE:] = kr
        v_ref[0, h] = y[:, MLA_NOPE:].astype(v_ref.dtype)


def _kvprep(kvp, gain, wkv, cos, sin, batch, p_rows):
    tm = p_rows // 2
    return pl.pallas_call(
        _kvprep_kernel,
        grid=(batch, 2),
        in_specs=[pl.BlockSpec((tm, MLA_KV_RANK + 2 * MLA_NOPE), lambda b, i: (b * 2 + i, 0)),
                  pl.BlockSpec((1, MLA_KV_RANK), lambda b, i: (0, 0)),
                  pl.BlockSpec((MLA_HEADS, MLA_KV_RANK, MLA_NOPE + MLA_V), lambda b, i: (0, 0, 0)),
                  pl.BlockSpec((tm, MLA_NOPE), lambda b, i: (i, 0)),
                  pl.BlockSpec((tm, MLA_NOPE), lambda b, i: (i, 0))],
        out_specs=[pl.BlockSpec((1, MLA_HEADS, tm, MLA_QK), lambda b, i: (b, 0, i, 0)),
                   pl.BlockSpec((1, MLA_HEADS, tm, MLA_V), lambda b, i: (b, 0, i, 0))],
        out_shape=[jax.ShapeDtypeStruct((batch, MLA_HEADS, p_rows, MLA_QK), BF16),
                   jax.ShapeDtypeStruct((batch, MLA_HEADS, p_rows, MLA_V), BF16)],
        compiler_params=_cparams(("parallel", "parallel")),
        name="mla_kv",
    )(kvp, gain, wkv, cos, sin)


def _flash_segments(seq, tq, tk):
    segs = []
    for qi in range(seq // tq):
        pos = 0
        while pos < qi * tq:
            n = min(tk, qi * tq - pos)
            segs.append((qi, False, pos, n))
            pos += n
        segs.append((qi, True, qi * tq, tq))
    return segs


def _flash_kernel(q_ref, k_ref, v_ref, o_ref, *, tq, tk):
    seq = q_ref.shape[2]
    nt = (((1,), (1,)), ((), ()))
    causal = (lax.broadcasted_iota(jnp.int32, (tq, tq), 1) <= lax.broadcasted_iota(jnp.int32, (tq, tq), 0))
    lead_mask = lax.broadcasted_iota(jnp.int32, (tq, LEAD), 1) >= N_PAD

    def scores(seg):
        qi, diag, k0, n = seg
        q = q_ref[0, 0, qi * tq:(qi + 1) * tq, :]
        s = lax.dot_general(q, k_ref[0, 0, LEAD + k0:LEAD + k0 + n, :], nt, preferred_element_type=F32)
        if diag:
            s_lead = lax.dot_general(q, k_ref[0, 0, 0:LEAD, :], nt, preferred_element_type=F32)
            s = jnp.concatenate([jnp.where(causal, s, NEG), jnp.where(lead_mask, s_lead, NEG)], axis=1)
        return s

    segs = _flash_segments(seq, tq, tk)
    s_cur = scores(segs[0])
    m = l = acc = None
    for i, (qi, diag, k0, n) in enumerate(segs):
        s_next = scores(segs[i + 1]) if i + 1 < len(segs) else None
        first = k0 == 0
        m_new = jnp.max(s_cur, axis=-1, keepdims=True)
        if not first:
            m_new = jnp.maximum(m, m_new)
        p = jnp.exp(s_cur - m_new)
        p_sum = jnp.sum(p, axis=-1, keepdims=True)
        p = p.astype(BF16)
        if diag:
            pv = (jnp.dot(p[:, :tq], v_ref[0, 0, LEAD + k0:LEAD + k0 + n, :], preferred_element_type=F32)
                  + jnp.dot(p[:, tq:], v_ref[0, 0, 0:LEAD, :], preferred_element_type=F32))
        else:
            pv = jnp.dot(p, v_ref[0, 0, LEAD + k0:LEAD + k0 + n, :], preferred_element_type=F32)
        if first:
            l, acc = p_sum, pv
        else:
            corr = jnp.exp(m - m_new)
            l = corr * l + p_sum
            acc = corr * acc + pv
        m = m_new
        if diag:
            o_ref[0, qi * tq:(qi + 1) * tq, :] = (acc / l).astype(o_ref.dtype)
        s_cur = s_next


def _flash(q, k, v, batch, seq, p_rows, tq=256, tk=512):
    tq = min(tq, seq)
    return pl.pallas_call(
        functools.partial(_flash_kernel, tq=tq, tk=tk),
        grid=(batch, MLA_HEADS),
        in_specs=[pl.BlockSpec((1, 1, seq, MLA_QK), lambda b, h: (b, h, 0, 0)),
                  pl.BlockSpec((1, 1, p_rows, MLA_QK), lambda b, h: (b, h, 0, 0)),
                  pl.BlockSpec((1, 1, p_rows, MLA_V), lambda b, h: (b, h, 0, 0))],
        out_specs=pl.BlockSpec((1, seq, MLA_V), lambda b, h: (b, 0, h)),
        out_shape=jax.ShapeDtypeStruct((batch, seq, MLA_HEADS * MLA_V), BF16),
        compiler_params=_cparams(("parallel", "parallel")),
        name="mla_flash",
    )(q, k, v)


def _layer_norm(z, gain, bias):
    mu = jnp.mean(z, axis=-1, keepdims=True)
    zc = z - mu
    var = jnp.mean(zc * zc, axis=-1, keepdims=True)
    return zc * lax.rsqrt(var + EPS) * gain + bias


def _mix_kernel(x_ref, hg_ref, at_ref, ga_ref, gb_ref, wph_ref, wpm_ref, wo_ref, gain_ref, bias_ref,
                h_ref, ht_ref):
    ya = jnp.dot(hg_ref[...], wph_ref[...], preferred_element_type=F32)
    yb = jnp.dot(at_ref[...], wpm_ref[...], preferred_element_type=F32)
    mix = (jax.nn.sigmoid(ga_ref[...].astype(F32)) * ya + jax.nn.sigmoid(gb_ref[...].astype(F32)) * yb)
    mixed = jnp.dot(mix.astype(BF16), wo_ref[...], preferred_element_type=F32)
    h = _layer_norm(ALPHA * x_ref[...] + mixed, gain_ref[...], bias_ref[...])
    h_ref[...] = h
    ht_ref[...] = h.T.astype(ht_ref.dtype)


def _mix(x2, hg_o, at_o, gates, wph, wpm, wo, gain, bias):
    t = x2.shape[0]
    tm = _pick(t, 512)
    full = lambda a: pl.BlockSpec(a.shape, lambda i: (0,) * a.ndim)
    return pl.pallas_call(
        _mix_kernel,
        grid=(t // tm,),
        in_specs=[pl.BlockSpec((tm, D_MODEL), lambda i: (i, 0)),
                  pl.BlockSpec((tm, HG_WIDTH), lambda i: (i, 0)),
                  pl.BlockSpec((tm, MLA_HEADS * MLA_V), lambda i: (i, 0)),
                  pl.BlockSpec((tm, D_MODEL), lambda i: (i, 0)),
                  pl.BlockSpec((tm, D_MODEL), lambda i: (i, 1)),
                  full(wph), full(wpm), full(wo), full(gain), full(bias)],
        out_specs=[pl.BlockSpec((tm, D_MODEL), lambda i: (i, 0)),
                   pl.BlockSpec((D_MODEL, tm), lambda i: (0, i))],
        out_shape=[jax.ShapeDtypeStruct((t, D_MODEL), F32),
                   jax.ShapeDtypeStruct((D_MODEL, t), BF16)],
        compiler_params=_cparams(("parallel",)),
        name="mix_ln1",
    )(x2, hg_o, at_o, gates, gates, wph, wpm, wo, gain, bias)


N_TOP = PEER_TOPK + 1
SUBLANES = 8
TOP_ROWS = 24
PAIR_LIMIT = [N_TOP // (a + 1) for a in range(N_TOP)]
N_WIDE = sum(1 for n in PAIR_LIMIT if n > 1)
ROUTE_GROUP = 2


def _sort_network(n):
    def merge(lo, hi, r):
        step = r * 2
        if step < hi - lo:
            yield from merge(lo, hi, step)
            yield from merge(lo + r, hi, step)
            yield from [(i, i + r) for i in range(lo + r, hi - r, step)]
        else:
            yield (lo, lo + r)

    def sort(lo, hi):
        if hi - lo >= 1:
            mid = lo + (hi - lo) // 2
            yield from sort(lo, mid)
            yield from sort(mid + 1, hi)
            yield from merge(lo, hi, 1)

    return list(sort(0, n - 1))


def _sorted_slabs(slabs):
    n = 1 << (len(slabs) - 1).bit_length()
    v = list(slabs) + [None] * (n - len(slabs))
    for i, j in _sort_network(n):
        if v[j] is None:
            continue
        if v[i] is None:
            v[i], v[j] = v[j], None
        else:
            v[i], v[j] = jnp.maximum(v[i], v[j]), jnp.minimum(v[i], v[j])
    return v[:len(slabs)]


def _top_rows(slab_sets, n):
    sets = [list(c) for c in slab_sets]
    vals = [[] for _ in sets]
    for r in range(n):
        for i, cols in enumerate(sets):
            m = jnp.max(cols[0], axis=0, keepdims=True)
            vals[i].append(m)
            hit = cols[0] == m
            keep = min(len(cols), n - r - 1)
            sets[i] = [jnp.where(hit, cols[k + 1] if k + 1 < len(cols) else -jnp.inf, cols[k])
                       for k in range(keep)]
    return vals


def _count_above(rows, thr):
    count = jnp.zeros_like(thr)
    passed = []
    step = PEER_TOPK // 2
    while step >= 1:
        cands = [rows[base + step - 1] for base in range(0, PEER_TOPK, 2 * step)]
        for m in reversed(passed):
            cands = [jnp.where(m, cands[2 * i + 1], cands[2 * i]) for i in range(len(cands) // 2)]
        m = cands[0] > thr
        count = jnp.where(m, count + float(step), count)
        passed.append(m)
        step //= 2
    return jnp.where(rows[PEER_TOPK - 1] > thr, float(PEER_TOPK), count)


def _stack_rows(rows, tokens):
    ri = lax.broadcasted_iota(jnp.int32, (TOP_ROWS, tokens), 0)
    out = jnp.full((TOP_ROWS, tokens), -jnp.inf, F32)
    for r, v in enumerate(rows):
        out = jnp.where(ri == r, v, out)
    return out


def _route_kernel(ht_ref, wq_ref, sk_ref, r2_ref, c1_ref, e1_ref, e2_ref):
    tokens = ht_ref.shape[1]
    ri8 = lax.broadcasted_iota(jnp.int32, (N_WIDE, tokens), 0)
    qp = jnp.dot(wq_ref[...], ht_ref[...], preferred_element_type=F32).astype(BF16)
    for h0 in range(0, PEER_HEADS, ROUTE_GROUP):
        heads = range(h0, h0 + ROUTE_GROUP)
        s = [jnp.dot(sk_ref[hp], qp[hp * PEER_HALF:(hp + 1) * PEER_HALF], preferred_element_type=F32)
             for hp in range(2 * h0, 2 * (h0 + ROUTE_GROUP))]
        slabs = lambda x: [x[k:k + SUBLANES] for k in range(0, x.shape[0], SUBLANES)]
        tops = _top_rows([_sorted_slabs(slabs(x)) for x in s], N_TOP)
        cand = []
        for g in range(ROUTE_GROUP):
            t1, t2 = tops[2 * g], tops[2 * g + 1]
            t1s = _stack_rows(t1, tokens)
            t2s = _stack_rows(t2, tokens)
            parts = [jnp.where(ri8 < min(PAIR_LIMIT[a], N_WIDE), t1[a] + t2s[:N_WIDE], -jnp.inf)
                     for a in range(N_WIDE)]
            parts += slabs(t1[0] + t2s[N_WIDE:]) + slabs(t1s[N_WIDE:] + t2[0])
            cand.append(_sorted_slabs(parts))
        bests = _top_rows(cand, N_TOP)
        for g, h in enumerate(heads):
            s1, s2 = s[2 * g], s[2 * g + 1]
            t1, t2, best = tops[2 * g], tops[2 * g + 1], bests[g]
            m = best[0]
            z = jnp.zeros_like(m)
            for r in range(PEER_TOPK):
                z = z + jnp.exp(best[r] - m)
            thr = 0.5 * (best[PEER_TOPK - 1] + best[PEER_TOPK]) - s1
            c1 = _count_above(t2, thr)
            r2_ref[h] = _count_above(t2, s2).astype(r2_ref.dtype)
            c1_ref[h] = c1
            e1_ref[h] = jnp.exp(s1 - t1[0] - jnp.log(z))
            e2_ref[h] = jnp.exp(s2 - t2[0]).astype(e2_ref.dtype)


def _route(ht, wqt, sk):
    t = ht.shape[1]
    tm = _pick(t, 256)
    out = jax.ShapeDtypeStruct((PEER_HEADS, PEER_NKEYS, t), ht.dtype)
    out32 = jax.ShapeDtypeStruct((PEER_HEADS, PEER_NKEYS, t), F32)
    ospec = pl.BlockSpec((PEER_HEADS, PEER_NKEYS, tm), lambda i: (0, 0, i))
    return pl.pallas_call(
        _route_kernel,
        grid=(t // tm,),
        in_specs=[pl.BlockSpec((D_MODEL, tm), lambda i: (0, i)),
                  pl.BlockSpec(wqt.shape, lambda i: (0, 0)),
                  pl.BlockSpec(sk.shape, lambda i: (0, 0, 0))],
        out_specs=[ospec] * 4,
        out_shape=[out, out32, out32, out],
        compiler_params=_cparams(("parallel",)),
        name="peer_route",
    )(ht, wqt, sk)


GELU_C0 = 0.7978845608028654
GELU_C1 = GELU_C0 * 0.044715
LOG2_E = 1.4426950408889634


def _gelu(a):
    t = jnp.exp2(a * ((-2.0 * LOG2_E * GELU_C0) + (-2.0 * LOG2_E * GELU_C1) * (a * a)))
    return a / (1.0 + t)


def _expert_kernel(ht_ref, h_ref, u_ref, vt_ref, r2_ref, c1_ref, e1_ref, e2_ref, gain_ref, bias_ref,
                   o_ref, acc_ref, *act_refs, chunk):
    e = pl.program_id(1)
    te, width = act_refs[0].shape
    dtype = act_refs[0].dtype
    packed_rows = 16

    @pl.when(e == 0)
    def _():
        acc_ref[...] = jnp.zeros_like(acc_ref)

    def row_bcast(ref, h, r, ls):
        row = jnp.broadcast_to(ref[h, r:r + 1, ls], (packed_rows, width)).astype(dtype)
        return pltpu.repeat(row, PEER_NKEYS // packed_rows, axis=0)

    lanes = [slice(i * width, (i + 1) * width) for i in range(len(act_refs))]
    a_all = [[jnp.dot(u_ref[c * chunk:(c + 1) * chunk, :], ht_ref[:, ls], preferred_element_type=F32)
              for c in range(te // chunk)] for ls in lanes]
    for act_ref, ls, a_chunks in zip(act_refs, lanes, a_all):
        for c, a in enumerate(a_chunks):
            for rr in range(chunk // PEER_NKEYS):
                r = c * (chunk // PEER_NKEYS) + rr
                g = None
                for h in range(PEER_HEADS):
                    gate = jnp.where(r2_ref[h, :, ls] < row_bcast(c1_ref, h, r, ls),
                                     e2_ref[h, :, ls] * row_bcast(e1_ref, h, r, ls), 0.0)
                    g = gate if g is None else g + gate
                a_r = a[rr * PEER_NKEYS:(rr + 1) * PEER_NKEYS, :]
                act_ref[r * PEER_NKEYS:(r + 1) * PEER_NKEYS, :] = _gelu(a_r.astype(dtype)) * g
        acc_ref[:, ls] += jnp.dot(vt_ref[...], act_ref[...], preferred_element_type=F32)

    @pl.when(e == pl.num_programs(1) - 1)
    def _():
        z = ALPHA * h_ref[...] + acc_ref[...].T
        o_ref[...] = _layer_norm(z, gain_ref[...], bias_ref[...])


def _experts(ht, h1, u, vt, r2, c1, e1, e2, gain, bias, tm=512, te=2048, chunk=512):
    t = h1.shape[0]
    tm = _pick(t, tm)
    hspec = pl.BlockSpec((PEER_HEADS, PEER_NKEYS, tm), lambda i, e: (0, 0, i))
    rspec = pl.BlockSpec((PEER_HEADS, te // PEER_NKEYS, tm), lambda i, e: (0, e, i))
    vec = pl.BlockSpec((1, D_MODEL), lambda i, e: (0, 0))
    return pl.pallas_call(
        functools.partial(_expert_kernel, chunk=chunk),
        grid=(t // tm, PEER_N // te),
        in_specs=[pl.BlockSpec((D_MODEL, tm), lambda i, e: (0, i)),
                  pl.BlockSpec((tm, D_MODEL), lambda i, e: (i, 0)),
                  pl.BlockSpec((te, D_MODEL), lambda i, e: (e, 0)),
                  pl.BlockSpec((D_MODEL, te), lambda i, e: (0, e)),
                  hspec, rspec, rspec, hspec, vec, vec],
        out_specs=pl.BlockSpec((tm, D_MODEL), lambda i, e: (i, 0)),
        out_shape=jax.ShapeDtypeStruct((t, D_MODEL), F32),
        scratch_shapes=[pltpu.VMEM((D_MODEL, tm), F32)] + [pltpu.VMEM((te, MXU_WIDTH), ht.dtype)] * (tm // MXU_WIDTH),
        compiler_params=_cparams(("parallel", "arbitrary")),
        name="peer_experts",
    )(ht, h1, u, vt, r2, c1, e1, e2, gain, bias)


def _rope_tables(pos):
    half = MLA_ROPE // 2
    inv_freq = ROPE_BASE ** (-jnp.arange(half, dtype=F32) / half)
    ang = pos.astype(F32)[:, None] * inv_freq[None, :]
    zeros = jnp.zeros((pos.shape[0], MLA_NOPE - MLA_ROPE), F32)
    cos, sin = jnp.cos(ang), jnp.sin(ang)
    return jnp.concatenate([cos, cos, zeros], axis=1), jnp.concatenate([sin, sin, zeros], axis=1)


def _rot_cols(w):
    half = MLA_ROPE // 2
    return jnp.concatenate([-w[..., half:], w[..., :half]], axis=-1)


def kernel(x, meta_tokens, hgrn_lb_logits, w_in, q_norm_gain, kv_norm_gain, w_uq, w_ukv, hgrn_norm_gain,
           w_proj_hgrn, w_proj_mla, w_out, ln1_gain, ln1_bias, peer_query, peer_sub_keys, peer_u, peer_v,
           ln2_gain, ln2_bias):
    batch, seq, d = x.shape
    p_rows = LEAD + seq
    t_real = batch * seq
    l = 0

    h_all = jnp.concatenate([jnp.zeros((batch, N_PAD, d), BF16),
                             jnp.broadcast_to(meta_tokens.astype(BF16)[None], (batch, N_META, d)),
                             x.astype(BF16)], axis=1).reshape(batch * p_rows, d)
    x2 = x.reshape(t_real, d)
    x_bf = x2.astype(BF16)

    lower_bounds = jnp.cumsum(jax.nn.softmax(hgrn_lb_logits.astype(F32), axis=0), axis=0)
    rows = jnp.arange(p_rows)
    cos_all, sin_all = _rope_tables(rows - N_PAD)

    w = w_in[l]
    o_hg = 4 * HG_WIDTH
    o_cq = o_hg + MLA_Q_RANK
    o_ckv = o_cq + MLA_KV_RANK
    o_kr = o_ckv + MLA_ROPE
    w_hg = w[:, :o_hg].astype(BF16)
    w_cq = w[:, o_hg:o_cq].astype(BF16)
    w_kr = w[:, o_ckv:o_kr]
    zpad = jnp.zeros((d, MLA_NOPE - MLA_ROPE), F32)
    w_kv = jnp.concatenate([w[:, o_cq:o_ckv], w_kr, zpad, _rot_cols(w_kr), zpad], axis=1).astype(BF16)
    w_g = w[:, o_kr:].astype(BF16)

    wq3 = w_uq[l].reshape(MLA_Q_RANK, MLA_HEADS, MLA_NOPE + MLA_ROPE)
    wq_rope = wq3[..., MLA_NOPE:]
    wq = jnp.concatenate([wq3[..., :MLA_NOPE], wq_rope, _rot_cols(wq_rope)], axis=-1)
    wq = wq.transpose(1, 0, 2).astype(BF16)
    wkv = w_ukv[l].reshape(MLA_KV_RANK, MLA_HEADS, MLA_NOPE + MLA_V).transpose(1, 0, 2).astype(BF16)

    tm_all = _pick(batch * p_rows, 1024)
    tm_real = _pick(t_real, 1024)
    hg = _matmul(h_all, w_hg, tm_all, 1024)
    kvp = _matmul(h_all, w_kv, tm_all, w_kv.shape[1])
    cq = _matmul(x_bf, w_cq, tm_real, MLA_Q_RANK)
    gates = _matmul(x_bf, w_g, tm_real, 1024)

    hg_o = _hgrn(hg, lower_bounds[l][None, :], hgrn_norm_gain[l][None, :], batch, p_rows)

    q = _qprep(cq, q_norm_gain[l][None, :], wq, cos_all[LEAD:], sin_all[LEAD:], batch, seq)
    k, v = _kvprep(kvp, kv_norm_gain[l][None, :], wkv, cos_all, sin_all, batch, p_rows)
    at_o = _flash(q, k, v, batch, seq, p_rows)

    h1, h1t = _mix(x2, hg_o.reshape(t_real, HG_WIDTH), at_o.reshape(t_real, MLA_HEADS * MLA_V), gates,
                   w_proj_hgrn[l].astype(BF16), w_proj_mla[l].astype(BF16), w_out[l].astype(BF16),
                   ln1_gain[l][None, :], ln1_bias[l][None, :])

    wqt = peer_query[l].T.astype(BF16)
    sk = peer_sub_keys[l].reshape(PEER_HEADS * 2, PEER_NKEYS, PEER_HALF).astype(BF16)
    r2, c1, e1, e2 = _route(h1t, wqt, sk)

    out = _experts(h1t, h1, peer_u[l].astype(BF16), peer_v[l].T.astype(BF16), r2, c1, e1, e2,
                   ln2_gain[l][None, :], ln2_bias[l][None, :])
    return out.reshape(batch, seq, d)
```

```python
import functools

import numpy as np
import jax
import jax.numpy as jnp
from jax import lax
from jax.experimental import pallas as pl
from jax.experimental.pallas import tpu as pltpu

F32 = jnp.float32
BF16 = jnp.bfloat16

D_MODEL = 1024
DEPTH = 1
N_META = 16
LEAD = 128
N_PAD = LEAD - N_META

HG_HEADS = 8
HG_D = 128
HG_WIDTH = HG_HEADS * HG_D
HG_SUB = 16

MLA_HEADS = 16
MLA_NOPE = 128
MLA_ROPE = 64
MLA_V = 128
MLA_Q_RANK = 384
MLA_KV_RANK = 256
MLA_Q_PAD = 512
MLA_QK = 256
MLA_SCALE = (MLA_NOPE + MLA_ROPE) ** -0.5
MLA_VW = 256
ROPE_BASE = 10000.0
LOG2_E = 1.4426950408889634

PEER_HEADS = 8
PEER_NKEYS = 128
PEER_N = PEER_NKEYS * PEER_NKEYS
PEER_HALF = 128
PEER_TOPK = 16

ALPHA = (2 * DEPTH) ** 0.25
EPS = 1e-5
NEG = -1e30

VMEM_LIMIT = 56 * 1024 * 1024
MXU_WIDTH = 256


def _cparams(sem):
    return pltpu.CompilerParams(dimension_semantics=sem, vmem_limit_bytes=VMEM_LIMIT)


def _pick(n, pref):
    t = min(n, pref)
    while n % t:
        t -= 128
    return t


def _mm_kernel(x_ref, w_ref, o_ref):
    x = x_ref[...].astype(w_ref.dtype)
    o_ref[...] = jnp.dot(x, w_ref[...], preferred_element_type=F32).astype(o_ref.dtype)


def _matmul(x, w, tm, tn):
    m, k = x.shape
    out_dtype = w.dtype
    n = w.shape[1]
    return pl.pallas_call(
        _mm_kernel,
        grid=(m // tm, n // tn),
        in_specs=[pl.BlockSpec((tm, k), lambda i, j: (i, 0)),
                  pl.BlockSpec((k, tn), lambda i, j: (0, j))],
        out_specs=pl.BlockSpec((tm, tn), lambda i, j: (i, j)),
        out_shape=jax.ShapeDtypeStruct((m, n), out_dtype),
        compiler_params=_cparams(("parallel", "parallel")),
        name="in_proj",
    )(x, w)


def _hgrn_chunk(q, k, lf, iv, st_ref, tri, masks, chunk):
    width = q.shape[1]
    hi = lf.astype(BF16)
    r1 = lf - hi.astype(F32)
    mid = r1.astype(BF16)
    lo = (r1 - mid.astype(F32)).astype(BF16)
    g = (jnp.dot(tri, hi, preferred_element_type=F32)
         + jnp.dot(tri, mid, preferred_element_type=F32)
         + jnp.dot(tri, lo, preferred_element_type=F32))

    def bcast_row(r, n):
        return jnp.broadcast_to(g[r:r + 1, :], (n, width))

    def prev_end(s):
        parts = [jnp.zeros((s, width), F32)] + [bcast_row(b * s - 1, s) for b in range(1, chunk // s)]
        return parts[0] if len(parts) == 1 else jnp.concatenate(parts, axis=0)

    def own_end(s):
        parts = [bcast_row((b + 1) * s - 1, s) for b in range(chunk // s)]
        return parts[0] if len(parts) == 1 else jnp.concatenate(parts, axis=0)

    q_c = (q * jnp.exp(g)).astype(BF16)
    k_c = (k * jnp.exp(own_end(chunk) - g)).astype(BF16)
    dec = jnp.exp(g[chunk - 1:chunk, :])
    p16 = prev_end(HG_SUB)
    q_lv = [(q * jnp.exp(g - p16)).astype(BF16)]
    k_lv = [(k * jnp.exp(p16 - g)).astype(BF16)]
    s = HG_SUB
    while s < chunk:
        q_lv.append(q_lv[0] if s == HG_SUB else (q * jnp.exp(g - prev_end(s))).astype(BF16))
        k_lv.append((k * jnp.exp(own_end(s) - g)).astype(BF16))
        s *= 2

    nt = (((1,), (1,)), ((), ()))
    tn = (((0,), (0,)), ((), ()))
    heads = range(HG_HEADS)
    hs = [slice(h * HG_D, (h + 1) * HG_D) for h in heads]
    st = [st_ref[h] for h in heads]
    o_inter = [lax.dot_general(q_c[:, hs[h]], st[h].astype(BF16), nt, preferred_element_type=F32) for h in heads]
    upd = [lax.dot_general(iv[:, hs[h]], k_c[:, hs[h]], tn, preferred_element_type=F32) for h in heads]
    scores = [[lax.dot_general(ql[:, hs[h]], kl[:, hs[h]], nt, preferred_element_type=F32) for h in heads]
              for ql, kl in zip(q_lv, k_lv)]
    outs, st_new = [], []
    for h in heads:
        a = jnp.where(masks[0], scores[0][h], 0.0)
        for lvl in range(1, len(masks)):
            a = jnp.where(masks[lvl], scores[lvl][h], a)
        outs.append(o_inter[h] + jnp.dot(a.astype(BF16), iv[:, hs[h]], preferred_element_type=F32))
        st_new.append(st[h] * dec[:, hs[h]] + upd[h])
    return outs, st_new


def _hgrn_kernel(q_ref, f_ref, i_ref, g_ref, lb_ref, gain_ref, tri_ref, o_ref, st_ref, *, chunk, rows):
    c = pl.program_id(1)

    @pl.when(c == 0)
    def _():
        st_ref[...] = jnp.zeros_like(st_ref)

    ri = lax.broadcasted_iota(jnp.int32, (chunk, chunk), 0)
    ci = lax.broadcasted_iota(jnp.int32, (chunk, chunk), 1)
    blk = lambda v, s: lax.shift_right_logical(v, s.bit_length() - 1)
    masks = [(blk(ri, HG_SUB) == blk(ci, HG_SUB)) & (ci <= ri)]
    s = HG_SUB
    while s < chunk:
        masks.append((blk(ri, 2 * s) == blk(ci, 2 * s)) & ((blk(ri, s) & 1) == 1) & ((blk(ci, s) & 1) == 0))
        s *= 2
    tri = tri_ref[...]
    lb = lb_ref[...]

    for n in range(rows // chunk):
        rs = slice(n * chunk, (n + 1) * chunk)
        seq_row = c * rows + n * chunk + lax.broadcasted_iota(jnp.int32, (chunk, HG_WIDTH), 0)
        valid = seq_row >= N_PAD
        f = lb + (1.0 - lb) * jax.nn.sigmoid(f_ref[rs, :].astype(F32))
        lf = jnp.where(valid, jnp.log(f), 0.0)
        k = jnp.where(valid, 1.0 - f, 0.0)
        outs, st_new = _hgrn_chunk(q_ref[rs, :].astype(F32), k, lf, i_ref[rs, :], st_ref, tri, masks, chunk)
        gate = g_ref[rs, :].astype(F32)
        gate = gate * jax.nn.sigmoid(gate)
        for h in range(HG_HEADS):
            hs = slice(h * HG_D, (h + 1) * HG_D)
            st_ref[h] = st_new[h]
            o = outs[h]
            ms = jnp.mean(o * o, axis=-1, keepdims=True)
            o = o * lax.rsqrt(ms + EPS) * gain_ref[:, hs]
            o_ref[0, rs, hs] = (o * gate[:, hs]).astype(o_ref.dtype)


def _hgrn(hg, lb, gain, batch, p_rows, chunk=128, rows=128):
    nc = p_rows // rows
    lead_blocks = LEAD // rows
    tri = jnp.asarray(np.tril(np.ones((chunk, chunk), np.float32)), BF16)
    col = lambda j: pl.BlockSpec((rows, HG_WIDTH), lambda b, c: (b * nc + c, j))
    vec = pl.BlockSpec((1, HG_WIDTH), lambda b, c: (0, 0))
    return pl.pallas_call(
        functools.partial(_hgrn_kernel, chunk=chunk, rows=rows),
        grid=(batch, nc),
        in_specs=[col(0), col(1), col(2), col(3), vec, vec,
                  pl.BlockSpec((chunk, chunk), lambda b, c: (0, 0))],
        out_specs=pl.BlockSpec((1, rows, HG_WIDTH), lambda b, c: (b, jnp.maximum(c - lead_blocks, 0), 0)),
        out_shape=jax.ShapeDtypeStruct((batch, p_rows - LEAD, HG_WIDTH), BF16),
        scratch_shapes=[pltpu.VMEM((HG_HEADS, HG_D, HG_D), F32)],
        compiler_params=_cparams(("parallel", "arbitrary")),
        name="hgrn2",
    )(hg, hg, hg, hg, lb, gain, tri)


def _rms(x, gain):
    ms = jnp.mean(x * x, axis=-1, keepdims=True)
    return x * lax.rsqrt(ms + EPS) * gain


def _qprep_kernel(cq_ref, gain_ref, w_ref, cos_ref, sin_ref, o_ref):
    xn = _rms(cq_ref[:, :MLA_Q_RANK].astype(F32), gain_ref[...]).astype(BF16)
    cos = cos_ref[...]
    sin = sin_ref[...]
    for h in range(MLA_HEADS):
        y = jnp.dot(xn, w_ref[h], preferred_element_type=F32)
        o_ref[0, h, :, :MLA_NOPE] = (y[:, :MLA_NOPE] * (MLA_SCALE * LOG2_E)).astype(o_ref.dtype)
        t = y[:, MLA_NOPE:]
        roped = t * cos + pltpu.roll(t, MLA_ROPE, axis=1) * sin
        o_ref[0, h, :, MLA_NOPE:] = (roped * (MLA_SCALE * LOG2_E)).astype(o_ref.dtype)


def _qprep(qg, gain, wq, cos, sin, batch, seq):
    tm = _pick(seq, 512)
    nb = seq // tm
    return pl.pallas_call(
        _qprep_kernel,
        grid=(batch, nb),
        in_specs=[pl.BlockSpec((tm, MLA_Q_PAD), lambda b, i: (b * nb + i, 2 * D_MODEL // MLA_Q_PAD)),
                  pl.BlockSpec((1, MLA_Q_RANK), lambda b, i: (0, 0)),
                  pl.BlockSpec((MLA_HEADS, MLA_Q_RANK, MLA_QK), lambda b, i: (0, 0, 0)),
                  pl.BlockSpec((tm, MLA_NOPE), lambda b, i: (i, 0)),
                  pl.BlockSpec((tm, MLA_NOPE), lambda b, i: (i, 0))],
        out_specs=pl.BlockSpec((1, MLA_HEADS, tm, MLA_QK), lambda b, i: (b, 0, i, 0)),
        out_shape=jax.ShapeDtypeStruct((batch, MLA_HEADS, seq, MLA_QK), BF16),
        compiler_params=_cparams(("parallel", "parallel")),
        name="mla_q",
    )(qg, gain, wq, cos, sin)


def _kvprep_kernel(kv_ref, gain_ref, w_ref, cos_ref, sin_ref, k_ref, v_ref):
    x = kv_ref[...]
    xn = _rms(x[:, :MLA_KV_RANK].astype(F32), gain_ref[...]).astype(BF16)
    kr = (x[:, MLA_KV_RANK:MLA_KV_RANK + MLA_NOPE].astype(F32) * cos_ref[...]
          + x[:, MLA_KV_RANK + MLA_NOPE:].astype(F32) * sin_ref[...]).astype(k_ref.dtype)
    ones_col = (lax.broadcasted_iota(jnp.int32, (x.shape[0], MLA_VW - MLA_V), 1) == 0).astype(v_ref.dtype)
    for h in range(MLA_HEADS):
        y = jnp.dot(xn, w_ref[h], preferred_element_type=F32)
        k_ref[0, h, :, :MLA_NOPE] = y[:, :MLA_NOPE].astype(k_ref.dtype)
        k_ref[0, h, :, MLA_NOPE:] = kr
        v_ref[0, h, :, :MLA_V] = y[:, MLA_NOPE:].astype(v_ref.dtype)
        v_ref[0, h, :, MLA_V:] = ones_col


def _kvprep(kvp, gain, wkv, cos, sin, batch, p_rows):
    tm = p_rows // 2
    return pl.pallas_call(
        _kvprep_kernel,
        grid=(batch, 2),
        in_specs=[pl.BlockSpec((tm, MLA_KV_RANK + 2 * MLA_NOPE), lambda b, i: (b * 2 + i, 0)),
                  pl.BlockSpec((1, MLA_KV_RANK), lambda b, i: (0, 0)),
                  pl.BlockSpec((MLA_HEADS, MLA_KV_RANK, MLA_NOPE + MLA_V), lambda b, i: (0, 0, 0)),
                  pl.BlockSpec((tm, MLA_NOPE), lambda b, i: (i, 0)),
                  pl.BlockSpec((tm, MLA_NOPE), lambda b, i: (i, 0))],
        out_specs=[pl.BlockSpec((1, MLA_HEADS, tm, MLA_QK), lambda b, i: (b, 0, i, 0)),
                   pl.BlockSpec((1, MLA_HEADS, tm, MLA_VW), lambda b, i: (b, 0, i, 0))],
        out_shape=[jax.ShapeDtypeStruct((batch, MLA_HEADS, p_rows, MLA_QK), BF16),
                   jax.ShapeDtypeStruct((batch, MLA_HEADS, p_rows, MLA_VW), BF16)],
        compiler_params=_cparams(("parallel", "parallel")),
        name="mla_kv",
    )(kvp, gain, wkv, cos, sin)


def _flash_segments(seq, tq, tk):
    segs = []
    for qi in range(seq // tq):
        pos = 0
        while pos < qi * tq:
            n = min(tk, qi * tq - pos)
            segs.append((qi, False, pos, n))
            pos += n
        segs.append((qi, True, qi * tq, tq))
    return segs


def _flash_kernel(q_ref, k_ref, v_ref, o_ref, *, tq, tk):
    seq = q_ref.shape[2]
    nt = (((1,), (1,)), ((), ()))
    causal = (lax.broadcasted_iota(jnp.int32, (tq, tq), 1) <= lax.broadcasted_iota(jnp.int32, (tq, tq), 0))
    lead_mask = lax.broadcasted_iota(jnp.int32, (tq, LEAD), 1) >= N_PAD

    def scores(seg):
        qi, diag, k0, n = seg
        q = q_ref[0, 0, qi * tq:(qi + 1) * tq, :]
        s = lax.dot_general(q, k_ref[0, 0, LEAD + k0:LEAD + k0 + n, :], nt, preferred_element_type=F32)
        if diag:
            s_lead = lax.dot_general(q, k_ref[0, 0, 0:LEAD, :], nt, preferred_element_type=F32)
            s = jnp.concatenate([jnp.where(causal, s, NEG), jnp.where(lead_mask, s_lead, NEG)], axis=1)
        return s

    segs = _flash_segments(seq, tq, tk)
    s_cur = scores(segs[0])
    m = acc = None
    for i, (qi, diag, k0, n) in enumerate(segs):
        s_next = scores(segs[i + 1]) if i + 1 < len(segs) else None
        first = k0 == 0
        m_new = jnp.max(s_cur, axis=-1, keepdims=True)
        if not first:
            m_new = jnp.maximum(m, m_new)
        p = jnp.exp2(s_cur - m_new).astype(BF16)
        if diag:
            pv = (jnp.dot(p[:, :tq], v_ref[0, 0, LEAD + k0:LEAD + k0 + n, :], preferred_element_type=F32)
                  + jnp.dot(p[:, tq:], v_ref[0, 0, 0:LEAD, :], preferred_element_type=F32))
        else:
            pv = jnp.dot(p, v_ref[0, 0, LEAD + k0:LEAD + k0 + n, :], preferred_element_type=F32)
        acc = pv if first else jnp.exp2(m - m_new) * acc + pv
        m = m_new
        if diag:
            o_ref[0, qi * tq:(qi + 1) * tq, :] = (acc[:, :MLA_V] / acc[:, MLA_V:MLA_V + 1]).astype(o_ref.dtype)
        s_cur = s_next


def _flash(q, k, v, batch, seq, p_rows, tq=256, tk=512):
    tq = min(tq, seq)
    return pl.pallas_call(
        functools.partial(_flash_kernel, tq=tq, tk=tk),
        grid=(batch, MLA_HEADS),
        in_specs=[pl.BlockSpec((1, 1, seq, MLA_QK), lambda b, h: (b, h, 0, 0)),
                  pl.BlockSpec((1, 1, p_rows, MLA_QK), lambda b, h: (b, h, 0, 0)),
                  pl.BlockSpec((1, 1, p_rows, MLA_VW), lambda b, h: (b, h, 0, 0))],
        out_specs=pl.BlockSpec((1, seq, MLA_V), lambda b, h: (b, 0, h)),
        out_shape=jax.ShapeDtypeStruct((batch, seq, MLA_HEADS * MLA_V), BF16),
        compiler_params=_cparams(("parallel", "parallel")),
        name="mla_flash",
    )(q, k, v)


def _layer_norm(z, gain, bias):
    mu = jnp.mean(z, axis=-1, keepdims=True)
    zc = z - mu
    var = jnp.mean(zc * zc, axis=-1, keepdims=True)
    return zc * lax.rsqrt(var + EPS) * gain + bias


def _mix_kernel(x_ref, hg_ref, at_ref, ga_ref, gb_ref, wph_ref, wpm_ref, wo_ref, gain_ref, bias_ref,
                h_ref, ht_ref):
    ya = jnp.dot(hg_ref[...], wph_ref[...], preferred_element_type=F32)
    yb = jnp.dot(at_ref[...], wpm_ref[...], preferred_element_type=F32)
    mix = (jax.nn.sigmoid(ga_ref[...].astype(F32)) * ya + jax.nn.sigmoid(gb_ref[...].astype(F32)) * yb)
    mixed = jnp.dot(mix.astype(BF16), wo_ref[...], preferred_element_type=F32)
    h = _layer_norm(ALPHA * x_ref[...] + mixed, gain_ref[...], bias_ref[...])
    h_ref[...] = h
    ht_ref[...] = h.T.astype(ht_ref.dtype)


def _mix(x2, hg_o, at_o, gates, wph, wpm, wo, gain, bias):
    t = x2.shape[0]
    tm = _pick(t, 512)
    full = lambda a: pl.BlockSpec(a.shape, lambda i: (0,) * a.ndim)
    return pl.pallas_call(
        _mix_kernel,
        grid=(t // tm,),
        in_specs=[pl.BlockSpec((tm, D_MODEL), lambda i: (i, 0)),
                  pl.BlockSpec((tm, HG_WIDTH), lambda i: (i, 0)),
                  pl.BlockSpec((tm, MLA_HEADS * MLA_V), lambda i: (i, 0)),
                  pl.BlockSpec((tm, D_MODEL), lambda i: (i, 0)),
                  pl.BlockSpec((tm, D_MODEL), lambda i: (i, 1)),
                  full(wph), full(wpm), full(wo), full(gain), full(bias)],
        out_specs=[pl.BlockSpec((tm, D_MODEL), lambda i: (i, 0)),
                   pl.BlockSpec((D_MODEL, tm), lambda i: (0, i))],
        out_shape=[jax.ShapeDtypeStruct((t, D_MODEL), F32),
                   jax.ShapeDtypeStruct((D_MODEL, t), BF16)],
        compiler_params=_cparams(("parallel",)),
        name="mix_ln1",
    )(x2, hg_o, at_o, gates, gates, wph, wpm, wo, gain, bias)


N_TOP = PEER_TOPK + 1
SUBLANES = 8
TOP_ROWS = 24
PAIR_LIMIT = [N_TOP // (a + 1) for a in range(N_TOP)]
N_WIDE = sum(1 for n in PAIR_LIMIT if n > 1)
ROUTE_GROUP = 2


def _sort_network(n):
    def merge(lo, hi, r):
        step = r * 2
        if step < hi - lo:
            yield from merge(lo, hi, step)
            yield from merge(lo + r, hi, step)
            yield from [(i, i + r) for i in range(lo + r, hi - r, step)]
        else:
            yield (lo, lo + r)

    def sort(lo, hi):
        if hi - lo >= 1:
            mid = lo + (hi - lo) // 2
            yield from sort(lo, mid)
            yield from sort(mid + 1, hi)
            yield from merge(lo, hi, 1)

    return list(sort(0, n - 1))


def _sorted_slabs(slabs):
    n = 1 << (len(slabs) - 1).bit_length()
    v = list(slabs) + [None] * (n - len(slabs))
    for i, j in _sort_network(n):
        if v[j] is None:
            continue
        if v[i] is None:
            v[i], v[j] = v[j], None
        else:
            v[i], v[j] = jnp.maximum(v[i], v[j]), jnp.minimum(v[i], v[j])
    return v[:len(slabs)]


def _top_rows(slab_sets, n):
    sets = [list(c) for c in slab_sets]
    vals = [[] for _ in sets]
    for r in range(n):
        for i, cols in enumerate(sets):
            m = jnp.max(cols[0], axis=0, keepdims=True)
            vals[i].append(m)
            hit = cols[0] == m
            keep = min(len(cols), n - r - 1)
            sets[i] = [jnp.where(hit, cols[k + 1] if k + 1 < len(cols) else -jnp.inf, cols[k])
                       for k in range(keep)]
    return vals


def _count_above(rows, thr):
    count = jnp.zeros_like(thr)
    passed = []
    step = PEER_TOPK // 2
    while step >= 1:
        cands = [rows[base + step - 1] for base in range(0, PEER_TOPK, 2 * step)]
        for m in reversed(passed):
            cands = [jnp.where(m, cands[2 * i + 1], cands[2 * i]) for i in range(len(cands) // 2)]
        m = cands[0] > thr
        count = jnp.where(m, count + float(step), count)
        passed.append(m)
        step //= 2
    return jnp.where(rows[PEER_TOPK - 1] > thr, float(PEER_TOPK), count)


def _stack_rows(rows, tokens):
    ri = lax.broadcasted_iota(jnp.int32, (TOP_ROWS, tokens), 0)
    out = jnp.full((TOP_ROWS, tokens), -jnp.inf, F32)
    for r, v in enumerate(rows):
        out = jnp.where(ri == r, v, out)
    return out


def _route_kernel(ht_ref, wq_ref, sk_ref, r2_ref, c1_ref, e1_ref, e2_ref):
    tokens = ht_ref.shape[1]
    ri8 = lax.broadcasted_iota(jnp.int32, (N_WIDE, tokens), 0)
    qp = jnp.dot(wq_ref[...], ht_ref[...], preferred_element_type=F32).astype(BF16)
    for h0 in range(0, PEER_HEADS, ROUTE_GROUP):
        heads = range(h0, h0 + ROUTE_GROUP)
        s = [jnp.dot(sk_ref[hp], qp[hp * PEER_HALF:(hp + 1) * PEER_HALF], preferred_element_type=F32)
             for hp in range(2 * h0, 2 * (h0 + ROUTE_GROUP))]
        slabs = lambda x: [x[k:k + SUBLANES] for k in range(0, x.shape[0], SUBLANES)]
        tops = _top_rows([_sorted_slabs(slabs(x)) for x in s], N_TOP)
        cand = []
        for g in range(ROUTE_GROUP):
            t1, t2 = tops[2 * g], tops[2 * g + 1]
            t1s = _stack_rows(t1, tokens)
            t2s = _stack_rows(t2, tokens)
            parts = [jnp.where(ri8 < min(PAIR_LIMIT[a], N_WIDE), t1[a] + t2s[:N_WIDE], -jnp.inf)
                     for a in range(N_WIDE)]
            parts += slabs(t1[0] + t2s[N_WIDE:]) + slabs(t1s[N_WIDE:] + t2[0])
            cand.append(_sorted_slabs(parts))
        bests = _top_rows(cand, N_TOP)
        for g, h in enumerate(heads):
            s1, s2 = s[2 * g], s[2 * g + 1]
            t1, t2, best = tops[2 * g], tops[2 * g + 1], bests[g]
            m = best[0]
            z = jnp.zeros_like(m)
            for r in range(PEER_TOPK):
                z = z + jnp.exp(best[r] - m)
            thr = 0.5 * (best[PEER_TOPK - 1] + best[PEER_TOPK]) - s1
            c1 = _count_above(t2, thr)
            r2_ref[h] = _count_above(t2, s2).astype(r2_ref.dtype)
            c1_ref[h] = c1
            e1_ref[h] = jnp.exp(s1 - t1[0] - jnp.log(z))
            e2_ref[h] = jnp.exp(s2 - t2[0]).astype(e2_ref.dtype)


def _route(ht, wqt, sk):
    t = ht.shape[1]
    tm = _pick(t, 256)
    out = jax.ShapeDtypeStruct((PEER_HEADS, PEER_NKEYS, t), ht.dtype)
    out32 = jax.ShapeDtypeStruct((PEER_HEADS, PEER_NKEYS, t), F32)
    ospec = pl.BlockSpec((PEER_HEADS, PEER_NKEYS, tm), lambda i: (0, 0, i))
    return pl.pallas_call(
        _route_kernel,
        grid=(t // tm,),
        in_specs=[pl.BlockSpec((D_MODEL, tm), lambda i: (0, i)),
                  pl.BlockSpec(wqt.shape, lambda i: (0, 0)),
                  pl.BlockSpec(sk.shape, lambda i: (0, 0, 0))],
        out_specs=[ospec] * 4,
        out_shape=[out, out32, out32, out],
        compiler_params=_cparams(("parallel",)),
        name="peer_route",
    )(ht, wqt, sk)


GELU_C0 = 0.7978845608028654
GELU_C1 = GELU_C0 * 0.044715


def _gelu(a):
    t = jnp.exp2(a * ((-2.0 * LOG2_E * GELU_C0) + (-2.0 * LOG2_E * GELU_C1) * (a * a)))
    return a / (1.0 + t)


def _expert_kernel(ht_ref, h_ref, u_ref, vt_ref, r2_ref, c1_ref, e1_ref, e2_ref, gain_ref, bias_ref,
                   o_ref, acc_ref, *act_refs, chunk):
    e = pl.program_id(1)
    te, width = act_refs[0].shape
    dtype = act_refs[0].dtype
    packed_rows = 16

    @pl.when(e == 0)
    def _():
        acc_ref[...] = jnp.zeros_like(acc_ref)

    def row_bcast(ref, h, r, ls):
        row = jnp.broadcast_to(ref[h, r:r + 1, ls], (packed_rows, width)).astype(dtype)
        return pltpu.repeat(row, PEER_NKEYS // packed_rows, axis=0)

    lanes = [slice(i * width, (i + 1) * width) for i in range(len(act_refs))]
    a_all = [[jnp.dot(u_ref[c * chunk:(c + 1) * chunk, :], ht_ref[:, ls], preferred_element_type=F32)
              for c in range(te // chunk)] for ls in lanes]
    for act_ref, ls, a_chunks in zip(act_refs, lanes, a_all):
        for c, a in enumerate(a_chunks):
            for rr in range(chunk // PEER_NKEYS):
                r = c * (chunk // PEER_NKEYS) + rr
                g = None
                for h in range(PEER_HEADS):
                    gate = jnp.where(r2_ref[h, :, ls] < row_bcast(c1_ref, h, r, ls),
                                     e2_ref[h, :, ls] * row_bcast(e1_ref, h, r, ls), 0.0)
                    g = gate if g is None else g + gate
                a_r = a[rr * PEER_NKEYS:(rr + 1) * PEER_NKEYS, :]
                act_ref[r * PEER_NKEYS:(r + 1) * PEER_NKEYS, :] = _gelu(a_r.astype(dtype)) * g
        acc_ref[:, ls] += jnp.dot(vt_ref[...], act_ref[...], preferred_element_type=F32)

    @pl.when(e == pl.num_programs(1) - 1)
    def _():
        z = ALPHA * h_ref[...] + acc_ref[...].T
        o_ref[...] = _layer_norm(z, gain_ref[...], bias_ref[...])


def _experts(ht, h1, u, vt, r2, c1, e1, e2, gain, bias, tm=512, te=2048, chunk=512):
    t = h1.shape[0]
    tm = _pick(t, tm)
    hspec = pl.BlockSpec((PEER_HEADS, PEER_NKEYS, tm), lambda i, e: (0, 0, i))
    rspec = pl.BlockSpec((PEER_HEADS, te // PEER_NKEYS, tm), lambda i, e: (0, e, i))
    vec = pl.BlockSpec((1, D_MODEL), lambda i, e: (0, 0))
    return pl.pallas_call(
        functools.partial(_expert_kernel, chunk=chunk),
        grid=(t // tm, PEER_N // te),
        in_specs=[pl.BlockSpec((D_MODEL, tm), lambda i, e: (0, i)),
                  pl.BlockSpec((tm, D_MODEL), lambda i, e: (i, 0)),
                  pl.BlockSpec((te, D_MODEL), lambda i, e: (e, 0)),
                  pl.BlockSpec((D_MODEL, te), lambda i, e: (0, e)),
                  hspec, rspec, rspec, hspec, vec, vec],
        out_specs=pl.BlockSpec((tm, D_MODEL), lambda i, e: (i, 0)),
        out_shape=jax.ShapeDtypeStruct((t, D_MODEL), F32),
        scratch_shapes=[pltpu.VMEM((D_MODEL, tm), F32)] + [pltpu.VMEM((te, MXU_WIDTH), ht.dtype)] * (tm // MXU_WIDTH),
        compiler_params=_cparams(("parallel", "arbitrary")),
        name="peer_experts",
    )(ht, h1, u, vt, r2, c1, e1, e2, gain, bias)


def _rope_tables(pos):
    half = MLA_ROPE // 2
    inv_freq = ROPE_BASE ** (-jnp.arange(half, dtype=F32) / half)
    ang = pos.astype(F32)[:, None] * inv_freq[None, :]
    zeros = jnp.zeros((pos.shape[0], MLA_NOPE - MLA_ROPE), F32)
    cos, sin = jnp.cos(ang), jnp.sin(ang)
    return jnp.concatenate([cos, cos, zeros], axis=1), jnp.concatenate([sin, sin, zeros], axis=1)


def _rot_cols(w):
    half = MLA_ROPE // 2
    return jnp.concatenate([-w[..., half:], w[..., :half]], axis=-1)


def kernel(x, meta_tokens, hgrn_lb_logits, w_in, q_norm_gain, kv_norm_gain, w_uq, w_ukv, hgrn_norm_gain,
           w_proj_hgrn, w_proj_mla, w_out, ln1_gain, ln1_bias, peer_query, peer_sub_keys, peer_u, peer_v,
           ln2_gain, ln2_bias):
    batch, seq, d = x.shape
    p_rows = LEAD + seq
    t_real = batch * seq
    l = 0

    h_all = jnp.concatenate([jnp.zeros((batch, N_PAD, d), BF16),
                             jnp.broadcast_to(meta_tokens.astype(BF16)[None], (batch, N_META, d)),
                             x.astype(BF16)], axis=1).reshape(batch * p_rows, d)
    x2 = x.reshape(t_real, d)

    lower_bounds = jnp.cumsum(jax.nn.softmax(hgrn_lb_logits.astype(F32), axis=0), axis=0)
    rows = jnp.arange(p_rows)
    cos_all, sin_all = _rope_tables(rows - N_PAD)

    w = w_in[l]
    o_hg = 4 * HG_WIDTH
    o_cq = o_hg + MLA_Q_RANK
    o_ckv = o_cq + MLA_KV_RANK
    o_kr = o_ckv + MLA_ROPE
    w_hg = w[:, :o_hg].astype(BF16)
    w_kr = w[:, o_ckv:o_kr]
    zpad = jnp.zeros((d, MLA_NOPE - MLA_ROPE), F32)
    w_kv = jnp.concatenate([w[:, o_cq:o_ckv], w_kr, zpad, _rot_cols(w_kr), zpad], axis=1).astype(BF16)
    w_qg = jnp.concatenate([w[:, o_kr:], w[:, o_hg:o_cq], jnp.zeros((d, MLA_Q_PAD - MLA_Q_RANK), F32)],
                           axis=1).astype(BF16)

    wq3 = w_uq[l].reshape(MLA_Q_RANK, MLA_HEADS, MLA_NOPE + MLA_ROPE)
    wq_rope = wq3[..., MLA_NOPE:]
    wq = jnp.concatenate([wq3[..., :MLA_NOPE], wq_rope, _rot_cols(wq_rope)], axis=-1)
    wq = wq.transpose(1, 0, 2).astype(BF16)
    wkv = w_ukv[l].reshape(MLA_KV_RANK, MLA_HEADS, MLA_NOPE + MLA_V).transpose(1, 0, 2).astype(BF16)

    tm_all = _pick(batch * p_rows, 1024)
    tm_real = _pick(t_real, 1024)
    hg = _matmul(h_all, w_hg, tm_all, 1024)
    kvp = _matmul(h_all, w_kv, tm_all, w_kv.shape[1])
    qg = _matmul(x2, w_qg, tm_real, w_qg.shape[1])

    hg_o = _hgrn(hg, lower_bounds[l][None, :], hgrn_norm_gain[l][None, :], batch, p_rows)

    q = _qprep(qg, q_norm_gain[l][None, :], wq, cos_all[LEAD:], sin_all[LEAD:], batch, seq)
    k, v = _kvprep(kvp, kv_norm_gain[l][None, :], wkv, cos_all, sin_all, batch, p_rows)
    at_o = _flash(q, k, v, batch, seq, p_rows)

    h1, h1t = _mix(x2, hg_o.reshape(t_real, HG_WIDTH), at_o.reshape(t_real, MLA_HEADS * MLA_V), qg,
                   w_proj_hgrn[l].astype(BF16), w_proj_mla[l].astype(BF16), w_out[l].astype(BF16),
                   ln1_gain[l][None, :], ln1_bias[l][None, :])

    wqt = peer_query[l].T.astype(BF16)
    sk = peer_sub_keys[l].reshape(PEER_HEADS * 2, PEER_NKEYS, PEER_HALF).astype(BF16)
    r2, c1, e1, e2 = _route(h1t, wqt, sk)

    out = _experts(h1t, h1, peer_u[l].astype(BF16), peer_v[l].T.astype(BF16), r2, c1, e1, e2,
                   ln2_gain[l][None, :], ln2_bias[l][None, :])
    return out.reshape(batch, seq, d)
```

```python
import functools

import numpy as np
import jax
import jax.numpy as jnp
from jax import lax
from jax.experimental import pallas as pl
from jax.experimental.pallas import tpu as pltpu

F32 = jnp.float32
BF16 = jnp.bfloat16

D_MODEL = 1024
DEPTH = 1
N_META = 16
LEAD = 128
N_PAD = LEAD - N_META

HG_HEADS = 8
HG_D = 128
HG_WIDTH = HG_HEADS * HG_D
HG_SUB = 16

MLA_HEADS = 16
MLA_NOPE = 128
MLA_ROPE = 64
MLA_V = 128
MLA_Q_RANK = 384
MLA_KV_RANK = 256
MLA_Q_PAD = 512
COL_KV = 4 * 1024 // 512
COL_CQ = COL_KV + 1
COL_GA = 5
COL_GB = 6
MLA_QK = 256
MLA_SCALE = (MLA_NOPE + MLA_ROPE) ** -0.5
MLA_VW = 256
ROPE_BASE = 10000.0
LOG2_E = 1.4426950408889634

PEER_HEADS = 8
PEER_NKEYS = 128
PEER_N = PEER_NKEYS * PEER_NKEYS
PEER_HALF = 128
PEER_TOPK = 16

ALPHA = (2 * DEPTH) ** 0.25
EPS = 1e-5
NEG = -1e30

VMEM_LIMIT = 56 * 1024 * 1024
MXU_WIDTH = 256


def _cparams(sem):
    return pltpu.CompilerParams(dimension_semantics=sem, vmem_limit_bytes=VMEM_LIMIT)


def _pick(n, pref):
    t = min(n, pref)
    while n % t:
        t -= 128
    return t


def _mm_kernel(x_ref, w_ref, o_ref):
    x = x_ref[...].astype(w_ref.dtype)
    o_ref[...] = jnp.dot(x, w_ref[...], preferred_element_type=F32).astype(o_ref.dtype)


def _matmul(x, w, tm, tn):
    m, k = x.shape
    out_dtype = w.dtype
    n = w.shape[1]
    return pl.pallas_call(
        _mm_kernel,
        grid=(m // tm, n // tn),
        in_specs=[pl.BlockSpec((tm, k), lambda i, j: (i, 0)),
                  pl.BlockSpec((k, tn), lambda i, j: (0, j))],
        out_specs=pl.BlockSpec((tm, tn), lambda i, j: (i, j)),
        out_shape=jax.ShapeDtypeStruct((m, n), out_dtype),
        compiler_params=_cparams(("parallel", "parallel")),
        name="in_proj",
    )(x, w)


def _hgrn_chunk(q, k, lf, iv, st_ref, tri, masks, chunk):
    width = q.shape[1]
    hi = lf.astype(BF16)
    r1 = lf - hi.astype(F32)
    mid = r1.astype(BF16)
    lo = (r1 - mid.astype(F32)).astype(BF16)
    g = (jnp.dot(tri, hi, preferred_element_type=F32)
         + jnp.dot(tri, mid, preferred_element_type=F32)
         + jnp.dot(tri, lo, preferred_element_type=F32))

    def bcast_row(r, n):
        return jnp.broadcast_to(g[r:r + 1, :], (n, width))

    def prev_end(s):
        parts = [jnp.zeros((s, width), F32)] + [bcast_row(b * s - 1, s) for b in range(1, chunk // s)]
        return parts[0] if len(parts) == 1 else jnp.concatenate(parts, axis=0)

    def own_end(s):
        parts = [bcast_row((b + 1) * s - 1, s) for b in range(chunk // s)]
        return parts[0] if len(parts) == 1 else jnp.concatenate(parts, axis=0)

    q_c = (q * jnp.exp(g)).astype(BF16)
    k_c = (k * jnp.exp(own_end(chunk) - g)).astype(BF16)
    dec = jnp.exp(g[chunk - 1:chunk, :])
    p16 = prev_end(HG_SUB)
    q_lv = [(q * jnp.exp(g - p16)).astype(BF16)]
    k_lv = [(k * jnp.exp(p16 - g)).astype(BF16)]
    s = HG_SUB
    while s < chunk:
        q_lv.append(q_lv[0] if s == HG_SUB else (q * jnp.exp(g - prev_end(s))).astype(BF16))
        k_lv.append((k * jnp.exp(own_end(s) - g)).astype(BF16))
        s *= 2

    nt = (((1,), (1,)), ((), ()))
    tn = (((0,), (0,)), ((), ()))
    heads = range(HG_HEADS)
    hs = [slice(h * HG_D, (h + 1) * HG_D) for h in heads]
    st = [st_ref[h] for h in heads]
    o_inter = [lax.dot_general(q_c[:, hs[h]], st[h].astype(BF16), nt, preferred_element_type=F32) for h in heads]
    upd = [lax.dot_general(iv[:, hs[h]], k_c[:, hs[h]], tn, preferred_element_type=F32) for h in heads]
    scores = [[lax.dot_general(ql[:, hs[h]], kl[:, hs[h]], nt, preferred_element_type=F32) for h in heads]
              for ql, kl in zip(q_lv, k_lv)]
    outs, st_new = [], []
    for h in heads:
        a = jnp.where(masks[0], scores[0][h], 0.0)
        for lvl in range(1, len(masks)):
            a = jnp.where(masks[lvl], scores[lvl][h], a)
        outs.append(o_inter[h] + jnp.dot(a.astype(BF16), iv[:, hs[h]], preferred_element_type=F32))
        st_new.append(st[h] * dec[:, hs[h]] + upd[h])
    return outs, st_new


def _hgrn_kernel(q_ref, f_ref, i_ref, g_ref, ql_ref, fl_ref, il_ref, gl_ref, lb_ref, gain_ref, tri_ref,
                 o_ref, st_ref, *, chunk, rows):
    c = pl.program_id(1)
    lead = c == 0
    pick = lambda real_ref, lead_ref, rs: jnp.where(lead, lead_ref[rs, :], real_ref[rs, :])

    @pl.when(c == 0)
    def _():
        st_ref[...] = jnp.zeros_like(st_ref)

    ri = lax.broadcasted_iota(jnp.int32, (chunk, chunk), 0)
    ci = lax.broadcasted_iota(jnp.int32, (chunk, chunk), 1)
    blk = lambda v, s: lax.shift_right_logical(v, s.bit_length() - 1)
    masks = [(blk(ri, HG_SUB) == blk(ci, HG_SUB)) & (ci <= ri)]
    s = HG_SUB
    while s < chunk:
        masks.append((blk(ri, 2 * s) == blk(ci, 2 * s)) & ((blk(ri, s) & 1) == 1) & ((blk(ci, s) & 1) == 0))
        s *= 2
    tri = tri_ref[...]
    lb = lb_ref[...]

    for n in range(rows // chunk):
        rs = slice(n * chunk, (n + 1) * chunk)
        seq_row = c * rows + n * chunk + lax.broadcasted_iota(jnp.int32, (chunk, HG_WIDTH), 0)
        valid = seq_row >= N_PAD
        f = lb + (1.0 - lb) * jax.nn.sigmoid(pick(f_ref, fl_ref, rs).astype(F32))
        lf = jnp.where(valid, jnp.log(f), 0.0)
        k = jnp.where(valid, 1.0 - f, 0.0)
        outs, st_new = _hgrn_chunk(pick(q_ref, ql_ref, rs).astype(F32), k, lf, pick(i_ref, il_ref, rs), st_ref, tri,
                                   masks, chunk)
        gate = pick(g_ref, gl_ref, rs).astype(F32)
        gate = gate * jax.nn.sigmoid(gate)
        for h in range(HG_HEADS):
            hs = slice(h * HG_D, (h + 1) * HG_D)
            st_ref[h] = st_new[h]
            o = outs[h]
            ms = jnp.mean(o * o, axis=-1, keepdims=True)
            o = o * lax.rsqrt(ms + EPS) * gain_ref[:, hs]
            o_ref[0, rs, hs] = (o * gate[:, hs]).astype(o_ref.dtype)


def _hgrn(proj, proj_lead, lb, gain, batch, seq, chunk=128, rows=LEAD):
    ns = seq // rows
    tri = jnp.asarray(np.tril(np.ones((chunk, chunk), np.float32)), BF16)
    real = lambda j: pl.BlockSpec((rows, HG_WIDTH), lambda b, c: (b * ns + jnp.maximum(c - 1, 0), j))
    lead = lambda j: pl.BlockSpec((rows, HG_WIDTH), lambda b, c: (0, j))
    vec = pl.BlockSpec((1, HG_WIDTH), lambda b, c: (0, 0))
    return pl.pallas_call(
        functools.partial(_hgrn_kernel, chunk=chunk, rows=rows),
        grid=(batch, ns + 1),
        in_specs=[real(0), real(1), real(2), real(3), lead(0), lead(1), lead(2), lead(3), vec, vec,
                  pl.BlockSpec((chunk, chunk), lambda b, c: (0, 0))],
        out_specs=pl.BlockSpec((1, rows, HG_WIDTH), lambda b, c: (b, jnp.maximum(c - 1, 0), 0)),
        out_shape=jax.ShapeDtypeStruct((batch, seq, HG_WIDTH), BF16),
        scratch_shapes=[pltpu.VMEM((HG_HEADS, HG_D, HG_D), F32)],
        compiler_params=_cparams(("parallel", "arbitrary")),
        name="hgrn2",
    )(proj, proj, proj, proj, proj_lead, proj_lead, proj_lead, proj_lead, lb, gain, tri)


def _rms(x, gain):
    ms = jnp.mean(x * x, axis=-1, keepdims=True)
    return x * lax.rsqrt(ms + EPS) * gain


def _qprep_kernel(cq_ref, gain_ref, w_ref, cos_ref, sin_ref, o_ref):
    xn = _rms(cq_ref[:, :MLA_Q_RANK].astype(F32), gain_ref[...]).astype(BF16)
    cos = cos_ref[...]
    sin = sin_ref[...]
    for h in range(MLA_HEADS):
        y = jnp.dot(xn, w_ref[h], preferred_element_type=F32)
        o_ref[0, h, :, :MLA_NOPE] = (y[:, :MLA_NOPE] * (MLA_SCALE * LOG2_E)).astype(o_ref.dtype)
        t = y[:, MLA_NOPE:]
        roped = t * cos + pltpu.roll(t, MLA_ROPE, axis=1) * sin
        o_ref[0, h, :, MLA_NOPE:] = (roped * (MLA_SCALE * LOG2_E)).astype(o_ref.dtype)


def _qprep(qg, gain, wq, cos, sin, batch, seq):
    tm = _pick(seq, 512)
    nb = seq // tm
    return pl.pallas_call(
        _qprep_kernel,
        grid=(batch, nb),
        in_specs=[pl.BlockSpec((tm, MLA_Q_PAD), lambda b, i: (b * nb + i, COL_CQ)),
                  pl.BlockSpec((1, MLA_Q_RANK), lambda b, i: (0, 0)),
                  pl.BlockSpec((MLA_HEADS, MLA_Q_RANK, MLA_QK), lambda b, i: (0, 0, 0)),
                  pl.BlockSpec((tm, MLA_NOPE), lambda b, i: (i, 0)),
                  pl.BlockSpec((tm, MLA_NOPE), lambda b, i: (i, 0))],
        out_specs=pl.BlockSpec((1, MLA_HEADS, tm, MLA_QK), lambda b, i: (b, 0, i, 0)),
        out_shape=jax.ShapeDtypeStruct((batch, MLA_HEADS, seq, MLA_QK), BF16),
        compiler_params=_cparams(("parallel", "parallel")),
        name="mla_q",
    )(qg, gain, wq, cos, sin)


def _kvprep_kernel(kv_ref, gain_ref, w_ref, cos_ref, sin_ref, k_ref, v_ref):
    x = kv_ref[...]
    xn = _rms(x[:, :MLA_KV_RANK].astype(F32), gain_ref[...]).astype(BF16)
    kr = (x[:, MLA_KV_RANK:MLA_KV_RANK + MLA_NOPE].astype(F32) * cos_ref[...]
          + x[:, MLA_KV_RANK + MLA_NOPE:].astype(F32) * sin_ref[...]).astype(k_ref.dtype)
    ones_col = (lax.broadcasted_iota(jnp.int32, (x.shape[0], MLA_VW - MLA_V), 1) == 0).astype(v_ref.dtype)
    for h in range(MLA_HEADS):
        y = jnp.dot(xn, w_ref[h], preferred_element_type=F32)
        k_ref[0, h, :, :MLA_NOPE] = y[:, :MLA_NOPE].astype(k_ref.dtype)
        k_ref[0, h, :, MLA_NOPE:] = kr
        v_ref[0, h, :, :MLA_V] = y[:, MLA_NOPE:].astype(v_ref.dtype)
        v_ref[0, h, :, MLA_V:] = ones_col


def _kvprep(proj, gain, wkv, cos, sin, batch, rows):
    tm = _pick(rows, 1024)
    nb = rows // tm
    return pl.pallas_call(
        _kvprep_kernel,
        grid=(batch, nb),
        in_specs=[pl.BlockSpec((tm, MLA_KV_RANK + 2 * MLA_NOPE), lambda b, i: (b * nb + i, COL_KV)),
                  pl.BlockSpec((1, MLA_KV_RANK), lambda b, i: (0, 0)),
                  pl.BlockSpec((MLA_HEADS, MLA_KV_RANK, MLA_NOPE + MLA_V), lambda b, i: (0, 0, 0)),
                  pl.BlockSpec((tm, MLA_NOPE), lambda b, i: (i, 0)),
                  pl.BlockSpec((tm, MLA_NOPE), lambda b, i: (i, 0))],
        out_specs=[pl.BlockSpec((1, MLA_HEADS, tm, MLA_QK), lambda b, i: (b, 0, i, 0)),
                   pl.BlockSpec((1, MLA_HEADS, tm, MLA_VW), lambda b, i: (b, 0, i, 0))],
        out_shape=[jax.ShapeDtypeStruct((batch, MLA_HEADS, rows, MLA_QK), BF16),
                   jax.ShapeDtypeStruct((batch, MLA_HEADS, rows, MLA_VW), BF16)],
        compiler_params=_cparams(("parallel", "parallel")),
        name="mla_kv",
    )(proj, gain, wkv, cos, sin)


def _flash_segments(seq, tq, tk):
    segs = []
    for qi in range(seq // tq):
        pos = 0
        while pos < qi * tq:
            n = min(tk, qi * tq - pos)
            segs.append((qi, False, pos, n))
            pos += n
        segs.append((qi, True, qi * tq, tq))
    return segs


def _flash_kernel(q_ref, k_ref, v_ref, kl_ref, vl_ref, o_ref, *, tq, tk):
    seq = q_ref.shape[2]
    nt = (((1,), (1,)), ((), ()))
    causal = (lax.broadcasted_iota(jnp.int32, (tq, tq), 1) <= lax.broadcasted_iota(jnp.int32, (tq, tq), 0))
    lead_mask = lax.broadcasted_iota(jnp.int32, (tq, LEAD), 1) >= N_PAD

    def scores(seg):
        qi, diag, k0, n = seg
        q = q_ref[0, 0, qi * tq:(qi + 1) * tq, :]
        s = lax.dot_general(q, k_ref[0, 0, k0:k0 + n, :], nt, preferred_element_type=F32)
        if diag:
            s_lead = lax.dot_general(q, kl_ref[0, 0], nt, preferred_element_type=F32)
            s = jnp.concatenate([jnp.where(causal, s, NEG), jnp.where(lead_mask, s_lead, NEG)], axis=1)
        return s

    segs = _flash_segments(seq, tq, tk)
    s_cur = scores(segs[0])
    m = acc = None
    for i, (qi, diag, k0, n) in enumerate(segs):
        s_next = scores(segs[i + 1]) if i + 1 < len(segs) else None
        first = k0 == 0
        m_new = jnp.max(s_cur, axis=-1, keepdims=True)
        if not first:
            m_new = jnp.maximum(m, m_new)
        p = jnp.exp2(s_cur - m_new).astype(BF16)
        if diag:
            pv = (jnp.dot(p[:, :tq], v_ref[0, 0, k0:k0 + n, :], preferred_element_type=F32)
                  + jnp.dot(p[:, tq:], vl_ref[0, 0], preferred_element_type=F32))
        else:
            pv = jnp.dot(p, v_ref[0, 0, k0:k0 + n, :], preferred_element_type=F32)
        acc = pv if first else jnp.exp2(m - m_new) * acc + pv
        m = m_new
        if diag:
            o_ref[0, qi * tq:(qi + 1) * tq, :] = (acc[:, :MLA_V] / acc[:, MLA_V:MLA_V + 1]).astype(o_ref.dtype)
        s_cur = s_next


def _flash(q, k, v, k_lead, v_lead, batch, seq, tq=256, tk=512):
    tq = min(tq, seq)
    return pl.pallas_call(
        functools.partial(_flash_kernel, tq=tq, tk=tk),
        grid=(batch, MLA_HEADS),
        in_specs=[pl.BlockSpec((1, 1, seq, MLA_QK), lambda b, h: (b, h, 0, 0)),
                  pl.BlockSpec((1, 1, seq, MLA_QK), lambda b, h: (b, h, 0, 0)),
                  pl.BlockSpec((1, 1, seq, MLA_VW), lambda b, h: (b, h, 0, 0)),
                  pl.BlockSpec((1, 1, LEAD, MLA_QK), lambda b, h: (0, h, 0, 0)),
                  pl.BlockSpec((1, 1, LEAD, MLA_VW), lambda b, h: (0, h, 0, 0))],
        out_specs=pl.BlockSpec((1, seq, MLA_V), lambda b, h: (b, 0, h)),
        out_shape=jax.ShapeDtypeStruct((batch, seq, MLA_HEADS * MLA_V), BF16),
        compiler_params=_cparams(("parallel", "parallel")),
        name="mla_flash",
    )(q, k, v, k_lead, v_lead)


def _layer_norm(z, gain, bias):
    mu = jnp.mean(z, axis=-1, keepdims=True)
    zc = z - mu
    var = jnp.mean(zc * zc, axis=-1, keepdims=True)
    return zc * lax.rsqrt(var + EPS) * gain + bias


def _mix_kernel(x_ref, hg_ref, at_ref, ga_ref, gb_ref, wph_ref, wpm_ref, wo_ref, gain_ref, bias_ref,
                h_ref, ht_ref):
    ya = jnp.dot(hg_ref[...], wph_ref[...], preferred_element_type=F32)
    yb = jnp.dot(at_ref[...], wpm_ref[...], preferred_element_type=F32)
    mix = (jax.nn.sigmoid(ga_ref[...].astype(F32)) * ya + jax.nn.sigmoid(gb_ref[...].astype(F32)) * yb)
    mixed = jnp.dot(mix.astype(BF16), wo_ref[...], preferred_element_type=F32)
    h = _layer_norm(ALPHA * x_ref[...] + mixed, gain_ref[...], bias_ref[...])
    h_ref[...] = h
    ht_ref[...] = h.T.astype(ht_ref.dtype)


def _mix(x2, hg_o, at_o, gates, wph, wpm, wo, gain, bias):
    t = x2.shape[0]
    tm = _pick(t, 512)
    full = lambda a: pl.BlockSpec(a.shape, lambda i: (0,) * a.ndim)
    return pl.pallas_call(
        _mix_kernel,
        grid=(t // tm,),
        in_specs=[pl.BlockSpec((tm, D_MODEL), lambda i: (i, 0)),
                  pl.BlockSpec((tm, HG_WIDTH), lambda i: (i, 0)),
                  pl.BlockSpec((tm, MLA_HEADS * MLA_V), lambda i: (i, 0)),
                  pl.BlockSpec((tm, D_MODEL), lambda i: (i, COL_GA)),
                  pl.BlockSpec((tm, D_MODEL), lambda i: (i, COL_GB)),
                  full(wph), full(wpm), full(wo), full(gain), full(bias)],
        out_specs=[pl.BlockSpec((tm, D_MODEL), lambda i: (i, 0)),
                   pl.BlockSpec((D_MODEL, tm), lambda i: (0, i))],
        out_shape=[jax.ShapeDtypeStruct((t, D_MODEL), F32),
                   jax.ShapeDtypeStruct((D_MODEL, t), BF16)],
        compiler_params=_cparams(("parallel",)),
        name="mix_ln1",
    )(x2, hg_o, at_o, gates, gates, wph, wpm, wo, gain, bias)


N_TOP = PEER_TOPK + 1
SUBLANES = 8
TOP_ROWS = 24
PAIR_LIMIT = [N_TOP // (a + 1) for a in range(N_TOP)]
N_WIDE = sum(1 for n in PAIR_LIMIT if n > 1)
ROUTE_GROUP = 2


def _sort_network(n):
    def merge(lo, hi, r):
        step = r * 2
        if step < hi - lo:
            yield from merge(lo, hi, step)
            yield from merge(lo + r, hi, step)
            yield from [(i, i + r) for i in range(lo + r, hi - r, step)]
        else:
            yield (lo, lo + r)

    def sort(lo, hi):
        if hi - lo >= 1:
            mid = lo + (hi - lo) // 2
            yield from sort(lo, mid)
            yield from sort(mid + 1, hi)
            yield from merge(lo, hi, 1)

    return list(sort(0, n - 1))


def _sorted_slabs(slabs):
    n = 1 << (len(slabs) - 1).bit_length()
    v = list(slabs) + [None] * (n - len(slabs))
    for i, j in _sort_network(n):
        if v[j] is None:
            continue
        if v[i] is None:
            v[i], v[j] = v[j], None
        else:
            v[i], v[j] = jnp.maximum(v[i], v[j]), jnp.minimum(v[i], v[j])
    return v[:len(slabs)]


def _top_rows(slab_sets, n):
    sets = [list(c) for c in slab_sets]
    vals = [[] for _ in sets]
    for r in range(n):
        for i, cols in enumerate(sets):
            m = jnp.max(cols[0], axis=0, keepdims=True)
            vals[i].append(m)
            hit = cols[0] == m
            keep = min(len(cols), n - r - 1)
            sets[i] = [jnp.where(hit, cols[k + 1] if k + 1 < len(cols) else -jnp.inf, cols[k])
                       for k in range(keep)]
    return vals


def _count_above(rows, thr):
    count = jnp.zeros_like(thr)
    passed = []
    step = PEER_TOPK // 2
    while step >= 1:
        cands = [rows[base + step - 1] for base in range(0, PEER_TOPK, 2 * step)]
        for m in reversed(passed):
            cands = [jnp.where(m, cands[2 * i + 1], cands[2 * i]) for i in range(len(cands) // 2)]
        m = cands[0] > thr
        count = jnp.where(m, count + float(step), count)
        passed.append(m)
        step //= 2
    return jnp.where(rows[PEER_TOPK - 1] > thr, float(PEER_TOPK), count)


def _stack_rows(rows, tokens):
    ri = lax.broadcasted_iota(jnp.int32, (TOP_ROWS, tokens), 0)
    out = jnp.full((TOP_ROWS, tokens), -jnp.inf, F32)
    for r, v in enumerate(rows):
        out = jnp.where(ri == r, v, out)
    return out


def _route_kernel(ht_ref, wq_ref, sk_ref, r2_ref, c1_ref, e1_ref, e2_ref):
    tokens = ht_ref.shape[1]
    ri8 = lax.broadcasted_iota(jnp.int32, (N_WIDE, tokens), 0)
    qp = jnp.dot(wq_ref[...], ht_ref[...], preferred_element_type=F32).astype(BF16)
    for h0 in range(0, PEER_HEADS, ROUTE_GROUP):
        heads = range(h0, h0 + ROUTE_GROUP)
        s = [jnp.dot(sk_ref[hp], qp[hp * PEER_HALF:(hp + 1) * PEER_HALF], preferred_element_type=F32)
             for hp in range(2 * h0, 2 * (h0 + ROUTE_GROUP))]
        slabs = lambda x: [x[k:k + SUBLANES] for k in range(0, x.shape[0], SUBLANES)]
        tops = _top_rows([_sorted_slabs(slabs(x)) for x in s], N_TOP)
        cand = []
        for g in range(ROUTE_GROUP):
            t1, t2 = tops[2 * g], tops[2 * g + 1]
            t1s = _stack_rows(t1, tokens)
            t2s = _stack_rows(t2, tokens)
            parts = [jnp.where(ri8 < min(PAIR_LIMIT[a], N_WIDE), t1[a] + t2s[:N_WIDE], -jnp.inf)
                     for a in range(N_WIDE)]
            parts += slabs(t1[0] + t2s[N_WIDE:]) + slabs(t1s[N_WIDE:] + t2[0])
            cand.append(_sorted_slabs(parts))
        bests = _top_rows(cand, N_TOP)
        for g, h in enumerate(heads):
            s1, s2 = s[2 * g], s[2 * g + 1]
            t1, t2, best = tops[2 * g], tops[2 * g + 1], bests[g]
            m = best[0]
            z = jnp.zeros_like(m)
            for r in range(PEER_TOPK):
                z = z + jnp.exp(best[r] - m)
            thr = 0.5 * (best[PEER_TOPK - 1] + best[PEER_TOPK]) - s1
            c1 = _count_above(t2, thr)
            r2_ref[h] = _count_above(t2, s2).astype(r2_ref.dtype)
            c1_ref[h] = c1
            e1_ref[h] = jnp.exp(s1 - t1[0] - jnp.log(z))
            e2_ref[h] = jnp.exp(s2 - t2[0]).astype(e2_ref.dtype)


def _route(ht, wqt, sk):
    t = ht.shape[1]
    tm = _pick(t, 256)
    out = jax.ShapeDtypeStruct((PEER_HEADS, PEER_NKEYS, t), ht.dtype)
    out32 = jax.ShapeDtypeStruct((PEER_HEADS, PEER_NKEYS, t), F32)
    ospec = pl.BlockSpec((PEER_HEADS, PEER_NKEYS, tm), lambda i: (0, 0, i))
    return pl.pallas_call(
        _route_kernel,
        grid=(t // tm,),
        in_specs=[pl.BlockSpec((D_MODEL, tm), lambda i: (0, i)),
                  pl.BlockSpec(wqt.shape, lambda i: (0, 0)),
                  pl.BlockSpec(sk.shape, lambda i: (0, 0, 0))],
        out_specs=[ospec] * 4,
        out_shape=[out, out32, out32, out],
        compiler_params=_cparams(("parallel",)),
        name="peer_route",
    )(ht, wqt, sk)


GELU_C0 = 0.7978845608028654
GELU_C1 = GELU_C0 * 0.044715


def _gelu(a):
    t = jnp.exp2(a * ((-2.0 * LOG2_E * GELU_C0) + (-2.0 * LOG2_E * GELU_C1) * (a * a)))
    return a / (1.0 + t)


def _expert_kernel(ht_ref, h_ref, u_ref, vt_ref, r2_ref, c1_ref, e1_ref, e2_ref, gain_ref, bias_ref,
                   o_ref, acc_ref, *act_refs, chunk):
    e = pl.program_id(1)
    te, width = act_refs[0].shape
    dtype = act_refs[0].dtype
    packed_rows = 16

    @pl.when(e == 0)
    def _():
        acc_ref[...] = jnp.zeros_like(acc_ref)

    def row_bcast(ref, h, r, ls):
        row = jnp.broadcast_to(ref[h, r:r + 1, ls], (packed_rows, width)).astype(dtype)
        return pltpu.repeat(row, PEER_NKEYS // packed_rows, axis=0)

    lanes = [slice(i * width, (i + 1) * width) for i in range(len(act_refs))]
    a_all = [[jnp.dot(u_ref[c * chunk:(c + 1) * chunk, :], ht_ref[:, ls], preferred_element_type=F32)
              for c in range(te // chunk)] for ls in lanes]
    for act_ref, ls, a_chunks in zip(act_refs, lanes, a_all):
        for c, a in enumerate(a_chunks):
            for rr in range(chunk // PEER_NKEYS):
                r = c * (chunk // PEER_NKEYS) + rr
                g = None
                for h in range(PEER_HEADS):
                    gate = jnp.where(r2_ref[h, :, ls] < row_bcast(c1_ref, h, r, ls),
                                     e2_ref[h, :, ls] * row_bcast(e1_ref, h, r, ls), 0.0)
                    g = gate if g is None else g + gate
                a_r = a[rr * PEER_NKEYS:(rr + 1) * PEER_NKEYS, :]
                act_ref[r * PEER_NKEYS:(r + 1) * PEER_NKEYS, :] = _gelu(a_r.astype(dtype)) * g
        acc_ref[:, ls] += jnp.dot(vt_ref[...], act_ref[...], preferred_element_type=F32)

    @pl.when(e == pl.num_programs(1) - 1)
    def _():
        z = ALPHA * h_ref[...] + acc_ref[...].T
        o_ref[...] = _layer_norm(z, gain_ref[...], bias_ref[...])


def _experts(ht, h1, u, vt, r2, c1, e1, e2, gain, bias, tm=512, te=2048, chunk=512):
    t = h1.shape[0]
    tm = _pick(t, tm)
    hspec = pl.BlockSpec((PEER_HEADS, PEER_NKEYS, tm), lambda i, e: (0, 0, i))
    rspec = pl.BlockSpec((PEER_HEADS, te // PEER_NKEYS, tm), lambda i, e: (0, e, i))
    vec = pl.BlockSpec((1, D_MODEL), lambda i, e: (0, 0))
    return pl.pallas_call(
        functools.partial(_expert_kernel, chunk=chunk),
        grid=(t // tm, PEER_N // te),
        in_specs=[pl.BlockSpec((D_MODEL, tm), lambda i, e: (0, i)),
                  pl.BlockSpec((tm, D_MODEL), lambda i, e: (i, 0)),
                  pl.BlockSpec((te, D_MODEL), lambda i, e: (e, 0)),
                  pl.BlockSpec((D_MODEL, te), lambda i, e: (0, e)),
                  hspec, rspec, rspec, hspec, vec, vec],
        out_specs=pl.BlockSpec((tm, D_MODEL), lambda i, e: (i, 0)),
        out_shape=jax.ShapeDtypeStruct((t, D_MODEL), F32),
        scratch_shapes=[pltpu.VMEM((D_MODEL, tm), F32)] + [pltpu.VMEM((te, MXU_WIDTH), ht.dtype)] * (tm // MXU_WIDTH),
        compiler_params=_cparams(("parallel", "arbitrary")),
        name="peer_experts",
    )(ht, h1, u, vt, r2, c1, e1, e2, gain, bias)


def _rope_tables(pos):
    half = MLA_ROPE // 2
    inv_freq = ROPE_BASE ** (-jnp.arange(half, dtype=F32) / half)
    ang = pos.astype(F32)[:, None] * inv_freq[None, :]
    zeros = jnp.zeros((pos.shape[0], MLA_NOPE - MLA_ROPE), F32)
    cos, sin = jnp.cos(ang), jnp.sin(ang)
    return jnp.concatenate([cos, cos, zeros], axis=1), jnp.concatenate([sin, sin, zeros], axis=1)


def _rot_cols(w):
    half = MLA_ROPE // 2
    return jnp.concatenate([-w[..., half:], w[..., :half]], axis=-1)


def kernel(x, meta_tokens, hgrn_lb_logits, w_in, q_norm_gain, kv_norm_gain, w_uq, w_ukv, hgrn_norm_gain,
           w_proj_hgrn, w_proj_mla, w_out, ln1_gain, ln1_bias, peer_query, peer_sub_keys, peer_u, peer_v,
           ln2_gain, ln2_bias):
    batch, seq, d = x.shape
    p_rows = LEAD + seq
    t_real = batch * seq
    l = 0

    h_lead = jnp.concatenate([jnp.zeros((N_PAD, d), BF16), meta_tokens.astype(BF16)], axis=0)
    x2 = x.reshape(t_real, d)

    lower_bounds = jnp.cumsum(jax.nn.softmax(hgrn_lb_logits.astype(F32), axis=0), axis=0)
    rows = jnp.arange(p_rows)
    cos_all, sin_all = _rope_tables(rows - N_PAD)

    w = w_in[l]
    o_hg = 4 * HG_WIDTH
    o_cq = o_hg + MLA_Q_RANK
    o_ckv = o_cq + MLA_KV_RANK
    o_kr = o_ckv + MLA_ROPE
    w_kr = w[:, o_ckv:o_kr]
    zpad = jnp.zeros((d, MLA_NOPE - MLA_ROPE), F32)
    w_all = jnp.concatenate([w[:, :o_hg],
                             w[:, o_cq:o_ckv], w_kr, zpad, _rot_cols(w_kr), zpad,
                             w[:, o_hg:o_cq], jnp.zeros((d, MLA_Q_PAD - MLA_Q_RANK), F32),
                             w[:, o_kr:]], axis=1).astype(BF16)
    n_lead_cols = o_hg + MLA_KV_RANK + 2 * MLA_NOPE

    wq3 = w_uq[l].reshape(MLA_Q_RANK, MLA_HEADS, MLA_NOPE + MLA_ROPE)
    wq_rope = wq3[..., MLA_NOPE:]
    wq = jnp.concatenate([wq3[..., :MLA_NOPE], wq_rope, _rot_cols(wq_rope)], axis=-1)
    wq = wq.transpose(1, 0, 2).astype(BF16)
    wkv = w_ukv[l].reshape(MLA_KV_RANK, MLA_HEADS, MLA_NOPE + MLA_V).transpose(1, 0, 2).astype(BF16)

    proj = _matmul(x2, w_all, _pick(t_real, 1024), 1024)
    proj_lead = _matmul(h_lead, w_all[:, :n_lead_cols], LEAD, n_lead_cols // 3)

    hg_o = _hgrn(proj, proj_lead, lower_bounds[l][None, :], hgrn_norm_gain[l][None, :], batch, seq)

    q = _qprep(proj, q_norm_gain[l][None, :], wq, cos_all[LEAD:], sin_all[LEAD:], batch, seq)
    k, v = _kvprep(proj, kv_norm_gain[l][None, :], wkv, cos_all[LEAD:], sin_all[LEAD:], batch, seq)
    k_lead, v_lead = _kvprep(proj_lead, kv_norm_gain[l][None, :], wkv, cos_all[:LEAD], sin_all[:LEAD], 1, LEAD)
    at_o = _flash(q, k, v, k_lead, v_lead, batch, seq)

    h1, h1t = _mix(x2, hg_o.reshape(t_real, HG_WIDTH), at_o.reshape(t_real, MLA_HEADS * MLA_V), proj,
                   w_proj_hgrn[l].astype(BF16), w_proj_mla[l].astype(BF16), w_out[l].astype(BF16),
                   ln1_gain[l][None, :], ln1_bias[l][None, :])

    wqt = peer_query[l].T.astype(BF16)
    sk = peer_sub_keys[l].reshape(PEER_HEADS * 2, PEER_NKEYS, PEER_HALF).astype(BF16)
    r2, c1, e1, e2 = _route(h1t, wqt, sk)

    out = _experts(h1t, h1, peer_u[l].astype(BF16), peer_v[l].T.astype(BF16), r2, c1, e1, e2,
                   ln2_gain[l][None, :], ln2_bias[l][None, :])
    return out.reshape(batch, seq, d)
```

```python
import functools

import numpy as np
import jax
import jax.numpy as jnp
from jax import lax
from jax.experimental import pallas as pl
from jax.experimental.pallas import tpu as pltpu

F32 = jnp.float32
BF16 = jnp.bfloat16

D_MODEL = 1024
DEPTH = 1
N_META = 16
LEAD = 128
N_PAD = LEAD - N_META

HG_HEADS = 8
HG_D = 128
HG_WIDTH = HG_HEADS * HG_D
HG_SUB = 16

MLA_HEADS = 16
MLA_NOPE = 128
MLA_ROPE = 64
MLA_V = 128
MLA_Q_RANK = 384
MLA_KV_RANK = 256
MLA_Q_PAD = 512
COL_KV = 4 * 1024 // 512
COL_CQ = COL_KV + 1
COL_GA = 5
COL_GB = 6
MLA_QK = 256
MLA_SCALE = (MLA_NOPE + MLA_ROPE) ** -0.5
MLA_VW = 256
ROPE_BASE = 10000.0
LOG2_E = 1.4426950408889634

PEER_HEADS = 8
PEER_NKEYS = 128
PEER_N = PEER_NKEYS * PEER_NKEYS
PEER_HALF = 128
PEER_TOPK = 16

ALPHA = (2 * DEPTH) ** 0.25
EPS = 1e-5
NEG = -1e30

VMEM_LIMIT = 56 * 1024 * 1024
MXU_WIDTH = 256


def _cparams(sem):
    return pltpu.CompilerParams(dimension_semantics=sem, vmem_limit_bytes=VMEM_LIMIT)


def _pick(n, pref):
    t = min(n, pref)
    while n % t:
        t -= 128
    return t


def _mm_kernel(x_ref, w_ref, o_ref):
    x = x_ref[...].astype(w_ref.dtype)
    o_ref[...] = jnp.dot(x, w_ref[...], preferred_element_type=F32).astype(o_ref.dtype)


def _matmul(x, w, tm, tn):
    m, k = x.shape
    out_dtype = w.dtype
    n = w.shape[1]
    return pl.pallas_call(
        _mm_kernel,
        grid=(m // tm, n // tn),
        in_specs=[pl.BlockSpec((tm, k), lambda i, j: (i, 0)),
                  pl.BlockSpec((k, tn), lambda i, j: (0, j))],
        out_specs=pl.BlockSpec((tm, tn), lambda i, j: (i, j)),
        out_shape=jax.ShapeDtypeStruct((m, n), out_dtype),
        compiler_params=_cparams(("parallel", "parallel")),
        name="in_proj",
    )(x, w)


def _hgrn_chunk(q, k, lf, iv, st_ref, tri, masks, chunk):
    width = q.shape[1]
    hi = lf.astype(BF16)
    r1 = lf - hi.astype(F32)
    mid = r1.astype(BF16)
    lo = (r1 - mid.astype(F32)).astype(BF16)
    g = (jnp.dot(tri, hi, preferred_element_type=F32)
         + jnp.dot(tri, mid, preferred_element_type=F32)
         + jnp.dot(tri, lo, preferred_element_type=F32))

    def bcast_row(r, n):
        return jnp.broadcast_to(g[r:r + 1, :], (n, width))

    def prev_end(s):
        parts = [jnp.zeros((s, width), F32)] + [bcast_row(b * s - 1, s) for b in range(1, chunk // s)]
        return parts[0] if len(parts) == 1 else jnp.concatenate(parts, axis=0)

    def own_end(s):
        parts = [bcast_row((b + 1) * s - 1, s) for b in range(chunk // s)]
        return parts[0] if len(parts) == 1 else jnp.concatenate(parts, axis=0)

    q_c = (q * jnp.exp(g)).astype(BF16)
    k_c = (k * jnp.exp(own_end(chunk) - g)).astype(BF16)
    dec = jnp.exp(g[chunk - 1:chunk, :])
    p16 = prev_end(HG_SUB)
    q_lv = [(q * jnp.exp(g - p16)).astype(BF16)]
    k_lv = [(k * jnp.exp(p16 - g)).astype(BF16)]
    s = HG_SUB
    while s < chunk:
        q_lv.append(q_lv[0] if s == HG_SUB else (q * jnp.exp(g - prev_end(s))).astype(BF16))
        k_lv.append((k * jnp.exp(own_end(s) - g)).astype(BF16))
        s *= 2

    nt = (((1,), (1,)), ((), ()))
    tn = (((0,), (0,)), ((), ()))
    heads = range(HG_HEADS)
    hs = [slice(h * HG_D, (h + 1) * HG_D) for h in heads]
    st = [st_ref[h] for h in heads]
    o_inter = [lax.dot_general(q_c[:, hs[h]], st[h].astype(BF16), nt, preferred_element_type=F32) for h in heads]
    upd = [lax.dot_general(iv[:, hs[h]], k_c[:, hs[h]], tn, preferred_element_type=F32) for h in heads]
    scores = [[lax.dot_general(ql[:, hs[h]], kl[:, hs[h]], nt, preferred_element_type=F32) for h in heads]
              for ql, kl in zip(q_lv, k_lv)]
    outs, st_new = [], []
    for h in heads:
        a = jnp.where(masks[0], scores[0][h], 0.0)
        for lvl in range(1, len(masks)):
            a = jnp.where(masks[lvl], scores[lvl][h], a)
        outs.append(o_inter[h] + jnp.dot(a.astype(BF16), iv[:, hs[h]], preferred_element_type=F32))
        st_new.append(st[h] * dec[:, hs[h]] + upd[h])
    return outs, st_new


def _hgrn_kernel(q_ref, f_ref, i_ref, g_ref, ql_ref, fl_ref, il_ref, gl_ref, lb_ref, gain_ref, tri_ref,
                 o_ref, st_ref, *, chunk, rows):
    c = pl.program_id(1)
    lead = c == 0
    pick = lambda real_ref, lead_ref, rs: jnp.where(lead, lead_ref[rs, :], real_ref[rs, :])

    @pl.when(c == 0)
    def _():
        st_ref[...] = jnp.zeros_like(st_ref)

    ri = lax.broadcasted_iota(jnp.int32, (chunk, chunk), 0)
    ci = lax.broadcasted_iota(jnp.int32, (chunk, chunk), 1)
    blk = lambda v, s: lax.shift_right_logical(v, s.bit_length() - 1)
    masks = [(blk(ri, HG_SUB) == blk(ci, HG_SUB)) & (ci <= ri)]
    s = HG_SUB
    while s < chunk:
        masks.append((blk(ri, 2 * s) == blk(ci, 2 * s)) & ((blk(ri, s) & 1) == 1) & ((blk(ci, s) & 1) == 0))
        s *= 2
    tri = tri_ref[...]
    lb = lb_ref[...]

    for n in range(rows // chunk):
        rs = slice(n * chunk, (n + 1) * chunk)
        seq_row = c * rows + n * chunk + lax.broadcasted_iota(jnp.int32, (chunk, HG_WIDTH), 0)
        valid = seq_row >= N_PAD
        f = lb + (1.0 - lb) * jax.nn.sigmoid(pick(f_ref, fl_ref, rs).astype(F32))
        lf = jnp.where(valid, jnp.log(f), 0.0)
        k = jnp.where(valid, 1.0 - f, 0.0)
        outs, st_new = _hgrn_chunk(pick(q_ref, ql_ref, rs).astype(F32), k, lf, pick(i_ref, il_ref, rs), st_ref, tri,
                                   masks, chunk)
        gate = pick(g_ref, gl_ref, rs).astype(F32)
        gate = gate * jax.nn.sigmoid(gate)
        for h in range(HG_HEADS):
            hs = slice(h * HG_D, (h + 1) * HG_D)
            st_ref[h] = st_new[h]
            o = outs[h]
            ms = jnp.mean(o * o, axis=-1, keepdims=True)
            o = o * lax.rsqrt(ms + EPS) * gain_ref[:, hs]
            o_ref[0, rs, hs] = (o * gate[:, hs]).astype(o_ref.dtype)


def _hgrn(proj, proj_lead, lb, gain, batch, seq, chunk=128, rows=LEAD):
    ns = seq // rows
    tri = jnp.asarray(np.tril(np.ones((chunk, chunk), np.float32)), BF16)
    real = lambda j: pl.BlockSpec((rows, HG_WIDTH), lambda b, c: (b * ns + jnp.maximum(c - 1, 0), j))
    lead = lambda j: pl.BlockSpec((rows, HG_WIDTH), lambda b, c: (0, j))
    vec = pl.BlockSpec((1, HG_WIDTH), lambda b, c: (0, 0))
    return pl.pallas_call(
        functools.partial(_hgrn_kernel, chunk=chunk, rows=rows),
        grid=(batch, ns + 1),
        in_specs=[real(0), real(1), real(2), real(3), lead(0), lead(1), lead(2), lead(3), vec, vec,
                  pl.BlockSpec((chunk, chunk), lambda b, c: (0, 0))],
        out_specs=pl.BlockSpec((1, rows, HG_WIDTH), lambda b, c: (b, jnp.maximum(c - 1, 0), 0)),
        out_shape=jax.ShapeDtypeStruct((batch, seq, HG_WIDTH), BF16),
        scratch_shapes=[pltpu.VMEM((HG_HEADS, HG_D, HG_D), F32)],
        compiler_params=_cparams(("parallel", "arbitrary")),
        name="hgrn2",
    )(proj, proj, proj, proj, proj_lead, proj_lead, proj_lead, proj_lead, lb, gain, tri)


def _rms(x, gain):
    ms = jnp.mean(x * x, axis=-1, keepdims=True)
    return x * lax.rsqrt(ms + EPS) * gain


def _qprep_kernel(cq_ref, gain_ref, w_ref, cos_ref, sin_ref, o_ref):
    xn = _rms(cq_ref[:, :MLA_Q_RANK].astype(F32), gain_ref[...]).astype(BF16)
    cos = cos_ref[...]
    sin = sin_ref[...]
    for h in range(MLA_HEADS):
        y = jnp.dot(xn, w_ref[h], preferred_element_type=F32)
        o_ref[0, h, :, :MLA_NOPE] = (y[:, :MLA_NOPE] * (MLA_SCALE * LOG2_E)).astype(o_ref.dtype)
        t = y[:, MLA_NOPE:]
        roped = t * cos + pltpu.roll(t, MLA_ROPE, axis=1) * sin
        o_ref[0, h, :, MLA_NOPE:] = (roped * (MLA_SCALE * LOG2_E)).astype(o_ref.dtype)


def _qprep(qg, gain, wq, cos, sin, batch, seq):
    tm = _pick(seq, 512)
    nb = seq // tm
    return pl.pallas_call(
        _qprep_kernel,
        grid=(batch, nb),
        in_specs=[pl.BlockSpec((tm, MLA_Q_PAD), lambda b, i: (b * nb + i, COL_CQ)),
                  pl.BlockSpec((1, MLA_Q_RANK), lambda b, i: (0, 0)),
                  pl.BlockSpec((MLA_HEADS, MLA_Q_RANK, MLA_QK), lambda b, i: (0, 0, 0)),
                  pl.BlockSpec((tm, MLA_NOPE), lambda b, i: (i, 0)),
                  pl.BlockSpec((tm, MLA_NOPE), lambda b, i: (i, 0))],
        out_specs=pl.BlockSpec((1, MLA_HEADS, tm, MLA_QK), lambda b, i: (b, 0, i, 0)),
        out_shape=jax.ShapeDtypeStruct((batch, MLA_HEADS, seq, MLA_QK), BF16),
        compiler_params=_cparams(("parallel", "parallel")),
        name="mla_q",
    )(qg, gain, wq, cos, sin)


def _kvprep_kernel(kv_ref, gain_ref, w_ref, cos_ref, sin_ref, k_ref, v_ref):
    x = kv_ref[...]
    xn = _rms(x[:, :MLA_KV_RANK].astype(F32), gain_ref[...]).astype(BF16)
    kr = (x[:, MLA_KV_RANK:MLA_KV_RANK + MLA_NOPE].astype(F32) * cos_ref[...]
          + x[:, MLA_KV_RANK + MLA_NOPE:].astype(F32) * sin_ref[...]).astype(k_ref.dtype)
    ones_col = (lax.broadcasted_iota(jnp.int32, (x.shape[0], MLA_VW - MLA_V), 1) == 0).astype(v_ref.dtype)
    for h in range(MLA_HEADS):
        y = jnp.dot(xn, w_ref[h], preferred_element_type=F32)
        k_ref[0, h, :, :MLA_NOPE] = y[:, :MLA_NOPE].astype(k_ref.dtype)
        k_ref[0, h, :, MLA_NOPE:] = kr
        v_ref[0, h, :, :MLA_V] = y[:, MLA_NOPE:].astype(v_ref.dtype)
        v_ref[0, h, :, MLA_V:] = ones_col


def _kvprep(proj, gain, wkv, cos, sin, batch, rows):
    tm = _pick(rows, 1024)
    nb = rows // tm
    return pl.pallas_call(
        _kvprep_kernel,
        grid=(batch, nb),
        in_specs=[pl.BlockSpec((tm, MLA_KV_RANK + 2 * MLA_NOPE), lambda b, i: (b * nb + i, COL_KV)),
                  pl.BlockSpec((1, MLA_KV_RANK), lambda b, i: (0, 0)),
                  pl.BlockSpec((MLA_HEADS, MLA_KV_RANK, MLA_NOPE + MLA_V), lambda b, i: (0, 0, 0)),
                  pl.BlockSpec((tm, MLA_NOPE), lambda b, i: (i, 0)),
                  pl.BlockSpec((tm, MLA_NOPE), lambda b, i: (i, 0))],
        out_specs=[pl.BlockSpec((1, MLA_HEADS, tm, MLA_QK), lambda b, i: (b, 0, i, 0)),
                   pl.BlockSpec((1, MLA_HEADS, tm, MLA_VW), lambda b, i: (b, 0, i, 0))],
        out_shape=[jax.ShapeDtypeStruct((batch, MLA_HEADS, rows, MLA_QK), BF16),
                   jax.ShapeDtypeStruct((batch, MLA_HEADS, rows, MLA_VW), BF16)],
        compiler_params=_cparams(("parallel", "parallel")),
        name="mla_kv",
    )(proj, gain, wkv, cos, sin)


def _flash_segments(seq, tq, tk):
    segs = []
    for qi in range(seq // tq):
        pos = 0
        while pos < qi * tq:
            n = min(tk, qi * tq - pos)
            segs.append((qi, False, pos, n))
            pos += n
        segs.append((qi, True, qi * tq, tq))
    return segs


def _flash_kernel(q_ref, k_ref, v_ref, kl_ref, vl_ref, o_ref, *, tq, tk):
    seq = q_ref.shape[2]
    nt = (((1,), (1,)), ((), ()))
    causal = (lax.broadcasted_iota(jnp.int32, (tq, tq), 1) <= lax.broadcasted_iota(jnp.int32, (tq, tq), 0))
    lead_mask = lax.broadcasted_iota(jnp.int32, (tq, LEAD), 1) >= N_PAD

    def scores(seg):
        qi, diag, k0, n = seg
        q = q_ref[0, 0, qi * tq:(qi + 1) * tq, :]
        s = lax.dot_general(q, k_ref[0, 0, k0:k0 + n, :], nt, preferred_element_type=F32)
        if diag:
            s_lead = lax.dot_general(q, kl_ref[0, 0], nt, preferred_element_type=F32)
            s = jnp.concatenate([jnp.where(causal, s, NEG), jnp.where(lead_mask, s_lead, NEG)], axis=1)
        return s

    segs = _flash_segments(seq, tq, tk)
    s_cur = scores(segs[0])
    m = acc = None
    for i, (qi, diag, k0, n) in enumerate(segs):
        s_next = scores(segs[i + 1]) if i + 1 < len(segs) else None
        first = k0 == 0
        m_new = jnp.max(s_cur, axis=-1, keepdims=True)
        if not first:
            m_new = jnp.maximum(m, m_new)
        p = jnp.exp2(s_cur - m_new).astype(BF16)
        if diag:
            pv = (jnp.dot(p[:, :tq], v_ref[0, 0, k0:k0 + n, :], preferred_element_type=F32)
                  + jnp.dot(p[:, tq:], vl_ref[0, 0], preferred_element_type=F32))
        else:
            pv = jnp.dot(p, v_ref[0, 0, k0:k0 + n, :], preferred_element_type=F32)
        acc = pv if first else jnp.exp2(m - m_new) * acc + pv
        m = m_new
        if diag:
            o_ref[0, qi * tq:(qi + 1) * tq, :] = (acc[:, :MLA_V] / acc[:, MLA_V:MLA_V + 1]).astype(o_ref.dtype)
        s_cur = s_next


def _flash(q, k, v, k_lead, v_lead, batch, seq, tq=256, tk=1024):
    tq = min(tq, seq)
    return pl.pallas_call(
        functools.partial(_flash_kernel, tq=tq, tk=tk),
        grid=(batch, MLA_HEADS),
        in_specs=[pl.BlockSpec((1, 1, seq, MLA_QK), lambda b, h: (b, h, 0, 0)),
                  pl.BlockSpec((1, 1, seq, MLA_QK), lambda b, h: (b, h, 0, 0)),
                  pl.BlockSpec((1, 1, seq, MLA_VW), lambda b, h: (b, h, 0, 0)),
                  pl.BlockSpec((1, 1, LEAD, MLA_QK), lambda b, h: (0, h, 0, 0)),
                  pl.BlockSpec((1, 1, LEAD, MLA_VW), lambda b, h: (0, h, 0, 0))],
        out_specs=pl.BlockSpec((1, seq, MLA_V), lambda b, h: (b, 0, h)),
        out_shape=jax.ShapeDtypeStruct((batch, seq, MLA_HEADS * MLA_V), BF16),
        compiler_params=_cparams(("parallel", "parallel")),
        name="mla_flash",
    )(q, k, v, k_lead, v_lead)


def _layer_norm(z, gain, bias):
    mu = jnp.mean(z, axis=-1, keepdims=True)
    zc = z - mu
    var = jnp.mean(zc * zc, axis=-1, keepdims=True)
    return zc * lax.rsqrt(var + EPS) * gain + bias


def _mix_kernel(x_ref, hg_ref, at_ref, ga_ref, gb_ref, wph_ref, wpm_ref, wo_ref, gain_ref, bias_ref,
                h_ref, ht_ref):
    ya = jnp.dot(hg_ref[...], wph_ref[...], preferred_element_type=F32)
    yb = jnp.dot(at_ref[...], wpm_ref[...], preferred_element_type=F32)
    mix = (jax.nn.sigmoid(ga_ref[...].astype(F32)) * ya + jax.nn.sigmoid(gb_ref[...].astype(F32)) * yb)
    mixed = jnp.dot(mix.astype(BF16), wo_ref[...], preferred_element_type=F32)
    h = _layer_norm(ALPHA * x_ref[...] + mixed, gain_ref[...], bias_ref[...])
    h_ref[...] = h
    ht_ref[...] = h.T.astype(ht_ref.dtype)


def _mix(x2, hg_o, at_o, gates, wph, wpm, wo, gain, bias):
    t = x2.shape[0]
    tm = _pick(t, 512)
    full = lambda a: pl.BlockSpec(a.shape, lambda i: (0,) * a.ndim)
    return pl.pallas_call(
        _mix_kernel,
        grid=(t // tm,),
        in_specs=[pl.BlockSpec((tm, D_MODEL), lambda i: (i, 0)),
                  pl.BlockSpec((tm, HG_WIDTH), lambda i: (i, 0)),
                  pl.BlockSpec((tm, MLA_HEADS * MLA_V), lambda i: (i, 0)),
                  pl.BlockSpec((tm, D_MODEL), lambda i: (i, COL_GA)),
                  pl.BlockSpec((tm, D_MODEL), lambda i: (i, COL_GB)),
                  full(wph), full(wpm), full(wo), full(gain), full(bias)],
        out_specs=[pl.BlockSpec((tm, D_MODEL), lambda i: (i, 0)),
                   pl.BlockSpec((D_MODEL, tm), lambda i: (0, i))],
        out_shape=[jax.ShapeDtypeStruct((t, D_MODEL), F32),
                   jax.ShapeDtypeStruct((D_MODEL, t), BF16)],
        compiler_params=_cparams(("parallel",)),
        name="mix_ln1",
    )(x2, hg_o, at_o, gates, gates, wph, wpm, wo, gain, bias)


N_TOP = PEER_TOPK + 1
SUBLANES = 8
TOP_ROWS = 24
PAIR_LIMIT = [N_TOP // (a + 1) for a in range(N_TOP)]
N_WIDE = sum(1 for n in PAIR_LIMIT if n > 1)
ROUTE_GROUP = 2


def _sort_network(n):
    def merge(lo, hi, r):
        step = r * 2
        if step < hi - lo:
            yield from merge(lo, hi, step)
            yield from merge(lo + r, hi, step)
            yield from [(i, i + r) for i in range(lo + r, hi - r, step)]
        else:
            yield (lo, lo + r)

    def sort(lo, hi):
        if hi - lo >= 1:
            mid = lo + (hi - lo) // 2
            yield from sort(lo, mid)
            yield from sort(mid + 1, hi)
            yield from merge(lo, hi, 1)

    return list(sort(0, n - 1))


def _sorted_slabs(slabs):
    n = 1 << (len(slabs) - 1).bit_length()
    v = list(slabs) + [None] * (n - len(slabs))
    for i, j in _sort_network(n):
        if v[j] is None:
            continue
        if v[i] is None:
            v[i], v[j] = v[j], None
        else:
            v[i], v[j] = jnp.maximum(v[i], v[j]), jnp.minimum(v[i], v[j])
    return v[:len(slabs)]


def _top_rows(slab_sets, n):
    sets = [list(c) for c in slab_sets]
    vals = [[] for _ in sets]
    for r in range(n):
        for i, cols in enumerate(sets):
            m = jnp.max(cols[0], axis=0, keepdims=True)
            vals[i].append(m)
            hit = cols[0] == m
            keep = min(len(cols), n - r - 1)
            sets[i] = [jnp.where(hit, cols[k + 1] if k + 1 < len(cols) else -jnp.inf, cols[k])
                       for k in range(keep)]
    return vals


def _count_above(rows, thr):
    count = jnp.zeros_like(thr)
    passed = []
    step = PEER_TOPK // 2
    while step >= 1:
        cands = [rows[base + step - 1] for base in range(0, PEER_TOPK, 2 * step)]
        for m in reversed(passed):
            cands = [jnp.where(m, cands[2 * i + 1], cands[2 * i]) for i in range(len(cands) // 2)]
        m = cands[0] > thr
        count = jnp.where(m, count + float(step), count)
        passed.append(m)
        step //= 2
    return jnp.where(rows[PEER_TOPK - 1] > thr, float(PEER_TOPK), count)


def _stack_rows(rows, tokens):
    ri = lax.broadcasted_iota(jnp.int32, (TOP_ROWS, tokens), 0)
    out = jnp.full((TOP_ROWS, tokens), -jnp.inf, F32)
    for r, v in enumerate(rows):
        out = jnp.where(ri == r, v, out)
    return out


def _route_kernel(ht_ref, wq_ref, sk_ref, r2_ref, c1_ref, e1_ref, e2_ref):
    tokens = ht_ref.shape[1]
    ri8 = lax.broadcasted_iota(jnp.int32, (N_WIDE, tokens), 0)
    qp = jnp.dot(wq_ref[...], ht_ref[...], preferred_element_type=F32).astype(BF16)
    for h0 in range(0, PEER_HEADS, ROUTE_GROUP):
        heads = range(h0, h0 + ROUTE_GROUP)
        s = [jnp.dot(sk_ref[hp], qp[hp * PEER_HALF:(hp + 1) * PEER_HALF], preferred_element_type=F32)
             for hp in range(2 * h0, 2 * (h0 + ROUTE_GROUP))]
        slabs = lambda x: [x[k:k + SUBLANES] for k in range(0, x.shape[0], SUBLANES)]
        tops = _top_rows([_sorted_slabs(slabs(x)) for x in s], N_TOP)
        cand = []
        for g in range(ROUTE_GROUP):
            t1, t2 = tops[2 * g], tops[2 * g + 1]
            t1s = _stack_rows(t1, tokens)
            t2s = _stack_rows(t2, tokens)
            parts = [jnp.where(ri8 < min(PAIR_LIMIT[a], N_WIDE), t1[a] + t2s[:N_WIDE], -jnp.inf)
                     for a in range(N_WIDE)]
            parts += slabs(t1[0] + t2s[N_WIDE:]) + slabs(t1s[N_WIDE:] + t2[0])
            cand.append(_sorted_slabs(parts))
        bests = _top_rows(cand, N_TOP)
        for g, h in enumerate(heads):
            s1, s2 = s[2 * g], s[2 * g + 1]
            t1, t2, best = tops[2 * g], tops[2 * g + 1], bests[g]
            m = best[0]
            z = jnp.zeros_like(m)
            for r in range(PEER_TOPK):
                z = z + jnp.exp(best[r] - m)
            thr = 0.5 * (best[PEER_TOPK - 1] + best[PEER_TOPK]) - s1
            c1 = _count_above(t2, thr)
            r2_ref[h] = _count_above(t2, s2).astype(r2_ref.dtype)
            c1_ref[h] = c1
            e1_ref[h] = jnp.exp(s1 - t1[0] - jnp.log(z))
            e2_ref[h] = jnp.exp(s2 - t2[0]).astype(e2_ref.dtype)


def _route(ht, wqt, sk):
    t = ht.shape[1]
    tm = _pick(t, 256)
    out = jax.ShapeDtypeStruct((PEER_HEADS, PEER_NKEYS, t), ht.dtype)
    out32 = jax.ShapeDtypeStruct((PEER_HEADS, PEER_NKEYS, t), F32)
    ospec = pl.BlockSpec((PEER_HEADS, PEER_NKEYS, tm), lambda i: (0, 0, i))
    return pl.pallas_call(
        _route_kernel,
        grid=(t // tm,),
        in_specs=[pl.BlockSpec((D_MODEL, tm), lambda i: (0, i)),
                  pl.BlockSpec(wqt.shape, lambda i: (0, 0)),
                  pl.BlockSpec(sk.shape, lambda i: (0, 0, 0))],
        out_specs=[ospec] * 4,
        out_shape=[out, out32, out32, out],
        compiler_params=_cparams(("parallel",)),
        name="peer_route",
    )(ht, wqt, sk)


GELU_C0 = 0.7978845608028654
GELU_C1 = GELU_C0 * 0.044715


def _gelu(a):
    t = jnp.exp2(a * ((-2.0 * LOG2_E * GELU_C0) + (-2.0 * LOG2_E * GELU_C1) * (a * a)))
    return a / (1.0 + t)


def _expert_kernel(ht_ref, h_ref, u_ref, vt_ref, r2_ref, c1_ref, e1_ref, e2_ref, gain_ref, bias_ref,
                   o_ref, acc_ref, *act_refs, chunk):
    e = pl.program_id(1)
    te, width = act_refs[0].shape
    dtype = act_refs[0].dtype
    packed_rows = 16

    @pl.when(e == 0)
    def _():
        acc_ref[...] = jnp.zeros_like(acc_ref)

    def row_bcast(ref, h, r, ls):
        row = jnp.broadcast_to(ref[h, r:r + 1, ls], (packed_rows, width)).astype(dtype)
        return pltpu.repeat(row, PEER_NKEYS // packed_rows, axis=0)

    lanes = [slice(i * width, (i + 1) * width) for i in range(len(act_refs))]
    a_all = [[jnp.dot(u_ref[c * chunk:(c + 1) * chunk, :], ht_ref[:, ls], preferred_element_type=F32)
              for c in range(te // chunk)] for ls in lanes]
    for act_ref, ls, a_chunks in zip(act_refs, lanes, a_all):
        for c, a in enumerate(a_chunks):
            for rr in range(chunk // PEER_NKEYS):
                r = c * (chunk // PEER_NKEYS) + rr
                g = None
                for h in range(PEER_HEADS):
                    gate = jnp.where(r2_ref[h, :, ls] < row_bcast(c1_ref, h, r, ls),
                                     e2_ref[h, :, ls] * row_bcast(e1_ref, h, r, ls), 0.0)
                    g = gate if g is None else g + gate
                a_r = a[rr * PEER_NKEYS:(rr + 1) * PEER_NKEYS, :]
                act_ref[r * PEER_NKEYS:(r + 1) * PEER_NKEYS, :] = _gelu(a_r.astype(dtype)) * g
        acc_ref[:, ls] += jnp.dot(vt_ref[...], act_ref[...], preferred_element_type=F32)

    @pl.when(e == pl.num_programs(1) - 1)
    def _():
        z = ALPHA * h_ref[...] + acc_ref[...].T
        o_ref[...] = _layer_norm(z, gain_ref[...], bias_ref[...])


def _experts(ht, h1, u, vt, r2, c1, e1, e2, gain, bias, tm=512, te=2048, chunk=512):
    t = h1.shape[0]
    tm = _pick(t, tm)
    hspec = pl.BlockSpec((PEER_HEADS, PEER_NKEYS, tm), lambda i, e: (0, 0, i))
    rspec = pl.BlockSpec((PEER_HEADS, te // PEER_NKEYS, tm), lambda i, e: (0, e, i))
    vec = pl.BlockSpec((1, D_MODEL), lambda i, e: (0, 0))
    return pl.pallas_call(
        functools.partial(_expert_kernel, chunk=chunk),
        grid=(t // tm, PEER_N // te),
        in_specs=[pl.BlockSpec((D_MODEL, tm), lambda i, e: (0, i)),
                  pl.BlockSpec((tm, D_MODEL), lambda i, e: (i, 0)),
                  pl.BlockSpec((te, D_MODEL), lambda i, e: (e, 0)),
                  pl.BlockSpec((D_MODEL, te), lambda i, e: (0, e)),
                  hspec, rspec, rspec, hspec, vec, vec],
        out_specs=pl.BlockSpec((tm, D_MODEL), lambda i, e: (i, 0)),
        out_shape=jax.ShapeDtypeStruct((t, D_MODEL), F32),
        scratch_shapes=[pltpu.VMEM((D_MODEL, tm), F32)] + [pltpu.VMEM((te, MXU_WIDTH), ht.dtype)] * (tm // MXU_WIDTH),
        compiler_params=_cparams(("parallel", "arbitrary")),
        name="peer_experts",
    )(ht, h1, u, vt, r2, c1, e1, e2, gain, bias)


def _rope_tables(pos):
    half = MLA_ROPE // 2
    inv_freq = ROPE_BASE ** (-jnp.arange(half, dtype=F32) / half)
    ang = pos.astype(F32)[:, None] * inv_freq[None, :]
    zeros = jnp.zeros((pos.shape[0], MLA_NOPE - MLA_ROPE), F32)
    cos, sin = jnp.cos(ang), jnp.sin(ang)
    return jnp.concatenate([cos, cos, zeros], axis=1), jnp.concatenate([sin, sin, zeros], axis=1)


def _rot_cols(w):
    half = MLA_ROPE // 2
    return jnp.concatenate([-w[..., half:], w[..., :half]], axis=-1)


def kernel(x, meta_tokens, hgrn_lb_logits, w_in, q_norm_gain, kv_norm_gain, w_uq, w_ukv, hgrn_norm_gain,
           w_proj_hgrn, w_proj_mla, w_out, ln1_gain, ln1_bias, peer_query, peer_sub_keys, peer_u, peer_v,
           ln2_gain, ln2_bias):
    batch, seq, d = x.shape
    p_rows = LEAD + seq
    t_real = batch * seq
    l = 0

    h_lead = jnp.concatenate([jnp.zeros((N_PAD, d), BF16), meta_tokens.astype(BF16)], axis=0)
    x2 = x.reshape(t_real, d)

    lower_bounds = jnp.cumsum(jax.nn.softmax(hgrn_lb_logits.astype(F32), axis=0), axis=0)
    rows = jnp.arange(p_rows)
    cos_all, sin_all = _rope_tables(rows - N_PAD)

    w = w_in[l]
    o_hg = 4 * HG_WIDTH
    o_cq = o_hg + MLA_Q_RANK
    o_ckv = o_cq + MLA_KV_RANK
    o_kr = o_ckv + MLA_ROPE
    w_kr = w[:, o_ckv:o_kr]
    zpad = jnp.zeros((d, MLA_NOPE - MLA_ROPE), F32)
    w_all = jnp.concatenate([w[:, :o_hg],
                             w[:, o_cq:o_ckv], w_kr, zpad, _rot_cols(w_kr), zpad,
                             w[:, o_hg:o_cq], jnp.zeros((d, MLA_Q_PAD - MLA_Q_RANK), F32),
                             w[:, o_kr:]], axis=1).astype(BF16)
    n_lead_cols = o_hg + MLA_KV_RANK + 2 * MLA_NOPE

    wq3 = w_uq[l].reshape(MLA_Q_RANK, MLA_HEADS, MLA_NOPE + MLA_ROPE)
    wq_rope = wq3[..., MLA_NOPE:]
    wq = jnp.concatenate([wq3[..., :MLA_NOPE], wq_rope, _rot_cols(wq_rope)], axis=-1)
    wq = wq.transpose(1, 0, 2).astype(BF16)
    wkv = w_ukv[l].reshape(MLA_KV_RANK, MLA_HEADS, MLA_NOPE + MLA_V).transpose(1, 0, 2).astype(BF16)

    proj = _matmul(x2, w_all, _pick(t_real, 1024), w_all.shape[1] // 2)
    proj_lead = _matmul(h_lead, w_all[:, :n_lead_cols], LEAD, n_lead_cols // 3)

    hg_o = _hgrn(proj, proj_lead, lower_bounds[l][None, :], hgrn_norm_gain[l][None, :], batch, seq)

    q = _qprep(proj, q_norm_gain[l][None, :], wq, cos_all[LEAD:], sin_all[LEAD:], batch, seq)
    k, v = _kvprep(proj, kv_norm_gain[l][None, :], wkv, cos_all[LEAD:], sin_all[LEAD:], batch, seq)
    k_lead, v_lead = _kvprep(proj_lead, kv_norm_gain[l][None, :], wkv, cos_all[:LEAD], sin_all[:LEAD], 1, LEAD)
    at_o = _flash(q, k, v, k_lead, v_lead, batch, seq)

    h1, h1t = _mix(x2, hg_o.reshape(t_real, HG_WIDTH), at_o.reshape(t_real, MLA_HEADS * MLA_V), proj,
                   w_proj_hgrn[l].astype(BF16), w_proj_mla[l].astype(BF16), w_out[l].astype(BF16),
                   ln1_gain[l][None, :], ln1_bias[l][None, :])

    wqt = peer_query[l].T.astype(BF16)
    sk = peer_sub_keys[l].reshape(PEER_HEADS * 2, PEER_NKEYS, PEER_HALF).astype(BF16)
    r2, c1, e1, e2 = _route(h1t, wqt, sk)

    out = _experts(h1t, h1, peer_u[l].astype(BF16), peer_v[l].T.astype(BF16), r2, c1, e1, e2,
                   ln2_gain[l][None, :], ln2_bias[l][None, :])
    return out.reshape(batch, seq, d)
```

```python
import functools

import numpy as np
import jax
import jax.numpy as jnp
from jax import lax
from jax.experimental import pallas as pl
from jax.experimental.pallas import tpu as pltpu

F32 = jnp.float32
BF16 = jnp.bfloat16

D_MODEL = 1024
DEPTH = 1
N_META = 16
LEAD = 128
N_PAD = LEAD - N_META

HG_HEADS = 8
HG_D = 128
HG_WIDTH = HG_HEADS * HG_D
HG_SUB = 16

MLA_HEADS = 16
MLA_NOPE = 128
MLA_ROPE = 64
MLA_V = 128
MLA_Q_RANK = 384
MLA_KV_RANK = 256
MLA_Q_PAD = 512
COL_KV = 4 * 1024 // 512
COL_CQ = COL_KV + 1
COL_GA = 5
COL_GB = 6
MLA_QK = 256
MLA_SCALE = (MLA_NOPE + MLA_ROPE) ** -0.5
MLA_VW = 256
ROPE_BASE = 10000.0
LOG2_E = 1.4426950408889634

PEER_HEADS = 8
PEER_NKEYS = 128
PEER_N = PEER_NKEYS * PEER_NKEYS
PEER_HALF = 128
PEER_TOPK = 16

ALPHA = (2 * DEPTH) ** 0.25
EPS = 1e-5
NEG = -1e30

VMEM_LIMIT = 56 * 1024 * 1024
MXU_WIDTH = 256


def _cparams(sem):
    return pltpu.CompilerParams(dimension_semantics=sem, vmem_limit_bytes=VMEM_LIMIT)


def _pick(n, pref):
    t = min(n, pref)
    while n % t:
        t -= 128
    return t


def _mm_kernel(x_ref, w_ref, o_ref):
    x = x_ref[...].astype(w_ref.dtype)
    o_ref[...] = jnp.dot(x, w_ref[...], preferred_element_type=F32).astype(o_ref.dtype)


def _matmul(x, w, tm, tn):
    m, k = x.shape
    out_dtype = w.dtype
    n = w.shape[1]
    return pl.pallas_call(
        _mm_kernel,
        grid=(m // tm, n // tn),
        in_specs=[pl.BlockSpec((tm, k), lambda i, j: (i, 0)),
                  pl.BlockSpec((k, tn), lambda i, j: (0, j))],
        out_specs=pl.BlockSpec((tm, tn), lambda i, j: (i, j)),
        out_shape=jax.ShapeDtypeStruct((m, n), out_dtype),
        compiler_params=_cparams(("parallel", "parallel")),
        name="in_proj",
    )(x, w)


def _hgrn_chunk(q, k, lf, iv, st_ref, tri, masks, chunk):
    width = q.shape[1]
    hi = lf.astype(BF16)
    r1 = lf - hi.astype(F32)
    mid = r1.astype(BF16)
    lo = (r1 - mid.astype(F32)).astype(BF16)
    g = (jnp.dot(tri, hi, preferred_element_type=F32)
         + jnp.dot(tri, mid, preferred_element_type=F32)
         + jnp.dot(tri, lo, preferred_element_type=F32))

    def bcast_row(r, n):
        return jnp.broadcast_to(g[r:r + 1, :], (n, width))

    def prev_end(s):
        parts = [jnp.zeros((s, width), F32)] + [bcast_row(b * s - 1, s) for b in range(1, chunk // s)]
        return parts[0] if len(parts) == 1 else jnp.concatenate(parts, axis=0)

    def own_end(s):
        parts = [bcast_row((b + 1) * s - 1, s) for b in range(chunk // s)]
        return parts[0] if len(parts) == 1 else jnp.concatenate(parts, axis=0)

    q_c = (q * jnp.exp(g)).astype(BF16)
    k_c = (k * jnp.exp(own_end(chunk) - g)).astype(BF16)
    dec = jnp.exp(g[chunk - 1:chunk, :])
    p16 = prev_end(HG_SUB)
    q_lv = [(q * jnp.exp(g - p16)).astype(BF16)]
    k_lv = [(k * jnp.exp(p16 - g)).astype(BF16)]
    s = HG_SUB
    while s < chunk:
        q_lv.append(q_lv[0] if s == HG_SUB else (q * jnp.exp(g - prev_end(s))).astype(BF16))
        k_lv.append((k * jnp.exp(own_end(s) - g)).astype(BF16))
        s *= 2

    nt = (((1,), (1,)), ((), ()))
    tn = (((0,), (0,)), ((), ()))
    heads = range(HG_HEADS)
    hs = [slice(h * HG_D, (h + 1) * HG_D) for h in heads]
    st = [st_ref[h] for h in heads]
    o_inter = [lax.dot_general(q_c[:, hs[h]], st[h].astype(BF16), nt, preferred_element_type=F32) for h in heads]
    upd = [lax.dot_general(iv[:, hs[h]], k_c[:, hs[h]], tn, preferred_element_type=F32) for h in heads]
    scores = [[lax.dot_general(ql[:, hs[h]], kl[:, hs[h]], nt, preferred_element_type=F32) for h in heads]
              for ql, kl in zip(q_lv, k_lv)]
    outs, st_new = [], []
    for h in heads:
        a = jnp.where(masks[0], scores[0][h], 0.0)
        for lvl in range(1, len(masks)):
            a = jnp.where(masks[lvl], scores[lvl][h], a)
        outs.append(o_inter[h] + jnp.dot(a.astype(BF16), iv[:, hs[h]], preferred_element_type=F32))
        st_new.append(st[h] * dec[:, hs[h]] + upd[h])
    return outs, st_new


def _hgrn_kernel(q_ref, f_ref, i_ref, g_ref, lb_ref, gain_ref, tri_ref, s0_ref, o_ref, s1_ref, st_ref, *,
                 chunk, rows, n_inert):
    c = pl.program_id(1)

    @pl.when(c == 0)
    def _():
        st_ref[...] = s0_ref[...]

    ri = lax.broadcasted_iota(jnp.int32, (chunk, chunk), 0)
    ci = lax.broadcasted_iota(jnp.int32, (chunk, chunk), 1)
    blk = lambda v, s: lax.shift_right_logical(v, s.bit_length() - 1)
    masks = [(blk(ri, HG_SUB) == blk(ci, HG_SUB)) & (ci <= ri)]
    s = HG_SUB
    while s < chunk:
        masks.append((blk(ri, 2 * s) == blk(ci, 2 * s)) & ((blk(ri, s) & 1) == 1) & ((blk(ci, s) & 1) == 0))
        s *= 2
    tri = tri_ref[...]
    lb = lb_ref[...]

    for n in range(rows // chunk):
        rs = slice(n * chunk, (n + 1) * chunk)
        f = lb + (1.0 - lb) * jax.nn.sigmoid(f_ref[rs, :].astype(F32))
        lf = jnp.log(f)
        k = 1.0 - f
        if n * chunk < n_inert:
            valid = n * chunk + lax.broadcasted_iota(jnp.int32, (chunk, HG_WIDTH), 0) >= n_inert
            lf = jnp.where(valid, lf, 0.0)
            k = jnp.where(valid, k, 0.0)
        outs, st_new = _hgrn_chunk(q_ref[rs, :].astype(F32), k, lf, i_ref[rs, :], st_ref, tri, masks, chunk)
        gate = g_ref[rs, :].astype(F32)
        gate = gate * jax.nn.sigmoid(gate)
        for h in range(HG_HEADS):
            hs = slice(h * HG_D, (h + 1) * HG_D)
            st_ref[h] = st_new[h]
            o = outs[h]
            ms = jnp.mean(o * o, axis=-1, keepdims=True)
            o = o * lax.rsqrt(ms + EPS) * gain_ref[:, hs]
            o_ref[0, rs, hs] = (o * gate[:, hs]).astype(o_ref.dtype)

    @pl.when(c == pl.num_programs(1) - 1)
    def _():
        s1_ref[0] = st_ref[...]


def _hgrn(proj, state, lb, gain, batch, seq, n_inert=0, chunk=128, rows=512):
    rows = min(rows, seq)
    chunk = min(chunk, rows)
    ns = seq // rows
    tri = jnp.asarray(np.tril(np.ones((chunk, chunk), np.float32)), BF16)
    col = lambda j: pl.BlockSpec((rows, HG_WIDTH), lambda b, c: (b * ns + c, j))
    vec = pl.BlockSpec((1, HG_WIDTH), lambda b, c: (0, 0))
    return pl.pallas_call(
        functools.partial(_hgrn_kernel, chunk=chunk, rows=rows, n_inert=n_inert),
        grid=(batch, ns),
        in_specs=[col(0), col(1), col(2), col(3), vec, vec,
                  pl.BlockSpec((chunk, chunk), lambda b, c: (0, 0)),
                  pl.BlockSpec((HG_HEADS, HG_D, HG_D), lambda b, c: (0, 0, 0))],
        out_specs=[pl.BlockSpec((1, rows, HG_WIDTH), lambda b, c: (b, c, 0)),
                   pl.BlockSpec((1, HG_HEADS, HG_D, HG_D), lambda b, c: (b, 0, 0, 0))],
        out_shape=[jax.ShapeDtypeStruct((batch, seq, HG_WIDTH), BF16),
                   jax.ShapeDtypeStruct((batch, HG_HEADS, HG_D, HG_D), F32)],
        scratch_shapes=[pltpu.VMEM((HG_HEADS, HG_D, HG_D), F32)],
        compiler_params=_cparams(("parallel", "arbitrary")),
        name="hgrn2",
    )(proj, proj, proj, proj, lb, gain, tri, state)


def _rms(x, gain):
    ms = jnp.mean(x * x, axis=-1, keepdims=True)
    return x * lax.rsqrt(ms + EPS) * gain


def _qprep_kernel(cq_ref, gain_ref, w_ref, cos_ref, sin_ref, o_ref):
    xn = _rms(cq_ref[:, :MLA_Q_RANK].astype(F32), gain_ref[...]).astype(BF16)
    cos = cos_ref[...]
    sin = sin_ref[...]
    for h in range(MLA_HEADS):
        y = jnp.dot(xn, w_ref[h], preferred_element_type=F32)
        o_ref[0, h, :, :MLA_NOPE] = (y[:, :MLA_NOPE] * (MLA_SCALE * LOG2_E)).astype(o_ref.dtype)
        t = y[:, MLA_NOPE:]
        roped = t * cos + pltpu.roll(t, MLA_ROPE, axis=1) * sin
        o_ref[0, h, :, MLA_NOPE:] = (roped * (MLA_SCALE * LOG2_E)).astype(o_ref.dtype)


def _qprep(qg, gain, wq, cos, sin, batch, seq):
    tm = _pick(seq, 1024)
    nb = seq // tm
    return pl.pallas_call(
        _qprep_kernel,
        grid=(batch, nb),
        in_specs=[pl.BlockSpec((tm, MLA_Q_PAD), lambda b, i: (b * nb + i, COL_CQ)),
                  pl.BlockSpec((1, MLA_Q_RANK), lambda b, i: (0, 0)),
                  pl.BlockSpec((MLA_HEADS, MLA_Q_RANK, MLA_QK), lambda b, i: (0, 0, 0)),
                  pl.BlockSpec((tm, MLA_NOPE), lambda b, i: (i, 0)),
                  pl.BlockSpec((tm, MLA_NOPE), lambda b, i: (i, 0))],
        out_specs=pl.BlockSpec((1, MLA_HEADS, tm, MLA_QK), lambda b, i: (b, 0, i, 0)),
        out_shape=jax.ShapeDtypeStruct((batch, MLA_HEADS, seq, MLA_QK), BF16),
        compiler_params=_cparams(("parallel", "parallel")),
        name="mla_q",
    )(qg, gain, wq, cos, sin)


def _kvprep_kernel(kv_ref, gain_ref, w_ref, cos_ref, sin_ref, k_ref, v_ref):
    x = kv_ref[...]
    xn = _rms(x[:, :MLA_KV_RANK].astype(F32), gain_ref[...]).astype(BF16)
    kr = (x[:, MLA_KV_RANK:MLA_KV_RANK + MLA_NOPE].astype(F32) * cos_ref[...]
          + x[:, MLA_KV_RANK + MLA_NOPE:].astype(F32) * sin_ref[...]).astype(k_ref.dtype)
    ones_col = (lax.broadcasted_iota(jnp.int32, (x.shape[0], MLA_VW - MLA_V), 1) == 0).astype(v_ref.dtype)
    for h in range(MLA_HEADS):
        y = jnp.dot(xn, w_ref[h], preferred_element_type=F32)
        k_ref[0, h, :, :MLA_NOPE] = y[:, :MLA_NOPE].astype(k_ref.dtype)
        k_ref[0, h, :, MLA_NOPE:] = kr
        v_ref[0, h, :, :MLA_V] = y[:, MLA_NOPE:].astype(v_ref.dtype)
        v_ref[0, h, :, MLA_V:] = ones_col


def _kvprep(proj, gain, wkv, cos, sin, batch, rows):
    tm = _pick(rows, 1024)
    nb = rows // tm
    return pl.pallas_call(
        _kvprep_kernel,
        grid=(batch, nb),
        in_specs=[pl.BlockSpec((tm, MLA_KV_RANK + 2 * MLA_NOPE), lambda b, i: (b * nb + i, COL_KV)),
                  pl.BlockSpec((1, MLA_KV_RANK), lambda b, i: (0, 0)),
                  pl.BlockSpec((MLA_HEADS, MLA_KV_RANK, MLA_NOPE + MLA_V), lambda b, i: (0, 0, 0)),
                  pl.BlockSpec((tm, MLA_NOPE), lambda b, i: (i, 0)),
                  pl.BlockSpec((tm, MLA_NOPE), lambda b, i: (i, 0))],
        out_specs=[pl.BlockSpec((1, MLA_HEADS, tm, MLA_QK), lambda b, i: (b, 0, i, 0)),
                   pl.BlockSpec((1, MLA_HEADS, tm, MLA_VW), lambda b, i: (b, 0, i, 0))],
        out_shape=[jax.ShapeDtypeStruct((batch, MLA_HEADS, rows, MLA_QK), BF16),
                   jax.ShapeDtypeStruct((batch, MLA_HEADS, rows, MLA_VW), BF16)],
        compiler_params=_cparams(("parallel", "parallel")),
        name="mla_kv",
    )(proj, gain, wkv, cos, sin)


def _flash_segments(seq, tq, tk):
    segs = []
    for qi in range(seq // tq):
        pos = 0
        while pos < qi * tq:
            n = min(tk, qi * tq - pos)
            segs.append((qi, False, pos, n))
            pos += n
        segs.append((qi, True, qi * tq, tq))
    return segs


def _flash_kernel(q_ref, k_ref, v_ref, kl_ref, vl_ref, o_ref, *, tq, tk):
    seq = q_ref.shape[2]
    nt = (((1,), (1,)), ((), ()))
    causal = (lax.broadcasted_iota(jnp.int32, (tq, tq), 1) <= lax.broadcasted_iota(jnp.int32, (tq, tq), 0))
    lead_mask = lax.broadcasted_iota(jnp.int32, (tq, LEAD), 1) >= N_PAD

    def scores(seg):
        qi, diag, k0, n = seg
        q = q_ref[0, 0, qi * tq:(qi + 1) * tq, :]
        s = lax.dot_general(q, k_ref[0, 0, k0:k0 + n, :], nt, preferred_element_type=F32)
        if diag:
            s_lead = lax.dot_general(q, kl_ref[0, 0], nt, preferred_element_type=F32)
            s = jnp.concatenate([jnp.where(causal, s, NEG), jnp.where(lead_mask, s_lead, NEG)], axis=1)
        return s

    segs = _flash_segments(seq, tq, tk)
    s_cur = scores(segs[0])
    m = acc = None
    for i, (qi, diag, k0, n) in enumerate(segs):
        s_next = scores(segs[i + 1]) if i + 1 < len(segs) else None
        first = k0 == 0
        m_new = jnp.max(s_cur, axis=-1, keepdims=True)
        if not first:
            m_new = jnp.maximum(m, m_new)
        p = jnp.exp2(s_cur - m_new).astype(BF16)
        if diag:
            pv = (jnp.dot(p[:, :tq], v_ref[0, 0, k0:k0 + n, :], preferred_element_type=F32)
                  + jnp.dot(p[:, tq:], vl_ref[0, 0], preferred_element_type=F32))
        else:
            pv = jnp.dot(p, v_ref[0, 0, k0:k0 + n, :], preferred_element_type=F32)
        acc = pv if first else jnp.exp2(m - m_new) * acc + pv
        m = m_new
        if diag:
            o_ref[0, qi * tq:(qi + 1) * tq, :] = (acc[:, :MLA_V] / acc[:, MLA_V:MLA_V + 1]).astype(o_ref.dtype)
        s_cur = s_next


def _flash(q, k, v, k_lead, v_lead, batch, seq, tq=256, tk=1024):
    tq = min(tq, seq)
    return pl.pallas_call(
        functools.partial(_flash_kernel, tq=tq, tk=tk),
        grid=(batch, MLA_HEADS),
        in_specs=[pl.BlockSpec((1, 1, seq, MLA_QK), lambda b, h: (b, h, 0, 0)),
                  pl.BlockSpec((1, 1, seq, MLA_QK), lambda b, h: (b, h, 0, 0)),
                  pl.BlockSpec((1, 1, seq, MLA_VW), lambda b, h: (b, h, 0, 0)),
                  pl.BlockSpec((1, 1, LEAD, MLA_QK), lambda b, h: (0, h, 0, 0)),
                  pl.BlockSpec((1, 1, LEAD, MLA_VW), lambda b, h: (0, h, 0, 0))],
        out_specs=pl.BlockSpec((1, seq, MLA_V), lambda b, h: (b, 0, h)),
        out_shape=jax.ShapeDtypeStruct((batch, seq, MLA_HEADS * MLA_V), BF16),
        compiler_params=_cparams(("parallel", "parallel")),
        name="mla_flash",
    )(q, k, v, k_lead, v_lead)


def _layer_norm(z, gain, bias):
    mu = jnp.mean(z, axis=-1, keepdims=True)
    zc = z - mu
    var = jnp.mean(zc * zc, axis=-1, keepdims=True)
    return zc * lax.rsqrt(var + EPS) * gain + bias


def _mix_kernel(x_ref, hg_ref, at_ref, ga_ref, gb_ref, wph_ref, wpm_ref, wo_ref, gain_ref, bias_ref,
                h_ref, ht_ref):
    ya = jnp.dot(hg_ref[...], wph_ref[...], preferred_element_type=F32)
    yb = jnp.dot(at_ref[...], wpm_ref[...], preferred_element_type=F32)
    mix = (jax.nn.sigmoid(ga_ref[...].astype(F32)) * ya + jax.nn.sigmoid(gb_ref[...].astype(F32)) * yb)
    mixed = jnp.dot(mix.astype(BF16), wo_ref[...], preferred_element_type=F32)
    h = _layer_norm(ALPHA * x_ref[...] + mixed, gain_ref[...], bias_ref[...])
    h_ref[...] = h
    ht_ref[...] = h.T.astype(ht_ref.dtype)


def _mix(x2, hg_o, at_o, gates, wph, wpm, wo, gain, bias):
    t = x2.shape[0]
    tm = _pick(t, 512)
    full = lambda a: pl.BlockSpec(a.shape, lambda i: (0,) * a.ndim)
    return pl.pallas_call(
        _mix_kernel,
        grid=(t // tm,),
        in_specs=[pl.BlockSpec((tm, D_MODEL), lambda i: (i, 0)),
                  pl.BlockSpec((tm, HG_WIDTH), lambda i: (i, 0)),
                  pl.BlockSpec((tm, MLA_HEADS * MLA_V), lambda i: (i, 0)),
                  pl.BlockSpec((tm, D_MODEL), lambda i: (i, COL_GA)),
                  pl.BlockSpec((tm, D_MODEL), lambda i: (i, COL_GB)),
                  full(wph), full(wpm), full(wo), full(gain), full(bias)],
        out_specs=[pl.BlockSpec((tm, D_MODEL), lambda i: (i, 0)),
                   pl.BlockSpec((D_MODEL, tm), lambda i: (0, i))],
        out_shape=[jax.ShapeDtypeStruct((t, D_MODEL), F32),
                   jax.ShapeDtypeStruct((D_MODEL, t), BF16)],
        compiler_params=_cparams(("parallel",)),
        name="mix_ln1",
    )(x2, hg_o, at_o, gates, gates, wph, wpm, wo, gain, bias)


N_TOP = PEER_TOPK + 1
SUBLANES = 8
TOP_ROWS = 24
PAIR_LIMIT = [N_TOP // (a + 1) for a in range(N_TOP)]
N_WIDE = sum(1 for n in PAIR_LIMIT if n > 1)
ROUTE_GROUP = 2


def _sort_network(n):
    def merge(lo, hi, r):
        step = r * 2
        if step < hi - lo:
            yield from merge(lo, hi, step)
            yield from merge(lo + r, hi, step)
            yield from [(i, i + r) for i in range(lo + r, hi - r, step)]
        else:
            yield (lo, lo + r)

    def sort(lo, hi):
        if hi - lo >= 1:
            mid = lo + (hi - lo) // 2
            yield from sort(lo, mid)
            yield from sort(mid + 1, hi)
            yield from merge(lo, hi, 1)

    return list(sort(0, n - 1))


def _sorted_slabs(slabs):
    n = 1 << (len(slabs) - 1).bit_length()
    v = list(slabs) + [None] * (n - len(slabs))
    for i, j in _sort_network(n):
        if v[j] is None:
            continue
        if v[i] is None:
            v[i], v[j] = v[j], None
        else:
            v[i], v[j] = jnp.maximum(v[i], v[j]), jnp.minimum(v[i], v[j])
    return v[:len(slabs)]


def _top_rows(slab_sets, n):
    sets = [list(c) for c in slab_sets]
    vals = [[] for _ in sets]
    for r in range(n):
        for i, cols in enumerate(sets):
            m = jnp.max(cols[0], axis=0, keepdims=True)
            vals[i].append(m)
            hit = cols[0] == m
            keep = min(len(cols), n - r - 1)
            sets[i] = [jnp.where(hit, cols[k + 1] if k + 1 < len(cols) else -jnp.inf, cols[k])
                       for k in range(keep)]
    return vals


def _count_above(rows, thr):
    count = jnp.zeros_like(thr)
    passed = []
    step = PEER_TOPK // 2
    while step >= 1:
        cands = [rows[base + step - 1] for base in range(0, PEER_TOPK, 2 * step)]
        for m in reversed(passed):
            cands = [jnp.where(m, cands[2 * i + 1], cands[2 * i]) for i in range(len(cands) // 2)]
        m = cands[0] > thr
        count = jnp.where(m, count + float(step), count)
        passed.append(m)
        step //= 2
    return jnp.where(rows[PEER_TOPK - 1] > thr, float(PEER_TOPK), count)


def _stack_rows(rows, tokens):
    ri = lax.broadcasted_iota(jnp.int32, (TOP_ROWS, tokens), 0)
    out = jnp.full((TOP_ROWS, tokens), -jnp.inf, F32)
    for r, v in enumerate(rows):
        out = jnp.where(ri == r, v, out)
    return out


def _route_kernel(ht_ref, wq_ref, sk_ref, r2_ref, c1_ref, e1_ref, e2_ref):
    tokens = ht_ref.shape[1]
    ri8 = lax.broadcasted_iota(jnp.int32, (N_WIDE, tokens), 0)
    qp = jnp.dot(wq_ref[...], ht_ref[...], preferred_element_type=F32).astype(BF16)
    for h0 in range(0, PEER_HEADS, ROUTE_GROUP):
        heads = range(h0, h0 + ROUTE_GROUP)
        s = [jnp.dot(sk_ref[hp], qp[hp * PEER_HALF:(hp + 1) * PEER_HALF], preferred_element_type=F32)
             for hp in range(2 * h0, 2 * (h0 + ROUTE_GROUP))]
        slabs = lambda x: [x[k:k + SUBLANES] for k in range(0, x.shape[0], SUBLANES)]
        tops = _top_rows([_sorted_slabs(slabs(x)) for x in s], N_TOP)
        cand = []
        for g in range(ROUTE_GROUP):
            t1, t2 = tops[2 * g], tops[2 * g + 1]
            t1s = _stack_rows(t1, tokens)
            t2s = _stack_rows(t2, tokens)
            parts = [jnp.where(ri8 < min(PAIR_LIMIT[a], N_WIDE), t1[a] + t2s[:N_WIDE], -jnp.inf)
                     for a in range(N_WIDE)]
            parts += slabs(t1[0] + t2s[N_WIDE:]) + slabs(t1s[N_WIDE:] + t2[0])
            cand.append(_sorted_slabs(parts))
        bests = _top_rows(cand, N_TOP)
        for g, h in enumerate(heads):
            s1, s2 = s[2 * g], s[2 * g + 1]
            t1, t2, best = tops[2 * g], tops[2 * g + 1], bests[g]
            m = best[0]
            z = jnp.zeros_like(m)
            for r in range(PEER_TOPK):
                z = z + jnp.exp(best[r] - m)
            thr = 0.5 * (best[PEER_TOPK - 1] + best[PEER_TOPK]) - s1
            c1 = _count_above(t2, thr)
            r2_ref[h] = _count_above(t2, s2).astype(r2_ref.dtype)
            c1_ref[h] = c1
            e1_ref[h] = jnp.exp(s1 - t1[0] - jnp.log(z))
            e2_ref[h] = jnp.exp(s2 - t2[0]).astype(e2_ref.dtype)


def _route(ht, wqt, sk):
    t = ht.shape[1]
    tm = _pick(t, 256)
    out = jax.ShapeDtypeStruct((PEER_HEADS, PEER_NKEYS, t), ht.dtype)
    out32 = jax.ShapeDtypeStruct((PEER_HEADS, PEER_NKEYS, t), F32)
    ospec = pl.BlockSpec((PEER_HEADS, PEER_NKEYS, tm), lambda i: (0, 0, i))
    return pl.pallas_call(
        _route_kernel,
        grid=(t // tm,),
        in_specs=[pl.BlockSpec((D_MODEL, tm), lambda i: (0, i)),
                  pl.BlockSpec(wqt.shape, lambda i: (0, 0)),
                  pl.BlockSpec(sk.shape, lambda i: (0, 0, 0))],
        out_specs=[ospec] * 4,
        out_shape=[out, out32, out32, out],
        compiler_params=_cparams(("parallel",)),
        name="peer_route",
    )(ht, wqt, sk)


GELU_C0 = 0.7978845608028654
GELU_C1 = GELU_C0 * 0.044715


def _gelu(a):
    t = jnp.exp2(a * ((-2.0 * LOG2_E * GELU_C0) + (-2.0 * LOG2_E * GELU_C1) * (a * a)))
    return a / (1.0 + t)


def _expert_kernel(ht_ref, h_ref, u_ref, vt_ref, r2_ref, c1_ref, e1_ref, e2_ref, gain_ref, bias_ref,
                   o_ref, acc_ref, *act_refs, chunk):
    e = pl.program_id(1)
    te, width = act_refs[0].shape
    dtype = act_refs[0].dtype
    packed_rows = 16

    @pl.when(e == 0)
    def _():
        acc_ref[...] = jnp.zeros_like(acc_ref)

    def row_bcast(ref, h, r, ls):
        row = jnp.broadcast_to(ref[h, r:r + 1, ls], (packed_rows, width)).astype(dtype)
        return pltpu.repeat(row, PEER_NKEYS // packed_rows, axis=0)

    lanes = [slice(i * width, (i + 1) * width) for i in range(len(act_refs))]
    a_all = [[jnp.dot(u_ref[c * chunk:(c + 1) * chunk, :], ht_ref[:, ls], preferred_element_type=F32)
              for c in range(te // chunk)] for ls in lanes]
    for act_ref, ls, a_chunks in zip(act_refs, lanes, a_all):
        for c, a in enumerate(a_chunks):
            for rr in range(chunk // PEER_NKEYS):
                r = c * (chunk // PEER_NKEYS) + rr
                g = None
                for h in range(PEER_HEADS):
                    gate = jnp.where(r2_ref[h, :, ls] < row_bcast(c1_ref, h, r, ls),
                                     e2_ref[h, :, ls] * row_bcast(e1_ref, h, r, ls), 0.0)
                    g = gate if g is None else g + gate
                a_r = a[rr * PEER_NKEYS:(rr + 1) * PEER_NKEYS, :]
                act_ref[r * PEER_NKEYS:(r + 1) * PEER_NKEYS, :] = _gelu(a_r.astype(dtype)) * g
        acc_ref[:, ls] += jnp.dot(vt_ref[...], act_ref[...], preferred_element_type=F32)

    @pl.when(e == pl.num_programs(1) - 1)
    def _():
        z = ALPHA * h_ref[...] + acc_ref[...].T
        o_ref[...] = _layer_norm(z, gain_ref[...], bias_ref[...])


def _experts(ht, h1, u, vt, r2, c1, e1, e2, gain, bias, tm=512, te=2048, chunk=512):
    t = h1.shape[0]
    tm = _pick(t, tm)
    hspec = pl.BlockSpec((PEER_HEADS, PEER_NKEYS, tm), lambda i, e: (0, 0, i))
    rspec = pl.BlockSpec((PEER_HEADS, te // PEER_NKEYS, tm), lambda i, e: (0, e, i))
    vec = pl.BlockSpec((1, D_MODEL), lambda i, e: (0, 0))
    return pl.pallas_call(
        functools.partial(_expert_kernel, chunk=chunk),
        grid=(t // tm, PEER_N // te),
        in_specs=[pl.BlockSpec((D_MODEL, tm), lambda i, e: (0, i)),
                  pl.BlockSpec((tm, D_MODEL), lambda i, e: (i, 0)),
                  pl.BlockSpec((te, D_MODEL), lambda i, e: (e, 0)),
                  pl.BlockSpec((D_MODEL, te), lambda i, e: (0, e)),
                  hspec, rspec, rspec, hspec, vec, vec],
        out_specs=pl.BlockSpec((tm, D_MODEL), lambda i, e: (i, 0)),
        out_shape=jax.ShapeDtypeStruct((t, D_MODEL), F32),
        scratch_shapes=[pltpu.VMEM((D_MODEL, tm), F32)] + [pltpu.VMEM((te, MXU_WIDTH), ht.dtype)] * (tm // MXU_WIDTH),
        compiler_params=_cparams(("parallel", "arbitrary")),
        name="peer_experts",
    )(ht, h1, u, vt, r2, c1, e1, e2, gain, bias)


def _rope_tables(pos):
    half = MLA_ROPE // 2
    inv_freq = ROPE_BASE ** (-jnp.arange(half, dtype=F32) / half)
    ang = pos.astype(F32)[:, None] * inv_freq[None, :]
    zeros = jnp.zeros((pos.shape[0], MLA_NOPE - MLA_ROPE), F32)
    cos, sin = jnp.cos(ang), jnp.sin(ang)
    return jnp.concatenate([cos, cos, zeros], axis=1), jnp.concatenate([sin, sin, zeros], axis=1)


def _rot_cols(w):
    half = MLA_ROPE // 2
    return jnp.concatenate([-w[..., half:], w[..., :half]], axis=-1)


def kernel(x, meta_tokens, hgrn_lb_logits, w_in, q_norm_gain, kv_norm_gain, w_uq, w_ukv, hgrn_norm_gain,
           w_proj_hgrn, w_proj_mla, w_out, ln1_gain, ln1_bias, peer_query, peer_sub_keys, peer_u, peer_v,
           ln2_gain, ln2_bias):
    batch, seq, d = x.shape
    p_rows = LEAD + seq
    t_real = batch * seq
    l = 0

    h_lead = jnp.concatenate([jnp.zeros((N_PAD, d), BF16), meta_tokens.astype(BF16)], axis=0)
    x2 = x.reshape(t_real, d)

    lower_bounds = jnp.cumsum(jax.nn.softmax(hgrn_lb_logits.astype(F32), axis=0), axis=0)
    rows = jnp.arange(p_rows)
    cos_all, sin_all = _rope_tables(rows - N_PAD)

    w = w_in[l]
    o_hg = 4 * HG_WIDTH
    o_cq = o_hg + MLA_Q_RANK
    o_ckv = o_cq + MLA_KV_RANK
    o_kr = o_ckv + MLA_ROPE
    w_kr = w[:, o_ckv:o_kr]
    zpad = jnp.zeros((d, MLA_NOPE - MLA_ROPE), F32)
    w_all = jnp.concatenate([w[:, :o_hg],
                             w[:, o_cq:o_ckv], w_kr, zpad, _rot_cols(w_kr), zpad,
                             w[:, o_hg:o_cq], jnp.zeros((d, MLA_Q_PAD - MLA_Q_RANK), F32),
                             w[:, o_kr:]], axis=1).astype(BF16)
    n_lead_cols = o_hg + MLA_KV_RANK + 2 * MLA_NOPE

    wq3 = w_uq[l].reshape(MLA_Q_RANK, MLA_HEADS, MLA_NOPE + MLA_ROPE)
    wq_rope = wq3[..., MLA_NOPE:]
    wq = jnp.concatenate([wq3[..., :MLA_NOPE], wq_rope, _rot_cols(wq_rope)], axis=-1)
    wq = wq.transpose(1, 0, 2).astype(BF16)
    wkv = w_ukv[l].reshape(MLA_KV_RANK, MLA_HEADS, MLA_NOPE + MLA_V).transpose(1, 0, 2).astype(BF16)

    proj = _matmul(x2, w_all, _pick(t_real, 1024), w_all.shape[1] // 2)
    proj_lead = _matmul(h_lead, w_all[:, :n_lead_cols], LEAD, n_lead_cols // 3)

    lb, hg_gain = lower_bounds[l][None, :], hgrn_norm_gain[l][None, :]
    _, lead_state = _hgrn(proj_lead, jnp.zeros((HG_HEADS, HG_D, HG_D), F32), lb, hg_gain, 1, LEAD, n_inert=N_PAD)
    hg_o, _ = _hgrn(proj, lead_state[0], lb, hg_gain, batch, seq)

    q = _qprep(proj, q_norm_gain[l][None, :], wq, cos_all[LEAD:], sin_all[LEAD:], batch, seq)
    k, v = _kvprep(proj, kv_norm_gain[l][None, :], wkv, cos_all[LEAD:], sin_all[LEAD:], batch, seq)
    k_lead, v_lead = _kvprep(proj_lead, kv_norm_gain[l][None, :], wkv, cos_all[:LEAD], sin_all[:LEAD], 1, LEAD)
    at_o = _flash(q, k, v, k_lead, v_lead, batch, seq)

    h1, h1t = _mix(x2, hg_o.reshape(t_real, HG_WIDTH), at_o.reshape(t_real, MLA_HEADS * MLA_V), proj,
                   w_proj_hgrn[l].astype(BF16), w_proj_mla[l].astype(BF16), w_out[l].astype(BF16),
                   ln1_gain[l][None, :], ln1_bias[l][None, :])

    wqt = peer_query[l].T.astype(BF16)
    sk = peer_sub_keys[l].reshape(PEER_HEADS * 2, PEER_NKEYS, PEER_HALF).astype(BF16)
    r2, c1, e1, e2 = _route(h1t, wqt, sk)

    out = _experts(h1t, h1, peer_u[l].astype(BF16), peer_v[l].T.astype(BF16), r2, c1, e1, e2,
                   ln2_gain[l][None, :], ln2_bias[l][None, :])
    return out.reshape(batch, seq, d)
```

```python
import functools

import numpy as np
import jax
import jax.numpy as jnp
from jax import lax
from jax.experimental import pallas as pl
from jax.experimental.pallas import tpu as pltpu

F32 = jnp.float32
BF16 = jnp.bfloat16

D_MODEL = 1024
DEPTH = 1
N_META = 16
LEAD = 128
N_PAD = LEAD - N_META

HG_HEADS = 8
HG_D = 128
HG_WIDTH = HG_HEADS * HG_D
HG_SUB = 16

MLA_HEADS = 16
MLA_NOPE = 128
MLA_ROPE = 64
MLA_V = 128
MLA_Q_RANK = 384
MLA_KV_RANK = 256
MLA_Q_PAD = 512
COL_KV = 4 * 1024 // 512
COL_CQ = COL_KV + 1
COL_GA = 5
COL_GB = 6
MLA_QK = 256
MLA_SCALE = (MLA_NOPE + MLA_ROPE) ** -0.5
MLA_VW = 256
ROPE_BASE = 10000.0
LOG2_E = 1.4426950408889634

PEER_HEADS = 8
PEER_NKEYS = 128
PEER_N = PEER_NKEYS * PEER_NKEYS
PEER_HALF = 128
PEER_TOPK = 16

ALPHA = (2 * DEPTH) ** 0.25
EPS = 1e-5
NEG = -1e30

VMEM_LIMIT = 56 * 1024 * 1024
MXU_WIDTH = 256


def _cparams(sem):
    return pltpu.CompilerParams(dimension_semantics=sem, vmem_limit_bytes=VMEM_LIMIT)


def _pick(n, pref):
    t = min(n, pref)
    while n % t:
        t -= 128
    return t


def _mm_kernel(x_ref, w_ref, o_ref):
    x = x_ref[...].astype(w_ref.dtype)
    o_ref[...] = jnp.dot(x, w_ref[...], preferred_element_type=F32).astype(o_ref.dtype)


def _matmul(x, w, tm, tn):
    m, k = x.shape
    out_dtype = w.dtype
    n = w.shape[1]
    return pl.pallas_call(
        _mm_kernel,
        grid=(m // tm, n // tn),
        in_specs=[pl.BlockSpec((tm, k), lambda i, j: (i, 0)),
                  pl.BlockSpec((k, tn), lambda i, j: (0, j))],
        out_specs=pl.BlockSpec((tm, tn), lambda i, j: (i, j)),
        out_shape=jax.ShapeDtypeStruct((m, n), out_dtype),
        compiler_params=_cparams(("parallel", "parallel")),
        name="in_proj",
    )(x, w)


def _hgrn_chunk(q, k, lf, iv, st_ref, tri, masks, chunk):
    width = q.shape[1]
    hi = lf.astype(BF16)
    r1 = lf - hi.astype(F32)
    mid = r1.astype(BF16)
    lo = (r1 - mid.astype(F32)).astype(BF16)
    g = (jnp.dot(tri, hi, preferred_element_type=F32)
         + jnp.dot(tri, mid, preferred_element_type=F32)
         + jnp.dot(tri, lo, preferred_element_type=F32))

    def bcast_row(r, n):
        return jnp.broadcast_to(g[r:r + 1, :], (n, width))

    def prev_end(s):
        parts = [jnp.zeros((s, width), F32)] + [bcast_row(b * s - 1, s) for b in range(1, chunk // s)]
        return parts[0] if len(parts) == 1 else jnp.concatenate(parts, axis=0)

    def own_end(s):
        parts = [bcast_row((b + 1) * s - 1, s) for b in range(chunk // s)]
        return parts[0] if len(parts) == 1 else jnp.concatenate(parts, axis=0)

    q_c = (q * jnp.exp(g)).astype(BF16)
    k_c = (k * jnp.exp(own_end(chunk) - g)).astype(BF16)
    dec = jnp.exp(g[chunk - 1:chunk, :])
    p16 = prev_end(HG_SUB)
    q_lv = [(q * jnp.exp(g - p16)).astype(BF16)]
    k_lv = [(k * jnp.exp(p16 - g)).astype(BF16)]
    s = HG_SUB
    while s < chunk:
        q_lv.append(q_lv[0] if s == HG_SUB else (q * jnp.exp(g - prev_end(s))).astype(BF16))
        k_lv.append((k * jnp.exp(own_end(s) - g)).astype(BF16))
        s *= 2

    nt = (((1,), (1,)), ((), ()))
    tn = (((0,), (0,)), ((), ()))
    heads = range(HG_HEADS)
    hs = [slice(h * HG_D, (h + 1) * HG_D) for h in heads]
    st = [st_ref[h] for h in heads]
    o_inter = [lax.dot_general(q_c[:, hs[h]], st[h].astype(BF16), nt, preferred_element_type=F32) for h in heads]
    upd = [lax.dot_general(iv[:, hs[h]], k_c[:, hs[h]], tn, preferred_element_type=F32) for h in heads]
    scores = [[lax.dot_general(ql[:, hs[h]], kl[:, hs[h]], nt, preferred_element_type=F32) for h in heads]
              for ql, kl in zip(q_lv, k_lv)]
    outs, st_new = [], []
    for h in heads:
        a = jnp.where(masks[0], scores[0][h], 0.0)
        for lvl in range(1, len(masks)):
            a = jnp.where(masks[lvl], scores[lvl][h], a)
        outs.append(o_inter[h] + jnp.dot(a.astype(BF16), iv[:, hs[h]], preferred_element_type=F32))
        st_new.append(st[h] * dec[:, hs[h]] + upd[h])
    return outs, st_new


def _hgrn_kernel(q_ref, f_ref, i_ref, g_ref, lb_ref, gain_ref, tri_ref, s0_ref, o_ref, s1_ref, st_ref, *,
                 chunk, rows, n_inert):
    c = pl.program_id(1)

    @pl.when(c == 0)
    def _():
        st_ref[...] = s0_ref[...]

    ri = lax.broadcasted_iota(jnp.int32, (chunk, chunk), 0)
    ci = lax.broadcasted_iota(jnp.int32, (chunk, chunk), 1)
    blk = lambda v, s: lax.shift_right_logical(v, s.bit_length() - 1)
    masks = [(blk(ri, HG_SUB) == blk(ci, HG_SUB)) & (ci <= ri)]
    s = HG_SUB
    while s < chunk:
        masks.append((blk(ri, 2 * s) == blk(ci, 2 * s)) & ((blk(ri, s) & 1) == 1) & ((blk(ci, s) & 1) == 0))
        s *= 2
    tri = tri_ref[...]
    lb = lb_ref[...]

    for n in range(rows // chunk):
        rs = slice(n * chunk, (n + 1) * chunk)
        f = lb + (1.0 - lb) * jax.nn.sigmoid(f_ref[rs, :].astype(F32))
        lf = jnp.log(f)
        k = 1.0 - f
        if n * chunk < n_inert:
            valid = n * chunk + lax.broadcasted_iota(jnp.int32, (chunk, HG_WIDTH), 0) >= n_inert
            lf = jnp.where(valid, lf, 0.0)
            k = jnp.where(valid, k, 0.0)
        outs, st_new = _hgrn_chunk(q_ref[rs, :].astype(F32), k, lf, i_ref[rs, :], st_ref, tri, masks, chunk)
        gate = g_ref[rs, :].astype(F32)
        gate = gate * jax.nn.sigmoid(gate)
        for h in range(HG_HEADS):
            hs = slice(h * HG_D, (h + 1) * HG_D)
            st_ref[h] = st_new[h]
            o = outs[h]
            ms = jnp.mean(o * o, axis=-1, keepdims=True)
            o = o * lax.rsqrt(ms + EPS) * gain_ref[:, hs]
            o_ref[0, rs, hs] = (o * gate[:, hs]).astype(o_ref.dtype)

    @pl.when(c == pl.num_programs(1) - 1)
    def _():
        s1_ref[0] = st_ref[...]


def _hgrn(proj, state, lb, gain, batch, seq, n_inert=0, chunk=128, rows=1024):
    rows = min(rows, seq)
    chunk = min(chunk, rows)
    ns = seq // rows
    tri = jnp.asarray(np.tril(np.ones((chunk, chunk), np.float32)), BF16)
    col = lambda j: pl.BlockSpec((rows, HG_WIDTH), lambda b, c: (b * ns + c, j))
    vec = pl.BlockSpec((1, HG_WIDTH), lambda b, c: (0, 0))
    return pl.pallas_call(
        functools.partial(_hgrn_kernel, chunk=chunk, rows=rows, n_inert=n_inert),
        grid=(batch, ns),
        in_specs=[col(0), col(1), col(2), col(3), vec, vec,
                  pl.BlockSpec((chunk, chunk), lambda b, c: (0, 0)),
                  pl.BlockSpec((HG_HEADS, HG_D, HG_D), lambda b, c: (0, 0, 0))],
        out_specs=[pl.BlockSpec((1, rows, HG_WIDTH), lambda b, c: (b, c, 0)),
                   pl.BlockSpec((1, HG_HEADS, HG_D, HG_D), lambda b, c: (b, 0, 0, 0))],
        out_shape=[jax.ShapeDtypeStruct((batch, seq, HG_WIDTH), BF16),
                   jax.ShapeDtypeStruct((batch, HG_HEADS, HG_D, HG_D), F32)],
        scratch_shapes=[pltpu.VMEM((HG_HEADS, HG_D, HG_D), F32)],
        compiler_params=_cparams(("parallel", "arbitrary")),
        name="hgrn2",
    )(proj, proj, proj, proj, lb, gain, tri, state)


def _rms(x, gain):
    ms = jnp.mean(x * x, axis=-1, keepdims=True)
    return x * lax.rsqrt(ms + EPS) * gain


def _qprep_kernel(cq_ref, gain_ref, w_ref, cos_ref, sin_ref, o_ref):
    xn = _rms(cq_ref[:, :MLA_Q_RANK].astype(F32), gain_ref[...]).astype(BF16)
    cos = cos_ref[...]
    sin = sin_ref[...]
    for h in range(MLA_HEADS):
        y = jnp.dot(xn, w_ref[h], preferred_element_type=F32)
        o_ref[0, h, :, :MLA_NOPE] = (y[:, :MLA_NOPE] * (MLA_SCALE * LOG2_E)).astype(o_ref.dtype)
        t = y[:, MLA_NOPE:]
        roped = t * cos + pltpu.roll(t, MLA_ROPE, axis=1) * sin
        o_ref[0, h, :, MLA_NOPE:] = (roped * (MLA_SCALE * LOG2_E)).astype(o_ref.dtype)


def _qprep(qg, gain, wq, cos, sin, batch, seq):
    tm = _pick(seq, 1024)
    nb = seq // tm
    return pl.pallas_call(
        _qprep_kernel,
        grid=(batch, nb),
        in_specs=[pl.BlockSpec((tm, MLA_Q_PAD), lambda b, i: (b * nb + i, COL_CQ)),
                  pl.BlockSpec((1, MLA_Q_RANK), lambda b, i: (0, 0)),
                  pl.BlockSpec((MLA_HEADS, MLA_Q_RANK, MLA_QK), lambda b, i: (0, 0, 0)),
                  pl.BlockSpec((tm, MLA_NOPE), lambda b, i: (i, 0)),
                  pl.BlockSpec((tm, MLA_NOPE), lambda b, i: (i, 0))],
        out_specs=pl.BlockSpec((1, MLA_HEADS, tm, MLA_QK), lambda b, i: (b, 0, i, 0)),
        out_shape=jax.ShapeDtypeStruct((batch, MLA_HEADS, seq, MLA_QK), BF16),
        compiler_params=_cparams(("parallel", "parallel")),
        name="mla_q",
    )(qg, gain, wq, cos, sin)


def _kvprep_kernel(kv_ref, gain_ref, w_ref, cos_ref, sin_ref, k_ref, v_ref):
    x = kv_ref[...]
    xn = _rms(x[:, :MLA_KV_RANK].astype(F32), gain_ref[...]).astype(BF16)
    kr = (x[:, MLA_KV_RANK:MLA_KV_RANK + MLA_NOPE].astype(F32) * cos_ref[...]
          + x[:, MLA_KV_RANK + MLA_NOPE:].astype(F32) * sin_ref[...]).astype(k_ref.dtype)
    ones_col = (lax.broadcasted_iota(jnp.int32, (x.shape[0], MLA_VW - MLA_V), 1) == 0).astype(v_ref.dtype)
    for h in range(MLA_HEADS):
        y = jnp.dot(xn, w_ref[h], preferred_element_type=F32)
        k_ref[0, h, :, :MLA_NOPE] = y[:, :MLA_NOPE].astype(k_ref.dtype)
        k_ref[0, h, :, MLA_NOPE:] = kr
        v_ref[0, h, :, :MLA_V] = y[:, MLA_NOPE:].astype(v_ref.dtype)
        v_ref[0, h, :, MLA_V:] = ones_col


def _kvprep(proj, gain, wkv, cos, sin, batch, rows):
    tm = _pick(rows, 1024)
    nb = rows // tm
    return pl.pallas_call(
        _kvprep_kernel,
        grid=(batch, nb),
        in_specs=[pl.BlockSpec((tm, MLA_KV_RANK + 2 * MLA_NOPE), lambda b, i: (b * nb + i, COL_KV)),
                  pl.BlockSpec((1, MLA_KV_RANK), lambda b, i: (0, 0)),
                  pl.BlockSpec((MLA_HEADS, MLA_KV_RANK, MLA_NOPE + MLA_V), lambda b, i: (0, 0, 0)),
                  pl.BlockSpec((tm, MLA_NOPE), lambda b, i: (i, 0)),
                  pl.BlockSpec((tm, MLA_NOPE), lambda b, i: (i, 0))],
        out_specs=[pl.BlockSpec((1, MLA_HEADS, tm, MLA_QK), lambda b, i: (b, 0, i, 0)),
                   pl.BlockSpec((1, MLA_HEADS, tm, MLA_VW), lambda b, i: (b, 0, i, 0))],
        out_shape=[jax.ShapeDtypeStruct((batch, MLA_HEADS, rows, MLA_QK), BF16),
                   jax.ShapeDtypeStruct((batch, MLA_HEADS, rows, MLA_VW), BF16)],
        compiler_params=_cparams(("parallel", "parallel")),
        name="mla_kv",
    )(proj, gain, wkv, cos, sin)


def _flash_segments(seq, tq, tk):
    segs = []
    for qi in range(seq // tq):
        pos = 0
        while pos < qi * tq:
            n = min(tk, qi * tq - pos)
            segs.append((qi, False, pos, n))
            pos += n
        segs.append((qi, True, qi * tq, tq))
    return segs


def _flash_kernel(q_ref, k_ref, v_ref, kl_ref, vl_ref, o_ref, *, tq, tk):
    seq = q_ref.shape[2]
    nt = (((1,), (1,)), ((), ()))
    causal = (lax.broadcasted_iota(jnp.int32, (tq, tq), 1) <= lax.broadcasted_iota(jnp.int32, (tq, tq), 0))
    lead_mask = lax.broadcasted_iota(jnp.int32, (tq, LEAD), 1) >= N_PAD

    def scores(seg):
        qi, diag, k0, n = seg
        q = q_ref[0, 0, qi * tq:(qi + 1) * tq, :]
        s = lax.dot_general(q, k_ref[0, 0, k0:k0 + n, :], nt, preferred_element_type=F32)
        if diag:
            s_lead = lax.dot_general(q, kl_ref[0, 0], nt, preferred_element_type=F32)
            s = jnp.concatenate([jnp.where(causal, s, NEG), jnp.where(lead_mask, s_lead, NEG)], axis=1)
        return s

    segs = _flash_segments(seq, tq, tk)
    s_cur = scores(segs[0])
    m = acc = None
    for i, (qi, diag, k0, n) in enumerate(segs):
        s_next = scores(segs[i + 1]) if i + 1 < len(segs) else None
        first = k0 == 0
        m_new = jnp.max(s_cur, axis=-1, keepdims=True)
        if not first:
            m_new = jnp.maximum(m, m_new)
        p = jnp.exp2(s_cur - m_new).astype(BF16)
        if diag:
            pv = (jnp.dot(p[:, :tq], v_ref[0, 0, k0:k0 + n, :], preferred_element_type=F32)
                  + jnp.dot(p[:, tq:], vl_ref[0, 0], preferred_element_type=F32))
        else:
            pv = jnp.dot(p, v_ref[0, 0, k0:k0 + n, :], preferred_element_type=F32)
        acc = pv if first else jnp.exp2(m - m_new) * acc + pv
        m = m_new
        if diag:
            o_ref[0, qi * tq:(qi + 1) * tq, :] = (acc[:, :MLA_V] / acc[:, MLA_V:MLA_V + 1]).astype(o_ref.dtype)
        s_cur = s_next


def _flash(q, k, v, k_lead, v_lead, batch, seq, tq=256, tk=1024):
    tq = min(tq, seq)
    return pl.pallas_call(
        functools.partial(_flash_kernel, tq=tq, tk=tk),
        grid=(batch, MLA_HEADS),
        in_specs=[pl.BlockSpec((1, 1, seq, MLA_QK), lambda b, h: (b, h, 0, 0)),
                  pl.BlockSpec((1, 1, seq, MLA_QK), lambda b, h: (b, h, 0, 0)),
                  pl.BlockSpec((1, 1, seq, MLA_VW), lambda b, h: (b, h, 0, 0)),
                  pl.BlockSpec((1, 1, LEAD, MLA_QK), lambda b, h: (0, h, 0, 0)),
                  pl.BlockSpec((1, 1, LEAD, MLA_VW), lambda b, h: (0, h, 0, 0))],
        out_specs=pl.BlockSpec((1, seq, MLA_V), lambda b, h: (b, 0, h)),
        out_shape=jax.ShapeDtypeStruct((batch, seq, MLA_HEADS * MLA_V), BF16),
        compiler_params=_cparams(("parallel", "parallel")),
        name="mla_flash",
    )(q, k, v, k_lead, v_lead)


def _layer_norm(z, gain, bias):
    mu = jnp.mean(z, axis=-1, keepdims=True)
    zc = z - mu
    var = jnp.mean(zc * zc, axis=-1, keepdims=True)
    return zc * lax.rsqrt(var + EPS) * gain + bias


def _mix_kernel(x_ref, hg_ref, at_ref, ga_ref, gb_ref, wph_ref, wpm_ref, wo_ref, gain_ref, bias_ref,
                h_ref, ht_ref):
    ya = jnp.dot(hg_ref[...], wph_ref[...], preferred_element_type=F32)
    yb = jnp.dot(at_ref[...], wpm_ref[...], preferred_element_type=F32)
    mix = (jax.nn.sigmoid(ga_ref[...].astype(F32)) * ya + jax.nn.sigmoid(gb_ref[...].astype(F32)) * yb)
    mixed = jnp.dot(mix.astype(BF16), wo_ref[...], preferred_element_type=F32)
    h = _layer_norm(ALPHA * x_ref[...] + mixed, gain_ref[...], bias_ref[...])
    h_ref[...] = h
    ht_ref[...] = h.T.astype(ht_ref.dtype)


def _mix(x2, hg_o, at_o, gates, wph, wpm, wo, gain, bias):
    t = x2.shape[0]
    tm = _pick(t, 512)
    full = lambda a: pl.BlockSpec(a.shape, lambda i: (0,) * a.ndim)
    return pl.pallas_call(
        _mix_kernel,
        grid=(t // tm,),
        in_specs=[pl.BlockSpec((tm, D_MODEL), lambda i: (i, 0)),
                  pl.BlockSpec((tm, HG_WIDTH), lambda i: (i, 0)),
                  pl.BlockSpec((tm, MLA_HEADS * MLA_V), lambda i: (i, 0)),
                  pl.BlockSpec((tm, D_MODEL), lambda i: (i, COL_GA)),
                  pl.BlockSpec((tm, D_MODEL), lambda i: (i, COL_GB)),
                  full(wph), full(wpm), full(wo), full(gain), full(bias)],
        out_specs=[pl.BlockSpec((tm, D_MODEL), lambda i: (i, 0)),
                   pl.BlockSpec((D_MODEL, tm), lambda i: (0, i))],
        out_shape=[jax.ShapeDtypeStruct((t, D_MODEL), F32),
                   jax.ShapeDtypeStruct((D_MODEL, t), BF16)],
        compiler_params=_cparams(("parallel",)),
        name="mix_ln1",
    )(x2, hg_o, at_o, gates, gates, wph, wpm, wo, gain, bias)


N_TOP = PEER_TOPK + 1
SUBLANES = 8
TOP_ROWS = 24
PAIR_LIMIT = [N_TOP // (a + 1) for a in range(N_TOP)]
N_WIDE = sum(1 for n in PAIR_LIMIT if n > 1)
ROUTE_GROUP = 4


def _sort_network(n):
    def merge(lo, hi, r):
        step = r * 2
        if step < hi - lo:
            yield from merge(lo, hi, step)
            yield from merge(lo + r, hi, step)
            yield from [(i, i + r) for i in range(lo + r, hi - r, step)]
        else:
            yield (lo, lo + r)

    def sort(lo, hi):
        if hi - lo >= 1:
            mid = lo + (hi - lo) // 2
            yield from sort(lo, mid)
            yield from sort(mid + 1, hi)
            yield from merge(lo, hi, 1)

    return list(sort(0, n - 1))


def _sorted_slabs(slabs):
    n = 1 << (len(slabs) - 1).bit_length()
    v = list(slabs) + [None] * (n - len(slabs))
    for i, j in _sort_network(n):
        if v[j] is None:
            continue
        if v[i] is None:
            v[i], v[j] = v[j], None
        else:
            v[i], v[j] = jnp.maximum(v[i], v[j]), jnp.minimum(v[i], v[j])
    return v[:len(slabs)]


def _top_rows(slab_sets, n):
    sets = [list(c) for c in slab_sets]
    vals = [[] for _ in sets]
    for r in range(n):
        for i, cols in enumerate(sets):
            m = jnp.max(cols[0], axis=0, keepdims=True)
            vals[i].append(m)
            hit = cols[0] == m
            keep = min(len(cols), n - r - 1)
            sets[i] = [jnp.where(hit, cols[k + 1] if k + 1 < len(cols) else -jnp.inf, cols[k])
                       for k in range(keep)]
    return vals


def _count_above(rows, thr):
    count = jnp.zeros_like(thr)
    passed = []
    step = PEER_TOPK // 2
    while step >= 1:
        cands = [rows[base + step - 1] for base in range(0, PEER_TOPK, 2 * step)]
        for m in reversed(passed):
            cands = [jnp.where(m, cands[2 * i + 1], cands[2 * i]) for i in range(len(cands) // 2)]
        m = cands[0] > thr
        count = jnp.where(m, count + float(step), count)
        passed.append(m)
        step //= 2
    return jnp.where(rows[PEER_TOPK - 1] > thr, float(PEER_TOPK), count)


def _stack_rows(rows, tokens):
    ri = lax.broadcasted_iota(jnp.int32, (TOP_ROWS, tokens), 0)
    out = jnp.full((TOP_ROWS, tokens), -jnp.inf, F32)
    for r, v in enumerate(rows):
        out = jnp.where(ri == r, v, out)
    return out


def _route_kernel(ht_ref, wq_ref, sk_ref, r2_ref, c1_ref, e1_ref, e2_ref):
    tokens = ht_ref.shape[1]
    ri8 = lax.broadcasted_iota(jnp.int32, (N_WIDE, tokens), 0)
    qp = jnp.dot(wq_ref[...], ht_ref[...], preferred_element_type=F32).astype(BF16)
    for h0 in range(0, PEER_HEADS, ROUTE_GROUP):
        heads = range(h0, h0 + ROUTE_GROUP)
        s = [jnp.dot(sk_ref[hp], qp[hp * PEER_HALF:(hp + 1) * PEER_HALF], preferred_element_type=F32)
             for hp in range(2 * h0, 2 * (h0 + ROUTE_GROUP))]
        slabs = lambda x: [x[k:k + SUBLANES] for k in range(0, x.shape[0], SUBLANES)]
        tops = _top_rows([_sorted_slabs(slabs(x)) for x in s], N_TOP)
        cand = []
        for g in range(ROUTE_GROUP):
            t1, t2 = tops[2 * g], tops[2 * g + 1]
            t1s = _stack_rows(t1, tokens)
            t2s = _stack_rows(t2, tokens)
            parts = [jnp.where(ri8 < min(PAIR_LIMIT[a], N_WIDE), t1[a] + t2s[:N_WIDE], -jnp.inf)
                     for a in range(N_WIDE)]
            parts += slabs(t1[0] + t2s[N_WIDE:]) + slabs(t1s[N_WIDE:] + t2[0])
            cand.append(_sorted_slabs(parts))
        bests = _top_rows(cand, N_TOP)
        for g, h in enumerate(heads):
            s1, s2 = s[2 * g], s[2 * g + 1]
            t1, t2, best = tops[2 * g], tops[2 * g + 1], bests[g]
            m = best[0]
            z = jnp.zeros_like(m)
            for r in range(PEER_TOPK):
                z = z + jnp.exp(best[r] - m)
            thr = 0.5 * (best[PEER_TOPK - 1] + best[PEER_TOPK]) - s1
            c1 = _count_above(t2, thr)
            r2_ref[h] = _count_above(t2, s2).astype(r2_ref.dtype)
            c1_ref[h] = c1
            e1_ref[h] = jnp.exp(s1 - t1[0] - jnp.log(z))
            e2_ref[h] = jnp.exp(s2 - t2[0]).astype(e2_ref.dtype)


def _route(ht, wqt, sk):
    t = ht.shape[1]
    tm = _pick(t, 256)
    out = jax.ShapeDtypeStruct((PEER_HEADS, PEER_NKEYS, t), ht.dtype)
    out32 = jax.ShapeDtypeStruct((PEER_HEADS, PEER_NKEYS, t), F32)
    ospec = pl.BlockSpec((PEER_HEADS, PEER_NKEYS, tm), lambda i: (0, 0, i))
    return pl.pallas_call(
        _route_kernel,
        grid=(t // tm,),
        in_specs=[pl.BlockSpec((D_MODEL, tm), lambda i: (0, i)),
                  pl.BlockSpec(wqt.shape, lambda i: (0, 0)),
                  pl.BlockSpec(sk.shape, lambda i: (0, 0, 0))],
        out_specs=[ospec] * 4,
        out_shape=[out, out32, out32, out],
        compiler_params=_cparams(("parallel",)),
        name="peer_route",
    )(ht, wqt, sk)


GELU_C0 = 0.7978845608028654
GELU_C1 = GELU_C0 * 0.044715


def _gelu(a):
    t = jnp.exp2(a * ((-2.0 * LOG2_E * GELU_C0) + (-2.0 * LOG2_E * GELU_C1) * (a * a)))
    return a / (1.0 + t)


def _expert_kernel(ht_ref, h_ref, u_ref, vt_ref, r2_ref, c1_ref, e1_ref, e2_ref, gain_ref, bias_ref,
                   o_ref, acc_ref, *act_refs, chunk):
    e = pl.program_id(1)
    te, width = act_refs[0].shape
    dtype = act_refs[0].dtype
    packed_rows = 16

    @pl.when(e == 0)
    def _():
        acc_ref[...] = jnp.zeros_like(acc_ref)

    def row_bcast(ref, h, r, ls):
        row = jnp.broadcast_to(ref[h, r:r + 1, ls], (packed_rows, width)).astype(dtype)
        return jnp.concatenate([row] * (PEER_NKEYS // packed_rows), axis=0)

    lanes = [slice(i * width, (i + 1) * width) for i in range(len(act_refs))]
    a_all = [[jnp.dot(u_ref[c * chunk:(c + 1) * chunk, :], ht_ref[:, ls], preferred_element_type=F32)
              for c in range(te // chunk)] for ls in lanes]
    for act_ref, ls, a_chunks in zip(act_refs, lanes, a_all):
        for c, a in enumerate(a_chunks):
            for rr in range(chunk // PEER_NKEYS):
                r = c * (chunk // PEER_NKEYS) + rr
                g = None
                for h in range(PEER_HEADS):
                    gate = jnp.where(r2_ref[h, :, ls] < row_bcast(c1_ref, h, r, ls),
                                     e2_ref[h, :, ls] * row_bcast(e1_ref, h, r, ls), 0.0)
                    g = gate if g is None else g + gate
                a_r = a[rr * PEER_NKEYS:(rr + 1) * PEER_NKEYS, :]
                act_ref[r * PEER_NKEYS:(r + 1) * PEER_NKEYS, :] = _gelu(a_r.astype(dtype)) * g
        acc_ref[:, ls] += jnp.dot(vt_ref[...], act_ref[...], preferred_element_type=F32)

    @pl.when(e == pl.num_programs(1) - 1)
    def _():
        z = ALPHA * h_ref[...] + acc_ref[...].T
        o_ref[...] = _layer_norm(z, gain_ref[...], bias_ref[...])


def _experts(ht, h1, u, vt, r2, c1, e1, e2, gain, bias, tm=512, te=2048, chunk=512):
    t = h1.shape[0]
    tm = _pick(t, tm)
    hspec = pl.BlockSpec((PEER_HEADS, PEER_NKEYS, tm), lambda i, e: (0, 0, i))
    rspec = pl.BlockSpec((PEER_HEADS, te // PEER_NKEYS, tm), lambda i, e: (0, e, i))
    vec = pl.BlockSpec((1, D_MODEL), lambda i, e: (0, 0))
    return pl.pallas_call(
        functools.partial(_expert_kernel, chunk=chunk),
        grid=(t // tm, PEER_N // te),
        in_specs=[pl.BlockSpec((D_MODEL, tm), lambda i, e: (0, i)),
                  pl.BlockSpec((tm, D_MODEL), lambda i, e: (i, 0)),
                  pl.BlockSpec((te, D_MODEL), lambda i, e: (e, 0)),
                  pl.BlockSpec((D_MODEL, te), lambda i, e: (0, e)),
                  hspec, rspec, rspec, hspec, vec, vec],
        out_specs=pl.BlockSpec((tm, D_MODEL), lambda i, e: (i, 0)),
        out_shape=jax.ShapeDtypeStruct((t, D_MODEL), F32),
        scratch_shapes=[pltpu.VMEM((D_MODEL, tm), F32)] + [pltpu.VMEM((te, MXU_WIDTH), ht.dtype)] * (tm // MXU_WIDTH),
        compiler_params=_cparams(("parallel", "arbitrary")),
        name="peer_experts",
    )(ht, h1, u, vt, r2, c1, e1, e2, gain, bias)


def _rope_tables(pos):
    half = MLA_ROPE // 2
    inv_freq = ROPE_BASE ** (-jnp.arange(half, dtype=F32) / half)
    ang = pos.astype(F32)[:, None] * inv_freq[None, :]
    zeros = jnp.zeros((pos.shape[0], MLA_NOPE - MLA_ROPE), F32)
    cos, sin = jnp.cos(ang), jnp.sin(ang)
    return jnp.concatenate([cos, cos, zeros], axis=1), jnp.concatenate([sin, sin, zeros], axis=1)


def _rot_cols(w):
    half = MLA_ROPE // 2
    return jnp.concatenate([-w[..., half:], w[..., :half]], axis=-1)


def kernel(x, meta_tokens, hgrn_lb_logits, w_in, q_norm_gain, kv_norm_gain, w_uq, w_ukv, hgrn_norm_gain,
           w_proj_hgrn, w_proj_mla, w_out, ln1_gain, ln1_bias, peer_query, peer_sub_keys, peer_u, peer_v,
           ln2_gain, ln2_bias):
    batch, seq, d = x.shape
    p_rows = LEAD + seq
    t_real = batch * seq
    l = 0

    h_lead = jnp.concatenate([jnp.zeros((N_PAD, d), BF16), meta_tokens.astype(BF16)], axis=0)
    x2 = x.reshape(t_real, d)

    lower_bounds = jnp.cumsum(jax.nn.softmax(hgrn_lb_logits.astype(F32), axis=0), axis=0)
    rows = jnp.arange(p_rows)
    cos_all, sin_all = _rope_tables(rows - N_PAD)

    w = w_in[l]
    o_hg = 4 * HG_WIDTH
    o_cq = o_hg + MLA_Q_RANK
    o_ckv = o_cq + MLA_KV_RANK
    o_kr = o_ckv + MLA_ROPE
    w_kr = w[:, o_ckv:o_kr]
    zpad = jnp.zeros((d, MLA_NOPE - MLA_ROPE), F32)
    w_all = jnp.concatenate([w[:, :o_hg],
                             w[:, o_cq:o_ckv], w_kr, zpad, _rot_cols(w_kr), zpad,
                             w[:, o_hg:o_cq], jnp.zeros((d, MLA_Q_PAD - MLA_Q_RANK), F32),
                             w[:, o_kr:]], axis=1).astype(BF16)
    n_lead_cols = o_hg + MLA_KV_RANK + 2 * MLA_NOPE

    wq3 = w_uq[l].reshape(MLA_Q_RANK, MLA_HEADS, MLA_NOPE + MLA_ROPE)
    wq_rope = wq3[..., MLA_NOPE:]
    wq = jnp.concatenate([wq3[..., :MLA_NOPE], wq_rope, _rot_cols(wq_rope)], axis=-1)
    wq = wq.transpose(1, 0, 2).astype(BF16)
    wkv = w_ukv[l].reshape(MLA_KV_RANK, MLA_HEADS, MLA_NOPE + MLA_V).transpose(1, 0, 2).astype(BF16)

    proj = _matmul(x2, w_all, _pick(t_real, 1024), w_all.shape[1] // 2)
    proj_lead = _matmul(h_lead, w_all[:, :n_lead_cols], LEAD, n_lead_cols // 3)

    lb, hg_gain = lower_bounds[l][None, :], hgrn_norm_gain[l][None, :]
    _, lead_state = _hgrn(proj_lead, jnp.zeros((HG_HEADS, HG_D, HG_D), F32), lb, hg_gain, 1, LEAD, n_inert=N_PAD)
    hg_o, _ = _hgrn(proj, lead_state[0], lb, hg_gain, batch, seq)

    q = _qprep(proj, q_norm_gain[l][None, :], wq, cos_all[LEAD:], sin_all[LEAD:], batch, seq)
    k, v = _kvprep(proj, kv_norm_gain[l][None, :], wkv, cos_all[LEAD:], sin_all[LEAD:], batch, seq)
    k_lead, v_lead = _kvprep(proj_lead, kv_norm_gain[l][None, :], wkv, cos_all[:LEAD], sin_all[:LEAD], 1, LEAD)
    at_o = _flash(q, k, v, k_lead, v_lead, batch, seq)

    h1, h1t = _mix(x2, hg_o.reshape(t_real, HG_WIDTH), at_o.reshape(t_real, MLA_HEADS * MLA_V), proj,
                   w_proj_hgrn[l].astype(BF16), w_proj_mla[l].astype(BF16), w_out[l].astype(BF16),
                   ln1_gain[l][None, :], ln1_bias[l][None, :])

    wqt = peer_query[l].T.astype(BF16)
    sk = peer_sub_keys[l].reshape(PEER_HEADS * 2, PEER_NKEYS, PEER_HALF).astype(BF16)
    r2, c1, e1, e2 = _route(h1t, wqt, sk)

    out = _experts(h1t, h1, peer_u[l].astype(BF16), peer_v[l].T.astype(BF16), r2, c1, e1, e2,
                   ln2_gain[l][None, :], ln2_bias[l][None, :])
    return out.reshape(batch, seq, d)
```
